```python
import jax, jax.numpy as jnp
from jax import lax
import numpy as np

D_MODEL = 2048
BATCH = 4
SEQ = 2048
DEPTH = 4
DEC_BATCH = 128
DEC_SEQ = 8
PAST_LEN = 16384
PAGE_SIZE = 128

HEAD_DIM = 128
CONV_WIDTH = (D_MODEL * 3) // 8
POOL_WIDTH = D_MODEL // 4
CHUNK_WIDTH = D_MODEL - CONV_WIDTH - POOL_WIDTH
CONV_K = 3
CHUNK = 128
N_CHUNK_HEADS = CHUNK_WIDTH // HEAD_DIM
POOL_WINDOWS = (2, 4, 8, 16)
N_POOL_GROUPS = len(POOL_WINDOWS)
POOL_GROUP_DIM = POOL_WIDTH // N_POOL_GROUPS
POOL_BUF = max(POOL_WINDOWS) - 1
IN_WIDTH = 3 * CONV_WIDTH + 2 * CHUNK_WIDTH + POOL_WIDTH
N_EXPERT_GROUPS = 4
EXPERTS_PER_GROUP = 4
N_EXPERTS = N_EXPERT_GROUPS * EXPERTS_PER_GROUP
TOP_K_EXPERT = 2
D_EXPERT = D_MODEL // 4
ALPHA = (2 * DEPTH) ** 0.25
BETA = (8 * DEPTH) ** -0.25
LN_EPS = 1e-5

kernel_name = "hybrid_conv_chunkmlp_pool_hmoe_step"


def layer_norm(x, g, b):
    xf = x.astype(jnp.float32)
    mu = jnp.mean(xf, axis=-1, keepdims=True)
    var = jnp.mean(jnp.square(xf - mu), axis=-1, keepdims=True)
    y = (xf - mu) * lax.rsqrt(var + LN_EPS) * g.astype(jnp.float32) + b.astype(jnp.float32)
    return y.astype(x.dtype)


def short_conv(z, buf, w):
    T = z.shape[1]
    ext = jnp.concatenate([buf.astype(z.dtype), z], axis=1)
    y = w[0] * ext[:, 0:T]
    for k in range(1, CONV_K):
        y = y + w[k] * ext[:, k:k + T]
    return y, ext[:, T:]


def chunk_mix(vn, w_s, b_s):
    B, T, _ = vn.shape
    L = min(T, CHUNK)
    nc = T // L
    vc = vn.reshape(B, nc, L, N_CHUNK_HEADS, HEAD_DIM)
    ws = jnp.tril(w_s[:, :L, :L])
    mixed = jnp.einsum('hts,bcshd->bcthd', ws, vc) + jnp.transpose(b_s[:, :L])[None, None, :, :, None]
    return mixed.reshape(B, T, CHUNK_WIDTH)


def multiscale_pool(p, buf, pos0, pool_w, pool_scale):
    B, T, _ = p.shape
    ext = jnp.concatenate([buf.astype(p.dtype), p], axis=1)
    new_buf = ext[:, T:]
    extf = ext.astype(jnp.float32).reshape(B, POOL_BUF + T, N_POOL_GROUPS, POOL_GROUP_DIM)
    cs = jnp.concatenate([jnp.zeros((B, 1, N_POOL_GROUPS, POOL_GROUP_DIM), jnp.float32),
                          jnp.cumsum(extf, axis=1)], axis=1)
    end = cs[:, POOL_BUF + 1:]
    pos = pos0 + jnp.arange(T)
    means = []
    for g, w in enumerate(POOL_WINDOWS):
        start = cs[:, POOL_BUF + 1 - w:POOL_BUF + 1 - w + T, g]
        cnt = jnp.minimum(w, pos + 1).astype(jnp.float32)[None, :, None]
        means.append((end[:, :, g] - start) / cnt)
    pooled = jnp.stack(means, axis=2)
    diff = (pooled - extf[:, POOL_BUF:]).astype(p.dtype)
    y = jnp.einsum('btgc,gcd->btgd', diff, pool_w).reshape(B, T, POOL_WIDTH) * pool_scale
    return y, new_buf


def token_mixer(x, conv_buf, pool_buf, pos0, w_in, conv_w, sgu_ln_g, sgu_ln_b, sgu_w, sgu_b,
                pool_w, pool_scale, w_out):
    h = jnp.einsum('btd,de->bte', x, w_in)
    cuts = [CONV_WIDTH, 2 * CONV_WIDTH, 3 * CONV_WIDTH, 3 * CONV_WIDTH + CHUNK_WIDTH,
            3 * CONV_WIDTH + 2 * CHUNK_WIDTH]
    b_gate, c_gate, h_conv, u, v, p = jnp.split(h, cuts, axis=-1)
    conv_out, new_conv = short_conv(c_gate * h_conv, conv_buf, conv_w)
    y_conv = b_gate * conv_out
    vn = layer_norm(v, sgu_ln_g, sgu_ln_b)
    y_chunk = u * chunk_mix(vn, sgu_w, sgu_b)
    y_pool, new_pool = multiscale_pool(p, pool_buf, pos0, pool_w, pool_scale)
    y = jnp.concatenate([y_conv, y_chunk, y_pool], axis=-1)
    return jnp.einsum('bte,ed->btd', y, w_out), new_conv, new_pool, vn


def hier_moe(x, rg_w, rg_b, re_w, re_b, w_gate, w_up, w_down):
    B, T, D = x.shape
    xt = x.reshape(B * T, D)
    g_logits = (xt @ rg_w).astype(jnp.float32) + rg_b.astype(jnp.float32)
    g_prob = jax.nn.softmax(g_logits, axis=-1)
    g_val, g_idx = lax.top_k(g_prob, 1)
    e_logits = ((xt @ re_w).astype(jnp.float32) + re_b.astype(jnp.float32)).reshape(
        -1, N_EXPERT_GROUPS, EXPERTS_PER_GROUP)
    e_sel = jnp.take_along_axis(e_logits, g_idx[:, :, None], axis=1)[:, 0]
    e_prob = jax.nn.softmax(e_sel, axis=-1)
    e_val, e_idx = lax.top_k(e_prob, TOP_K_EXPERT)
    e_val = e_val / jnp.sum(e_val, axis=-1, keepdims=True)
    gates = g_val * e_val
    ids = g_idx * EXPERTS_PER_GROUP + e_idx
    combine = jnp.sum(jax.nn.one_hot(ids, N_EXPERTS, dtype=jnp.float32) * gates[..., None], axis=1)
    hg = jnp.einsum('nd,edf->nef', xt, w_gate)
    hu = jnp.einsum('nd,edf->nef', xt, w_up)
    a = jax.nn.silu(hg) * hu * combine[:, :, None].astype(x.dtype)
    out = jnp.einsum('nef,efd->nd', a, w_down)
    return out.reshape(B, T, D)


def trunk_layer(x, conv_buf, pool_buf, pos0, w_in, conv_w, sgu_ln_g, sgu_ln_b, sgu_w, sgu_b,
                pool_w, pool_scale, w_out, ln1_g, ln1_b, rg_w, rg_b, re_w, re_b,
                w_gate, w_up, w_down, ln2_g, ln2_b):
    mix, new_conv, new_pool, vn = token_mixer(x, conv_buf, pool_buf, pos0, w_in, conv_w, sgu_ln_g,
                                              sgu_ln_b, sgu_w, sgu_b, pool_w, pool_scale, w_out)
    x = layer_norm(ALPHA * x + mix, ln1_g, ln1_b)
    x = layer_norm(ALPHA * x + hier_moe(x, rg_w, rg_b, re_w, re_b, w_gate, w_up, w_down), ln2_g, ln2_b)
    return x, new_conv, new_pool, vn


def setup_inputs(seed: int = 0) -> dict:
    key = jax.random.key(seed)
    ks = jax.random.split(key, 26)
    f = jnp.float32
    nrm = lambda k, s: jax.random.normal(k, s, f)
    return {
        "x_prompt": nrm(ks[0], (BATCH, SEQ, D_MODEL)),
        "x_sample": nrm(ks[1], (DEC_BATCH, DEC_SEQ, D_MODEL)),
        "state_conv": nrm(ks[2], (DEPTH, DEC_BATCH, CONV_K - 1, CONV_WIDTH)),
        "state_pool": nrm(ks[3], (DEPTH, DEC_BATCH, POOL_BUF, POOL_WIDTH)),
        "w_in": nrm(ks[4], (DEPTH, D_MODEL, IN_WIDTH)) * D_MODEL ** -0.5,
        "conv_w": nrm(ks[5], (DEPTH, CONV_K, CONV_WIDTH)) * CONV_K ** -0.5,
        "sgu_ln_g": 1.0 + 0.1 * nrm(ks[6], (DEPTH, CHUNK_WIDTH)),
        "sgu_ln_b": 0.01 * nrm(ks[7], (DEPTH, CHUNK_WIDTH)),
        "sgu_w": nrm(ks[8], (DEPTH, N_CHUNK_HEADS, CHUNK, CHUNK)) * CHUNK ** -0.5,
        "sgu_b": 1.0 + 0.1 * nrm(ks[9], (DEPTH, N_CHUNK_HEADS, CHUNK)),
        "pool_w": nrm(ks[10], (DEPTH, N_POOL_GROUPS, POOL_GROUP_DIM, POOL_GROUP_DIM)) * POOL_GROUP_DIM ** -0.5,
        "pool_scale": 1.0 + 0.1 * nrm(ks[11], (DEPTH, POOL_WIDTH)),
        "w_out": nrm(ks[12], (DEPTH, D_MODEL, D_MODEL)) * (D_MODEL ** -0.5 * BETA),
        "ln1_g": 1.0 + 0.1 * nrm(ks[13], (DEPTH, D_MODEL)),
        "ln1_b": 0.01 * nrm(ks[14], (DEPTH, D_MODEL)),
        "router_group_w": nrm(ks[15], (DEPTH, D_MODEL, N_EXPERT_GROUPS)) * D_MODEL ** -0.5,
        "router_group_b": 0.01 * nrm(ks[16], (DEPTH, N_EXPERT_GROUPS)),
        "router_expert_w": nrm(ks[17], (DEPTH, D_MODEL, N_EXPERTS)) * D_MODEL ** -0.5,
        "router_expert_b": 0.01 * nrm(ks[18], (DEPTH, N_EXPERTS)),
        "expert_w_gate": nrm(ks[19], (DEPTH, N_EXPERTS, D_MODEL, D_EXPERT)) * D_MODEL ** -0.5,
        "expert_w_up": nrm(ks[20], (DEPTH, N_EXPERTS, D_MODEL, D_EXPERT)) * D_MODEL ** -0.5,
        "expert_w_down": nrm(ks[21], (DEPTH, N_EXPERTS, D_EXPERT, D_MODEL)) * (D_EXPERT ** -0.5 * BETA),
        "ln2_g": 1.0 + 0.1 * nrm(ks[22], (DEPTH, D_MODEL)),
        "ln2_b": 0.01 * nrm(ks[23], (DEPTH, D_MODEL)),
    }


def reference(x_prompt, x_sample, state_conv, state_pool, w_in, conv_w, sgu_ln_g, sgu_ln_b, sgu_w,
              sgu_b, pool_w, pool_scale, w_out, ln1_g, ln1_b, router_group_w, router_group_b,
              router_expert_w, router_expert_b, expert_w_gate, expert_w_up, expert_w_down,
              ln2_g, ln2_b):
    bp = x_prompt.shape[0]
    zero_conv = jnp.zeros((bp, CONV_K - 1, CONV_WIDTH), x_prompt.dtype)
    zero_pool = jnp.zeros((bp, POOL_BUF, POOL_WIDTH), x_prompt.dtype)
    yp, ys = x_prompt, x_sample
    conv_p, pool_p, conv_s, pool_s, v_s = [], [], [], [], []
    for l in range(DEPTH):
        params = (w_in[l], conv_w[l], sgu_ln_g[l], sgu_ln_b[l], sgu_w[l], sgu_b[l], pool_w[l],
                  pool_scale[l], w_out[l], ln1_g[l], ln1_b[l], router_group_w[l], router_group_b[l],
                  router_expert_w[l], router_expert_b[l], expert_w_gate[l], expert_w_up[l],
                  expert_w_down[l], ln2_g[l], ln2_b[l])
        yp, cp, pp, _vp = trunk_layer(yp, zero_conv, zero_pool, 0, *params)
        ys, cs_, ps_, vs_ = trunk_layer(ys, state_conv[l], state_pool[l], PAST_LEN, *params)
        conv_p.append(cp)
        pool_p.append(pp)
        conv_s.append(cs_)
        pool_s.append(ps_)
        v_s.append(vs_)
    new_conv_prompt = jnp.stack(conv_p, axis=0)
    new_pool_prompt = jnp.stack(pool_p, axis=0)
    new_conv_sample = jnp.stack(conv_s, axis=0)
    new_pool_sample = jnp.stack(pool_s, axis=0)
    new_chunk_v_sample = jnp.stack(v_s, axis=0)
    return (yp, ys, new_conv_prompt, new_pool_prompt, new_conv_sample, new_pool_sample, new_chunk_v_sample)
```

```python
import functools

import jax
import jax.numpy as jnp
from jax import lax
from jax.experimental import pallas as pl
from jax.experimental.pallas import tpu as pltpu

D_MODEL = 2048
BATCH = 4
SEQ = 2048
DEPTH = 4
DEC_BATCH = 128
DEC_SEQ = 8
PAST_LEN = 16384
HEAD_DIM = 128
CONV_WIDTH = 768
POOL_WIDTH = 512
CHUNK_WIDTH = 768
CHUNK = 128
N_CHUNK_HEADS = 6
POOL_WINDOWS = (2, 4, 8, 16)
POOL_GROUP_DIM = 128
POOL_BUF = 15
IN_WIDTH = 4352
N_EXPERT_GROUPS = 4
EXPERTS_PER_GROUP = 4
N_EXPERTS = 16
D_EXPERT = 512
ALPHA = (2 * DEPTH) ** 0.25
LN_EPS = 1e-5

N_PROMPT = BATCH * SEQ
N_SAMPLE = DEC_BATCH * DEC_SEQ
N_TOK = N_PROMPT + N_SAMPLE
N_ASSIGN = 2 * N_TOK

LANES = 128
SUBLANES = 8
CONV_HALO = 8
POOL_HALO = 16

TM = 256
TILES_PER_SEQ = SEQ // TM
N_PROMPT_TILES = N_PROMPT // TM
N_SAMPLE_TILES = N_SAMPLE // TM
N_TILES = N_PROMPT_TILES + N_SAMPLE_TILES
SEQS_PER_TILE = TM // DEC_SEQ

TE = 256
N_SLOTS = N_ASSIGN + N_EXPERTS * TE
N_ETILES = N_SLOTS // TE
ASSIGN_ROWS = N_ASSIGN // LANES
DISPATCH_CHUNK = 512

CUT_B, CUT_C, CUT_H, CUT_U, CUT_V, CUT_P = 0, 768, 1536, 2304, 3072, 3840

VMEM_LIMIT = 56 * 1024 * 1024

_f32 = jnp.float32
_bf16 = jnp.bfloat16


def _dot(a, b):
    return jnp.dot(a, b, preferred_element_type=_f32)


def _layer_norm(r, g, b):
    mu = jnp.mean(r, axis=-1, keepdims=True)
    c = r - mu
    var = jnp.mean(c * c, axis=-1, keepdims=True)
    return c * lax.rsqrt(var + LN_EPS) * g + b


def _mixer_body(i, xb, w_in_ref, conv_w_ref, sg_ref, sb_ref, ws_ref, bias_ref, pw_ref, ps_ref,
                zext_ref, pext_ref, ybuf_ref, seq_len, pos, chunk_len):
    n_seq = TM // seq_len

    z = _dot(xb, w_in_ref[:, CUT_C:CUT_H]) * _dot(xb, w_in_ref[:, CUT_H:CUT_U])
    zext_ref[:, CONV_HALO:, :] = z.reshape(n_seq, seq_len, CONV_WIDTH)
    cw = conv_w_ref[...]
    conv = (cw[0:1, :] * zext_ref[:, CONV_HALO - 2:CONV_HALO - 2 + seq_len, :]
            + cw[1:2, :] * zext_ref[:, CONV_HALO - 1:CONV_HALO - 1 + seq_len, :]
            + cw[2:3, :] * zext_ref[:, CONV_HALO:, :])
    y_conv = _dot(xb, w_in_ref[:, CUT_B:CUT_C]) * conv.reshape(TM, CONV_WIDTH)
    ybuf_ref[:, 0:CONV_WIDTH] = y_conv.astype(_bf16)

    v = _dot(xb, w_in_ref[:, CUT_V:CUT_P])
    vn = _layer_norm(v, sg_ref[...], sb_ref[...])
    vnb = vn.astype(_bf16)
    r_i = lax.broadcasted_iota(jnp.int32, (CHUNK, CHUNK), 0)
    c_i = lax.broadcasted_iota(jnp.int32, (CHUNK, CHUNK), 1)
    shift = chunk_len.bit_length() - 1
    mask = (c_i <= r_i) & ((r_i >> shift) == (c_i >> shift))
    for h in range(N_CHUNK_HEADS):
        m_h = jnp.where(mask, ws_ref[h], 0.0).astype(_bf16)
        b_h = bias_ref[:, h:h + 1]
        cols = slice(h * HEAD_DIM, (h + 1) * HEAD_DIM)
        u_h = _dot(xb, w_in_ref[:, CUT_U + h * HEAD_DIM:CUT_U + (h + 1) * HEAD_DIM])
        for c in range(TM // CHUNK):
            rows = slice(c * CHUNK, (c + 1) * CHUNK)
            mixed = _dot(m_h, vnb[rows, cols]) + b_h
            ybuf_ref[rows, CONV_WIDTH + h * HEAD_DIM:CONV_WIDTH + (h + 1) * HEAD_DIM] = (
                u_h[rows, :] * mixed).astype(_bf16)

    p = _dot(xb, w_in_ref[:, CUT_P:IN_WIDTH])
    pext_ref[:, POOL_HALO:, :] = p.reshape(n_seq, seq_len, POOL_WIDTH)
    posf = (pos + 1).astype(_f32)
    for g, w in enumerate(POOL_WINDOWS):
        cols = slice(g * POOL_GROUP_DIM, (g + 1) * POOL_GROUP_DIM)
        s = pext_ref[:, POOL_HALO:, cols]
        for j in range(1, w):
            s = s + pext_ref[:, POOL_HALO - j:POOL_HALO - j + seq_len, cols]
        cnt = jnp.minimum(float(w), posf)
        diff = s.reshape(TM, POOL_GROUP_DIM) / cnt - p[:, cols]
        y_g = _dot(diff.astype(_bf16), pw_ref[g].astype(_bf16)) * ps_ref[:, cols]
        ybuf_ref[:, CONV_WIDTH + CHUNK_WIDTH + g * POOL_GROUP_DIM:
                 CONV_WIDTH + CHUNK_WIDTH + (g + 1) * POOL_GROUP_DIM] = y_g.astype(_bf16)
    return z, p, vn


def _route(x1, wr_hi_ref, wr_lo_ref, br_ref):
    x_hi = x1.astype(_bf16)
    x_lo = (x1 - x_hi.astype(_f32)).astype(_bf16)
    logits = (_dot(x_hi, wr_hi_ref[...]) + _dot(x_lo, wr_hi_ref[...])
              + _dot(x_hi, wr_lo_ref[...]) + br_ref[...])
    lane = lax.broadcasted_iota(jnp.int32, logits.shape, 1)
    neg = -jnp.inf
    big = jnp.int32(1 << 20)

    gmask = lane < N_EXPERT_GROUPS
    gl = jnp.where(gmask, logits, neg)
    ge = jnp.exp(gl - jnp.max(gl, axis=1, keepdims=True))
    gp = ge / jnp.sum(ge, axis=1, keepdims=True)
    g_val = jnp.max(gp, axis=1, keepdims=True)
    g_idx = jnp.min(jnp.where(gmask & (gp == g_val), lane, big), axis=1, keepdims=True)

    e_lane = lane - N_EXPERT_GROUPS
    emask = (e_lane >= 0) & (e_lane < N_EXPERTS) & ((e_lane >> 2) == g_idx)
    el = jnp.where(emask, logits, neg)
    ee = jnp.exp(el - jnp.max(el, axis=1, keepdims=True))
    ep = ee / jnp.sum(ee, axis=1, keepdims=True)
    v1 = jnp.max(jnp.where(emask, ep, -1.0), axis=1, keepdims=True)
    i1 = jnp.min(jnp.where(emask & (ep == v1), e_lane, big), axis=1, keepdims=True)
    mask2 = emask & (e_lane != i1)
    v2 = jnp.max(jnp.where(mask2, ep, -1.0), axis=1, keepdims=True)
    i2 = jnp.min(jnp.where(mask2 & (ep == v2), e_lane, big), axis=1, keepdims=True)
    tot = v1 + v2
    gate1 = g_val * (v1 / tot)
    gate2 = g_val * (v2 / tot)
    return jnp.where(lane == 0, i1.astype(_f32),
                     jnp.where(lane == 1, i2.astype(_f32),
                               jnp.where(lane == 2, gate1, jnp.where(lane == 3, gate2, 0.0))))


def _mixer_kernel(x_ref, w_in_ref, w_out_ref, conv_w_ref, sg_ref, sb_ref, ws_p_ref, ws_s_ref,
                  bias_p_ref, bias_s_ref, pw_ref, ps_ref, g1_ref, b1_ref, wr_hi_ref, wr_lo_ref,
                  br_ref, zst_ref, pst_ref,
                  x1_ref, route_ref, ztail_ref, ptail_ref, zs_ref, psamp_ref, vns_ref,
                  zext_p, pext_p, zext_s, pext_s, ybuf_ref):
    i = pl.program_id(0)
    x = x_ref[...]
    xb = x.astype(_bf16)
    row = lax.broadcasted_iota(jnp.int32, (TM, 1), 0)

    @pl.when(i < N_PROMPT_TILES)
    def _prompt():
        t0 = (i % TILES_PER_SEQ) * TM

        @pl.when(t0 == 0)
        def _():
            zext_p[:, 0:CONV_HALO, :] = jnp.zeros((1, CONV_HALO, CONV_WIDTH), _f32)
            pext_p[:, 0:POOL_HALO, :] = jnp.zeros((1, POOL_HALO, POOL_WIDTH), _f32)

        z, p, _ = _mixer_body(i, xb, w_in_ref, conv_w_ref, sg_ref, sb_ref, ws_p_ref, bias_p_ref,
                              pw_ref, ps_ref, zext_p, pext_p, ybuf_ref, TM, t0 + row, CHUNK)
        z_last = z[TM - CONV_HALO:, :].reshape(1, CONV_HALO, CONV_WIDTH)
        p_last = p[TM - POOL_HALO:, :].reshape(1, POOL_HALO, POOL_WIDTH)
        zext_p[:, 0:CONV_HALO, :] = z_last
        pext_p[:, 0:POOL_HALO, :] = p_last
        ztail_ref[...] = z_last
        ptail_ref[...] = p_last

    @pl.when(i >= N_PROMPT_TILES)
    def _sample():
        zext_s[:, 0:CONV_HALO, :] = zst_ref[...]
        pext_s[:, 0:POOL_HALO, :] = pst_ref[...]
        z, p, vn = _mixer_body(i, xb, w_in_ref, conv_w_ref, sg_ref, sb_ref, ws_s_ref, bias_s_ref,
                               pw_ref, ps_ref, zext_s, pext_s, ybuf_ref, DEC_SEQ,
                               PAST_LEN + (row & (DEC_SEQ - 1)), DEC_SEQ)
        zs_ref[...] = z.reshape(SEQS_PER_TILE, DEC_SEQ, CONV_WIDTH)
        psamp_ref[...] = p.reshape(SEQS_PER_TILE, DEC_SEQ, POOL_WIDTH)
        vns_ref[...] = vn.reshape(SEQS_PER_TILE, DEC_SEQ, CHUNK_WIDTH)
        ztail_ref[...] = jnp.zeros(ztail_ref.shape, _f32)
        ptail_ref[...] = jnp.zeros(ptail_ref.shape, _f32)

    mix = _dot(ybuf_ref[...], w_out_ref[...])
    x1 = _layer_norm(ALPHA * x + mix, g1_ref[...], b1_ref[...])
    x1_ref[...] = x1
    route_ref[...] = _route(x1, wr_hi_ref, wr_lo_ref, br_ref)


def _const_spec(shape):
    nd = len(shape)
    return pl.BlockSpec(shape, lambda i: (0,) * nd, pipeline_mode=pl.Buffered(1))


def _mixer_call(x, w_in_b, w_out_b, conv_w, sg, sb, ws_p, ws_s, bias_p, bias_s, pw, ps, g1, b1,
                wr_hi, wr_lo, br, zst, pst):
    samp = lambda i: (jnp.maximum(i - N_PROMPT_TILES, 0), 0, 0)
    tail = lambda i: (i // TILES_PER_SEQ, 0, 0)
    tok = lambda i: (i, 0)
    in_specs = [
        pl.BlockSpec((TM, D_MODEL), tok),
        _const_spec((D_MODEL, IN_WIDTH)),
        _const_spec((D_MODEL, D_MODEL)),
        _const_spec((3, CONV_WIDTH)),
        _const_spec((1, CHUNK_WIDTH)),
        _const_spec((1, CHUNK_WIDTH)),
        _const_spec((N_CHUNK_HEADS, CHUNK, CHUNK)),
        _const_spec((N_CHUNK_HEADS, CHUNK, CHUNK)),
        _const_spec((CHUNK, N_CHUNK_HEADS)),
        _const_spec((CHUNK, N_CHUNK_HEADS)),
        _const_spec((4, POOL_GROUP_DIM, POOL_GROUP_DIM)),
        _const_spec((1, POOL_WIDTH)),
        _const_spec((1, D_MODEL)),
        _const_spec((1, D_MODEL)),
        _const_spec((D_MODEL, LANES)),
        _const_spec((D_MODEL, LANES)),
        _const_spec((1, LANES)),
        pl.BlockSpec((SEQS_PER_TILE, CONV_HALO, CONV_WIDTH), samp),
        pl.BlockSpec((SEQS_PER_TILE, POOL_HALO, POOL_WIDTH), samp),
    ]
    out_specs = [
        pl.BlockSpec((TM, D_MODEL), tok),
        pl.BlockSpec((TM, LANES), tok),
        pl.BlockSpec((1, CONV_HALO, CONV_WIDTH), tail),
        pl.BlockSpec((1, POOL_HALO, POOL_WIDTH), tail),
        pl.BlockSpec((SEQS_PER_TILE, DEC_SEQ, CONV_WIDTH), samp),
        pl.BlockSpec((SEQS_PER_TILE, DEC_SEQ, POOL_WIDTH), samp),
        pl.BlockSpec((SEQS_PER_TILE, DEC_SEQ, CHUNK_WIDTH), samp),
    ]
    out_shape = [
        jax.ShapeDtypeStruct((N_TOK, D_MODEL), _f32),
        jax.ShapeDtypeStruct((N_TOK, LANES), _f32),
        jax.ShapeDtypeStruct((BATCH + 1, CONV_HALO, CONV_WIDTH), _f32),
        jax.ShapeDtypeStruct((BATCH + 1, POOL_HALO, POOL_WIDTH), _f32),
        jax.ShapeDtypeStruct((DEC_BATCH, DEC_SEQ, CONV_WIDTH), _f32),
        jax.ShapeDtypeStruct((DEC_BATCH, DEC_SEQ, POOL_WIDTH), _f32),
        jax.ShapeDtypeStruct((DEC_BATCH, DEC_SEQ, CHUNK_WIDTH), _f32),
    ]
    scratch = [
        pltpu.VMEM((1, CONV_HALO + TM, CONV_WIDTH), _f32),
        pltpu.VMEM((1, POOL_HALO + TM, POOL_WIDTH), _f32),
        pltpu.VMEM((SEQS_PER_TILE, CONV_HALO + DEC_SEQ, CONV_WIDTH), _f32),
        pltpu.VMEM((SEQS_PER_TILE, POOL_HALO + DEC_SEQ, POOL_WIDTH), _f32),
        pltpu.VMEM((TM, D_MODEL), _bf16),
    ]
    return pl.pallas_call(
        _mixer_kernel,
        grid=(N_TILES,),
        in_specs=in_specs,
        out_specs=out_specs,
        out_shape=out_shape,
        scratch_shapes=scratch,
        compiler_params=pltpu.CompilerParams(
            dimension_semantics=("arbitrary",), vmem_limit_bytes=VMEM_LIMIT),
        name="mixer",
    )(x, w_in_b, w_out_b, conv_w, sg, sb, ws_p, ws_s, bias_p, bias_s, pw, ps, g1, b1,
      wr_hi, wr_lo, br, zst, pst)


def _slots_kernel(ids_ref, slot_ref, tile_expert_ref, n_used_ref):
    ids = ids_ref[...]
    r_i = lax.broadcasted_iota(jnp.int32, (LANES, LANES), 0)
    c_i = lax.broadcasted_iota(jnp.int32, (LANES, LANES), 1)
    upper = (r_i < c_i).astype(_bf16)
    rr = lax.broadcasted_iota(jnp.int32, (ASSIGN_ROWS, ASSIGN_ROWS), 0)
    rc = lax.broadcasted_iota(jnp.int32, (ASSIGN_ROWS, ASSIGN_ROWS), 1)
    lower = (rc < rr).astype(_bf16)
    tile_row0 = (lax.broadcasted_iota(jnp.int32, (1, LANES), 1) * TE).astype(_f32)

    slot = jnp.zeros((ASSIGN_ROWS, LANES), _f32)
    tile_expert = jnp.zeros((1, LANES), _f32)
    off = jnp.zeros((1, 1), _f32)
    for e in range(N_EXPERTS):
        m = (ids == e).astype(_f32)
        within = _dot(m.astype(_bf16), upper)
        rowsum = jnp.sum(m, axis=1, keepdims=True)
        rowpre = _dot(lower, jnp.broadcast_to(rowsum, (ASSIGN_ROWS, LANES)).astype(_bf16))
        cnt = jnp.sum(rowsum, axis=0, keepdims=True)
        padded = jnp.floor((cnt + (TE - 1)) * (1.0 / TE)) * TE
        slot = slot + m * (off + rowpre + within)
        in_seg = (tile_row0 >= off) & (tile_row0 < off + padded)
        tile_expert = tile_expert + jnp.where(in_seg, float(e), 0.0)
        off = off + padded
    tile_expert = jnp.where(tile_row0 >= off, float(N_EXPERTS - 1), tile_expert)
    slot_ref[...] = slot.astype(jnp.int32)
    tile_expert_ref[...] = tile_expert.astype(jnp.int32)
    n_used_ref[...] = jnp.broadcast_to(off * (1.0 / TE), (1, LANES)).astype(jnp.int32)


def _slots_call(ids):
    return pl.pallas_call(
        _slots_kernel,
        out_shape=[
            jax.ShapeDtypeStruct((ASSIGN_ROWS, LANES), jnp.int32),
            jax.ShapeDtypeStruct((1, LANES), jnp.int32),
            jax.ShapeDtypeStruct((1, LANES), jnp.int32),
        ],
        name="slots",
    )(ids)


def _dispatch_kernel(slot_ref, x1_hbm, xs_in_hbm, xs_hbm, sem):
    del xs_in_hbm
    c = pl.program_id(0)
    base = c * DISPATCH_CHUNK
    tok0 = base % N_TOK

    def row_copy(src_row, dst_row):
        return pltpu.make_async_copy(x1_hbm.at[pl.ds(src_row, 1)], xs_hbm.at[pl.ds(dst_row, 1)], sem)

    def start(r, carry):
        row_copy(tok0 + r, slot_ref[base + r]).start()
        return carry

    lax.fori_loop(0, DISPATCH_CHUNK, start, 0, unroll=8)

    def wait(r, carry):
        row_copy(0, 0).wait()
        return carry

    lax.fori_loop(0, DISPATCH_CHUNK, wait, 0, unroll=8)


def _dispatch_call(slot_flat, x1, xs):
    return pl.pallas_call(
        _dispatch_kernel,
        grid_spec=pltpu.PrefetchScalarGridSpec(
            num_scalar_prefetch=1,
            grid=(N_ASSIGN // DISPATCH_CHUNK,),
            in_specs=[pl.BlockSpec(memory_space=pl.ANY), pl.BlockSpec(memory_space=pl.ANY)],
            out_specs=pl.BlockSpec(memory_space=pl.ANY),
            scratch_shapes=[pltpu.SemaphoreType.DMA(())],
        ),
        out_shape=jax.ShapeDtypeStruct((N_SLOTS, D_MODEL), _f32),
        input_output_aliases={2: 0},
        compiler_params=pltpu.CompilerParams(dimension_semantics=("arbitrary",)),
        name="dispatch",
    )(slot_flat, x1, xs)


def _experts_kernel(te_ref, nu_ref, xs_ref, wg_ref, wu_ref, wd_ref, y_ref, wg_b, wu_b, wd_b):
    j = pl.program_id(0)

    @pl.when(j < nu_ref[0])
    def _():
        prev = te_ref[jnp.maximum(j - 1, 0)]

        @pl.when((j == 0) | (te_ref[j] != prev))
        def _():
            wg_b[...] = wg_ref[0].astype(_bf16)
            wu_b[...] = wu_ref[0].astype(_bf16)
            wd_b[...] = wd_ref[0].astype(_bf16)

        xb = xs_ref[...].astype(_bf16)
        hg = _dot(xb, wg_b[...])
        hu = _dot(xb, wu_b[...])
        a = hg / (1.0 + jnp.exp(-hg)) * hu
        y_ref[...] = _dot(a.astype(_bf16), wd_b[...])


def _experts_call(tile_expert, n_used, xs, wg, wu, wd):
    row = lambda j, te, nu: (jnp.minimum(j, nu[0] - 1), 0)
    wsel = lambda j, te, nu: (te[j], 0, 0)
    return pl.pallas_call(
        _experts_kernel,
        grid_spec=pltpu.PrefetchScalarGridSpec(
            num_scalar_prefetch=2,
            grid=(N_ETILES,),
            in_specs=[
                pl.BlockSpec((TE, D_MODEL), row),
                pl.BlockSpec((1, D_MODEL, D_EXPERT), wsel),
                pl.BlockSpec((1, D_MODEL, D_EXPERT), wsel),
                pl.BlockSpec((1, D_EXPERT, D_MODEL), wsel),
            ],
            out_specs=pl.BlockSpec((TE, D_MODEL), row),
            scratch_shapes=[
                pltpu.VMEM((D_MODEL, D_EXPERT), _bf16),
                pltpu.VMEM((D_MODEL, D_EXPERT), _bf16),
                pltpu.VMEM((D_EXPERT, D_MODEL), _bf16),
            ],
        ),
        out_shape=jax.ShapeDtypeStruct((N_SLOTS, D_MODEL), _f32),
        input_output_aliases={2: 0},
        compiler_params=pltpu.CompilerParams(
            dimension_semantics=("arbitrary",), vmem_limit_bytes=VMEM_LIMIT),
        name="experts",
    )(tile_expert, n_used, xs, wg, wu, wd)


def _combine_kernel(slot_ref, x1_ref, route_ref, g2_ref, b2_ref, y_hbm, out_ref, y0_buf, y1_buf, sems):
    i = pl.program_id(0)
    tok0 = i * TM

    def row_copy(k, buf, src_row, dst_row):
        return pltpu.make_async_copy(y_hbm.at[pl.ds(src_row, 1)], buf.at[pl.ds(dst_row, 1)], sems.at[k])

    def start(r, carry):
        row_copy(0, y0_buf, slot_ref[tok0 + r], r).start()
        row_copy(1, y1_buf, slot_ref[N_TOK + tok0 + r], r).start()
        return carry

    lax.fori_loop(0, TM, start, 0, unroll=8)

    def wait(r, carry):
        row_copy(0, y0_buf, 0, 0).wait()
        row_copy(1, y1_buf, 0, 0).wait()
        return carry

    lax.fori_loop(0, TM, wait, 0, unroll=8)

    route = route_ref[...]
    moe = route[:, 2:3] * y0_buf[...] + route[:, 3:4] * y1_buf[...]
    out_ref[...] = _layer_norm(ALPHA * x1_ref[...] + moe, g2_ref[...], b2_ref[...])


def _combine_call(slot_flat, x1, route, g2, b2, y):
    tok = lambda i, s: (i, 0)
    const = lambda i, s: (0, 0)
    return pl.pallas_call(
        _combine_kernel,
        grid_spec=pltpu.PrefetchScalarGridSpec(
            num_scalar_prefetch=1,
            grid=(N_TILES,),
            in_specs=[
                pl.BlockSpec((TM, D_MODEL), tok),
                pl.BlockSpec((TM, LANES), tok),
                pl.BlockSpec((1, D_MODEL), const),
                pl.BlockSpec((1, D_MODEL), const),
                pl.BlockSpec(memory_space=pl.ANY),
            ],
            out_specs=pl.BlockSpec((TM, D_MODEL), tok),
            scratch_shapes=[
                pltpu.VMEM((TM, D_MODEL), _f32),
                pltpu.VMEM((TM, D_MODEL), _f32),
                pltpu.SemaphoreType.DMA((2,)),
            ],
        ),
        out_shape=jax.ShapeDtypeStruct((N_TOK, D_MODEL), _f32),
        compiler_params=pltpu.CompilerParams(dimension_semantics=("arbitrary",)),
        name="combine",
    )(slot_flat, x1, route, g2, b2, y)


def kernel(x_prompt, x_sample, state_conv, state_pool, w_in, conv_w, sgu_ln_g, sgu_ln_b, sgu_w, sgu_b, pool_w, pool_scale, w_out, ln1_g, ln1_b, router_group_w, router_group_b, router_expert_w, router_expert_b, expert_w_gate, expert_w_up, expert_w_down, ln2_g, ln2_b):
    x = jnp.concatenate([x_prompt.reshape(N_PROMPT, D_MODEL), x_sample.reshape(N_SAMPLE, D_MODEL)], axis=0)
    xs = jnp.zeros((N_SLOTS, D_MODEL), _f32)
    reps = CHUNK // DEC_SEQ

    conv_p, pool_p, conv_s, pool_s, v_s = [], [], [], [], []
    for l in range(DEPTH):
        w_r = jnp.concatenate([router_group_w[l], router_expert_w[l]], axis=1)
        w_r = jnp.pad(w_r, ((0, 0), (0, LANES - w_r.shape[1])))
        wr_hi = w_r.astype(_bf16)
        wr_lo = (w_r - wr_hi.astype(_f32)).astype(_bf16)
        b_r = jnp.concatenate([router_group_b[l], router_expert_b[l]])
        b_r = jnp.pad(b_r, (0, LANES - b_r.shape[0])).reshape(1, LANES)
        ws_s = jnp.tile(sgu_w[l][:, :DEC_SEQ, :DEC_SEQ], (1, reps, reps))
        bias_p = sgu_b[l].T
        bias_s = jnp.tile(sgu_b[l][:, :DEC_SEQ].T, (reps, 1))
        zst = jnp.pad(state_conv[l], ((0, 0), (CONV_HALO - 2, 0), (0, 0)))
        pst = jnp.pad(state_pool[l], ((0, 0), (POOL_HALO - POOL_BUF, 0), (0, 0)))

        x1, route, ztail, ptail, zs, psamp, vns = _mixer_call(
            x, w_in[l].astype(_bf16), w_out[l].astype(_bf16), conv_w[l],
            sgu_ln_g[l].reshape(1, -1), sgu_ln_b[l].reshape(1, -1), sgu_w[l], ws_s, bias_p, bias_s,
            pool_w[l], pool_scale[l].reshape(1, -1), ln1_g[l].reshape(1, -1), ln1_b[l].reshape(1, -1),
            wr_hi, wr_lo, b_r, zst, pst)

        ids = route[:, 0:2].astype(jnp.int32).T.reshape(ASSIGN_ROWS, LANES)
        slot, tile_expert, n_used = _slots_call(ids)
        slot_flat = slot.reshape(N_ASSIGN)
        xs = _dispatch_call(slot_flat, x1, xs)
        y = _experts_call(tile_expert.reshape(LANES), n_used.reshape(LANES)[:1], xs,
                          expert_w_gate[l], expert_w_up[l], expert_w_down[l])
        x = _combine_call(slot_flat, x1, route, ln2_g[l].reshape(1, -1), ln2_b[l].reshape(1, -1), y)
        xs = y

        conv_p.append(ztail[:BATCH, CONV_HALO - 2:])
        pool_p.append(ptail[:BATCH, POOL_HALO - POOL_BUF:])
        conv_s.append(zs[:, DEC_SEQ - 2:])
        pool_s.append(jnp.concatenate([state_pool[l][:, DEC_SEQ:], psamp], axis=1))
        v_s.append(vns)

    y_prompt = x[:N_PROMPT].reshape(BATCH, SEQ, D_MODEL)
    y_sample = x[N_PROMPT:].reshape(DEC_BATCH, DEC_SEQ, D_MODEL)
    return (y_prompt, y_sample, jnp.stack(conv_p), jnp.stack(pool_p), jnp.stack(conv_s),
            jnp.stack(pool_s), jnp.stack(v_s))
```

```python
import functools

import jax
import jax.numpy as jnp
from jax import lax
from jax.experimental import pallas as pl
from jax.experimental.pallas import tpu as pltpu

D_MODEL = 2048
BATCH = 4
SEQ = 2048
DEPTH = 4
DEC_BATCH = 128
DEC_SEQ = 8
PAST_LEN = 16384
HEAD_DIM = 128
CONV_WIDTH = 768
POOL_WIDTH = 512
CHUNK_WIDTH = 768
CHUNK = 128
N_CHUNK_HEADS = 6
POOL_WINDOWS = (2, 4, 8, 16)
POOL_GROUP_DIM = 128
POOL_BUF = 15
IN_WIDTH = 4352
N_EXPERT_GROUPS = 4
EXPERTS_PER_GROUP = 4
N_EXPERTS = 16
D_EXPERT = 512
ALPHA = (2 * DEPTH) ** 0.25
LN_EPS = 1e-5

N_PROMPT = BATCH * SEQ
N_SAMPLE = DEC_BATCH * DEC_SEQ
N_TOK = N_PROMPT + N_SAMPLE
N_ASSIGN = 2 * N_TOK

LANES = 128
SUBLANES = 8
CONV_HALO = 8
POOL_HALO = 16

TM = 256
TILES_PER_SEQ = SEQ // TM
N_PROMPT_TILES = N_PROMPT // TM
N_SAMPLE_TILES = N_SAMPLE // TM
N_TILES = N_PROMPT_TILES + N_SAMPLE_TILES
SEQS_PER_TILE = TM // DEC_SEQ

TE = 256
N_SLOTS = N_ASSIGN + N_EXPERTS * TE
N_ETILES = N_SLOTS // TE
ASSIGN_ROWS = N_ASSIGN // LANES

CUT_B, CUT_C, CUT_H, CUT_U, CUT_V, CUT_P = 0, 768, 1536, 2304, 3072, 3840

VMEM_LIMIT = 60 * 1024 * 1024

_f32 = jnp.float32
_bf16 = jnp.bfloat16


def _dot(a, b):
    return jnp.dot(a, b, preferred_element_type=_f32)


def _layer_norm(r, g, b):
    mu = jnp.mean(r, axis=-1, keepdims=True)
    c = r - mu
    var = jnp.mean(c * c, axis=-1, keepdims=True)
    return c * lax.rsqrt(var + LN_EPS) * g + b


def _mixer_body(i, xb, w_in_ref, conv_w_ref, sg_ref, sb_ref, ws_ref, bias_ref, pw_ref, ps_ref,
                zext_ref, pext_ref, ybuf_ref, seq_len, pos, chunk_len):
    n_seq = TM // seq_len

    z = _dot(xb, w_in_ref[:, CUT_C:CUT_H]) * _dot(xb, w_in_ref[:, CUT_H:CUT_U])
    zext_ref[:, CONV_HALO:, :] = z.reshape(n_seq, seq_len, CONV_WIDTH)
    cw = conv_w_ref[...]
    conv = (cw[0:1, :] * zext_ref[:, CONV_HALO - 2:CONV_HALO - 2 + seq_len, :]
            + cw[1:2, :] * zext_ref[:, CONV_HALO - 1:CONV_HALO - 1 + seq_len, :]
            + cw[2:3, :] * zext_ref[:, CONV_HALO:, :])
    y_conv = _dot(xb, w_in_ref[:, CUT_B:CUT_C]) * conv.reshape(TM, CONV_WIDTH)
    ybuf_ref[:, 0:CONV_WIDTH] = y_conv.astype(_bf16)

    v = _dot(xb, w_in_ref[:, CUT_V:CUT_P])
    vn = _layer_norm(v, sg_ref[...], sb_ref[...])
    vnb = vn.astype(_bf16)
    r_i = lax.broadcasted_iota(jnp.int32, (CHUNK, CHUNK), 0)
    c_i = lax.broadcasted_iota(jnp.int32, (CHUNK, CHUNK), 1)
    shift = chunk_len.bit_length() - 1
    mask = (c_i <= r_i) & ((r_i >> shift) == (c_i >> shift))
    for h in range(N_CHUNK_HEADS):
        m_h = jnp.where(mask, ws_ref[h], 0.0).astype(_bf16)
        b_h = bias_ref[:, h:h + 1]
        cols = slice(h * HEAD_DIM, (h + 1) * HEAD_DIM)
        u_h = _dot(xb, w_in_ref[:, CUT_U + h * HEAD_DIM:CUT_U + (h + 1) * HEAD_DIM])
        for c in range(TM // CHUNK):
            rows = slice(c * CHUNK, (c + 1) * CHUNK)
            mixed = _dot(m_h, vnb[rows, cols]) + b_h
            ybuf_ref[rows, CONV_WIDTH + h * HEAD_DIM:CONV_WIDTH + (h + 1) * HEAD_DIM] = (
                u_h[rows, :] * mixed).astype(_bf16)

    p = _dot(xb, w_in_ref[:, CUT_P:IN_WIDTH])
    pext_ref[:, POOL_HALO:, :] = p.reshape(n_seq, seq_len, POOL_WIDTH)
    posf = (pos + 1).astype(_f32)
    for g, w in enumerate(POOL_WINDOWS):
        cols = slice(g * POOL_GROUP_DIM, (g + 1) * POOL_GROUP_DIM)
        s = pext_ref[:, POOL_HALO:, cols]
        for j in range(1, w):
            s = s + pext_ref[:, POOL_HALO - j:POOL_HALO - j + seq_len, cols]
        cnt = jnp.minimum(float(w), posf)
        diff = s.reshape(TM, POOL_GROUP_DIM) / cnt - p[:, cols]
        y_g = _dot(diff.astype(_bf16), pw_ref[g].astype(_bf16)) * ps_ref[:, cols]
        ybuf_ref[:, CONV_WIDTH + CHUNK_WIDTH + g * POOL_GROUP_DIM:
                 CONV_WIDTH + CHUNK_WIDTH + (g + 1) * POOL_GROUP_DIM] = y_g.astype(_bf16)
    return z, p, vn


def _route(x1, wr_hi_ref, wr_lo_ref, br_ref):
    x_hi = x1.astype(_bf16)
    x_lo = (x1 - x_hi.astype(_f32)).astype(_bf16)
    logits = (_dot(x_hi, wr_hi_ref[...]) + _dot(x_lo, wr_hi_ref[...])
              + _dot(x_hi, wr_lo_ref[...]) + br_ref[...])
    lane = lax.broadcasted_iota(jnp.int32, logits.shape, 1)
    neg = -jnp.inf
    big = jnp.int32(1 << 20)

    gmask = lane < N_EXPERT_GROUPS
    gl = jnp.where(gmask, logits, neg)
    ge = jnp.exp(gl - jnp.max(gl, axis=1, keepdims=True))
    gp = ge / jnp.sum(ge, axis=1, keepdims=True)
    g_val = jnp.max(gp, axis=1, keepdims=True)
    g_idx = jnp.min(jnp.where(gmask & (gp == g_val), lane, big), axis=1, keepdims=True)

    e_lane = lane - N_EXPERT_GROUPS
    emask = (e_lane >= 0) & (e_lane < N_EXPERTS) & ((e_lane >> 2) == g_idx)
    el = jnp.where(emask, logits, neg)
    ee = jnp.exp(el - jnp.max(el, axis=1, keepdims=True))
    ep = ee / jnp.sum(ee, axis=1, keepdims=True)
    v1 = jnp.max(jnp.where(emask, ep, -1.0), axis=1, keepdims=True)
    i1 = jnp.min(jnp.where(emask & (ep == v1), e_lane, big), axis=1, keepdims=True)
    mask2 = emask & (e_lane != i1)
    v2 = jnp.max(jnp.where(mask2, ep, -1.0), axis=1, keepdims=True)
    i2 = jnp.min(jnp.where(mask2 & (ep == v2), e_lane, big), axis=1, keepdims=True)
    tot = v1 + v2
    gate1 = g_val * (v1 / tot)
    gate2 = g_val * (v2 / tot)
    return jnp.where(lane == 0, i1.astype(_f32),
                     jnp.where(lane == 1, i2.astype(_f32),
                               jnp.where(lane == 2, gate1, jnp.where(lane == 3, gate2, 0.0))))


def _mixer_kernel(xp_ref, xsamp_ref, w_in_ref, w_out_ref, conv_w_ref, sg_ref, sb_ref, ws_p_ref, ws_s_ref,
                  bias_p_ref, bias_s_ref, pw_ref, ps_ref, g1_ref, b1_ref, wr_hi_ref, wr_lo_ref,
                  br_ref, zst_ref, pst_ref,
                  x1_ref, route_ref, ztail_ref, ptail_ref, zs_ref, psamp_ref, vns_ref,
                  zext_p, pext_p, zext_s, pext_s, ybuf_ref):
    i = pl.program_id(0)
    x = jnp.where(i < N_PROMPT_TILES, xp_ref[...], xsamp_ref[...])
    xb = x.astype(_bf16)
    row = lax.broadcasted_iota(jnp.int32, (TM, 1), 0)

    @pl.when(i < N_PROMPT_TILES)
    def _prompt():
        t0 = (i % TILES_PER_SEQ) * TM

        @pl.when(t0 == 0)
        def _():
            zext_p[:, 0:CONV_HALO, :] = jnp.zeros((1, CONV_HALO, CONV_WIDTH), _f32)
            pext_p[:, 0:POOL_HALO, :] = jnp.zeros((1, POOL_HALO, POOL_WIDTH), _f32)

        z, p, _ = _mixer_body(i, xb, w_in_ref, conv_w_ref, sg_ref, sb_ref, ws_p_ref, bias_p_ref,
                              pw_ref, ps_ref, zext_p, pext_p, ybuf_ref, TM, t0 + row, CHUNK)
        z_last = z[TM - CONV_HALO:, :].reshape(1, CONV_HALO, CONV_WIDTH)
        p_last = p[TM - POOL_HALO:, :].reshape(1, POOL_HALO, POOL_WIDTH)
        zext_p[:, 0:CONV_HALO, :] = z_last
        pext_p[:, 0:POOL_HALO, :] = p_last
        ztail_ref[...] = z_last
        ptail_ref[...] = p_last

    @pl.when(i >= N_PROMPT_TILES)
    def _sample():
        zext_s[:, 0:CONV_HALO, :] = zst_ref[...]
        pext_s[:, 0:POOL_HALO, :] = pst_ref[...]
        z, p, vn = _mixer_body(i, xb, w_in_ref, conv_w_ref, sg_ref, sb_ref, ws_s_ref, bias_s_ref,
                               pw_ref, ps_ref, zext_s, pext_s, ybuf_ref, DEC_SEQ,
                               PAST_LEN + (row & (DEC_SEQ - 1)), DEC_SEQ)
        zs_ref[...] = z.reshape(SEQS_PER_TILE, DEC_SEQ, CONV_WIDTH)
        psamp_ref[...] = p.reshape(SEQS_PER_TILE, DEC_SEQ, POOL_WIDTH)
        vns_ref[...] = vn.reshape(SEQS_PER_TILE, DEC_SEQ, CHUNK_WIDTH)
        ztail_ref[...] = jnp.zeros(ztail_ref.shape, _f32)
        ptail_ref[...] = jnp.zeros(ptail_ref.shape, _f32)

    mix = _dot(ybuf_ref[...], w_out_ref[...])
    x1 = _layer_norm(ALPHA * x + mix, g1_ref[...], b1_ref[...])
    x1_ref[...] = x1
    route_ref[...] = _route(x1, wr_hi_ref, wr_lo_ref, br_ref)


def _const_spec(shape):
    nd = len(shape)
    return pl.BlockSpec(shape, lambda i: (0,) * nd, pipeline_mode=pl.Buffered(1))


def _mixer_call(x_p, x_s, w_in_b, w_out_b, conv_w, sg, sb, ws_p, ws_s, bias_p, bias_s, pw, ps, g1, b1,
                wr_hi, wr_lo, br, zst, pst):
    samp = lambda i: (jnp.maximum(i - N_PROMPT_TILES, 0), 0, 0)
    tail = lambda i: (i // TILES_PER_SEQ, 0, 0)
    tok = lambda i: (i, 0)
    in_specs = [
        pl.BlockSpec((TM, D_MODEL), lambda i: (jnp.minimum(i, N_PROMPT_TILES - 1), 0)),
        pl.BlockSpec((TM, D_MODEL), lambda i: (jnp.maximum(i - N_PROMPT_TILES, 0), 0)),
        _const_spec((D_MODEL, IN_WIDTH)),
        _const_spec((D_MODEL, D_MODEL)),
        _const_spec((3, CONV_WIDTH)),
        _const_spec((1, CHUNK_WIDTH)),
        _const_spec((1, CHUNK_WIDTH)),
        _const_spec((N_CHUNK_HEADS, CHUNK, CHUNK)),
        _const_spec((N_CHUNK_HEADS, CHUNK, CHUNK)),
        _const_spec((CHUNK, N_CHUNK_HEADS)),
        _const_spec((CHUNK, N_CHUNK_HEADS)),
        _const_spec((4, POOL_GROUP_DIM, POOL_GROUP_DIM)),
        _const_spec((1, POOL_WIDTH)),
        _const_spec((1, D_MODEL)),
        _const_spec((1, D_MODEL)),
        _const_spec((D_MODEL, LANES)),
        _const_spec((D_MODEL, LANES)),
        _const_spec((1, LANES)),
        pl.BlockSpec((SEQS_PER_TILE, CONV_HALO, CONV_WIDTH), samp),
        pl.BlockSpec((SEQS_PER_TILE, POOL_HALO, POOL_WIDTH), samp),
    ]
    out_specs = [
        pl.BlockSpec((TM, D_MODEL), tok),
        pl.BlockSpec((TM, LANES), tok),
        pl.BlockSpec((1, CONV_HALO, CONV_WIDTH), tail),
        pl.BlockSpec((1, POOL_HALO, POOL_WIDTH), tail),
        pl.BlockSpec((SEQS_PER_TILE, DEC_SEQ, CONV_WIDTH), samp),
        pl.BlockSpec((SEQS_PER_TILE, DEC_SEQ, POOL_WIDTH), samp),
        pl.BlockSpec((SEQS_PER_TILE, DEC_SEQ, CHUNK_WIDTH), samp),
    ]
    out_shape = [
        jax.ShapeDtypeStruct((N_TOK, D_MODEL), _f32),
        jax.ShapeDtypeStruct((N_TOK, LANES), _f32),
        jax.ShapeDtypeStruct((BATCH + 1, CONV_HALO, CONV_WIDTH), _f32),
        jax.ShapeDtypeStruct((BATCH + 1, POOL_HALO, POOL_WIDTH), _f32),
        jax.ShapeDtypeStruct((DEC_BATCH, DEC_SEQ, CONV_WIDTH), _f32),
        jax.ShapeDtypeStruct((DEC_BATCH, DEC_SEQ, POOL_WIDTH), _f32),
        jax.ShapeDtypeStruct((DEC_BATCH, DEC_SEQ, CHUNK_WIDTH), _f32),
    ]
    scratch = [
        pltpu.VMEM((1, CONV_HALO + TM, CONV_WIDTH), _f32),
        pltpu.VMEM((1, POOL_HALO + TM, POOL_WIDTH), _f32),
        pltpu.VMEM((SEQS_PER_TILE, CONV_HALO + DEC_SEQ, CONV_WIDTH), _f32),
        pltpu.VMEM((SEQS_PER_TILE, POOL_HALO + DEC_SEQ, POOL_WIDTH), _f32),
        pltpu.VMEM((TM, D_MODEL), _bf16),
    ]
    return pl.pallas_call(
        _mixer_kernel,
        grid=(N_TILES,),
        in_specs=in_specs,
        out_specs=out_specs,
        out_shape=out_shape,
        scratch_shapes=scratch,
        compiler_params=pltpu.CompilerParams(
            dimension_semantics=("arbitrary",), vmem_limit_bytes=VMEM_LIMIT),
        name="mixer",
    )(x_p, x_s, w_in_b, w_out_b, conv_w, sg, sb, ws_p, ws_s, bias_p, bias_s, pw, ps, g1, b1,
      wr_hi, wr_lo, br, zst, pst)


def _slots_kernel(ids_ref, slot_ref, tile_expert_ref, n_used_ref):
    ids = ids_ref[...]
    r_i = lax.broadcasted_iota(jnp.int32, (LANES, LANES), 0)
    c_i = lax.broadcasted_iota(jnp.int32, (LANES, LANES), 1)
    upper = (r_i < c_i).astype(_bf16)
    rr = lax.broadcasted_iota(jnp.int32, (ASSIGN_ROWS, ASSIGN_ROWS), 0)
    rc = lax.broadcasted_iota(jnp.int32, (ASSIGN_ROWS, ASSIGN_ROWS), 1)
    lower = (rc < rr).astype(_bf16)
    tile_row0 = (lax.broadcasted_iota(jnp.int32, (1, LANES), 1) * TE).astype(_f32)

    slot = jnp.zeros((ASSIGN_ROWS, LANES), _f32)
    tile_expert = jnp.zeros((1, LANES), _f32)
    off = jnp.zeros((1, 1), _f32)
    for e in range(N_EXPERTS):
        m = (ids == e).astype(_f32)
        within = _dot(m.astype(_bf16), upper)
        rowsum = jnp.sum(m, axis=1, keepdims=True)
        rowpre = _dot(lower, jnp.broadcast_to(rowsum, (ASSIGN_ROWS, LANES)).astype(_bf16))
        cnt = jnp.sum(rowsum, axis=0, keepdims=True)
        padded = jnp.floor((cnt + (TE - 1)) * (1.0 / TE)) * TE
        slot = slot + m * (off + rowpre + within)
        in_seg = (tile_row0 >= off) & (tile_row0 < off + padded)
        tile_expert = tile_expert + jnp.where(in_seg, float(e), 0.0)
        off = off + padded
    tile_expert = jnp.where(tile_row0 >= off, float(N_EXPERTS - 1), tile_expert)
    slot_ref[...] = slot.astype(jnp.int32)
    tile_expert_ref[...] = tile_expert.astype(jnp.int32)
    n_used_ref[...] = jnp.broadcast_to(off * (1.0 / TE), (1, LANES)).astype(jnp.int32)


def _slots_call(ids):
    return pl.pallas_call(
        _slots_kernel,
        out_shape=[
            jax.ShapeDtypeStruct((ASSIGN_ROWS, LANES), jnp.int32),
            jax.ShapeDtypeStruct((1, LANES), jnp.int32),
            jax.ShapeDtypeStruct((1, LANES), jnp.int32),
        ],
        name="slots",
    )(ids)


def _dispatch_kernel(slot_ref, x1_ref, xs_in_hbm, xs_hbm, sems):
    del xs_in_hbm
    tok0 = pl.program_id(0) * TM

    def row_copy(k, src_row, dst_row):
        return pltpu.make_async_copy(x1_ref.at[pl.ds(src_row, 1)], xs_hbm.at[pl.ds(dst_row, 1)], sems.at[k])

    def start(r, carry):
        row_copy(0, r, slot_ref[tok0 + r]).start()
        row_copy(1, r, slot_ref[N_TOK + tok0 + r]).start()
        return carry

    lax.fori_loop(0, TM, start, 0, unroll=8)

    def wait(r, carry):
        row_copy(0, 0, 0).wait()
        row_copy(1, 0, 0).wait()
        return carry

    lax.fori_loop(0, TM, wait, 0, unroll=8)


def _dispatch_call(slot_flat, x1, xs):
    return pl.pallas_call(
        _dispatch_kernel,
        grid_spec=pltpu.PrefetchScalarGridSpec(
            num_scalar_prefetch=1,
            grid=(N_TILES,),
            in_specs=[pl.BlockSpec((TM, D_MODEL), lambda i, s: (i, 0)), pl.BlockSpec(memory_space=pl.ANY)],
            out_specs=pl.BlockSpec(memory_space=pl.ANY),
            scratch_shapes=[pltpu.SemaphoreType.DMA((2,))],
        ),
        out_shape=jax.ShapeDtypeStruct((N_SLOTS, D_MODEL), _f32),
        input_output_aliases={2: 0},
        compiler_params=pltpu.CompilerParams(dimension_semantics=("arbitrary",)),
        name="dispatch",
    )(slot_flat, x1, xs)


def _experts_kernel(te_ref, nu_ref, xs_ref, wg_ref, wu_ref, wd_ref, y_ref, wg_b, wu_b, wd_b):
    j = pl.program_id(0)

    @pl.when(j < nu_ref[0])
    def _():
        prev = te_ref[jnp.maximum(j - 1, 0)]

        @pl.when((j == 0) | (te_ref[j] != prev))
        def _():
            wg_b[...] = wg_ref[0].astype(_bf16)
            wu_b[...] = wu_ref[0].astype(_bf16)
            wd_b[...] = wd_ref[0].astype(_bf16)

        xb = xs_ref[...].astype(_bf16)
        hg = _dot(xb, wg_b[...])
        hu = _dot(xb, wu_b[...])
        a = hg / (1.0 + jnp.exp(-hg)) * hu
        y_ref[...] = _dot(a.astype(_bf16), wd_b[...])


def _experts_call(tile_expert, n_used, xs, wg, wu, wd):
    row = lambda j, te, nu: (jnp.minimum(j, nu[0] - 1), 0)
    wsel = lambda j, te, nu: (te[j], 0, 0)
    return pl.pallas_call(
        _experts_kernel,
        grid_spec=pltpu.PrefetchScalarGridSpec(
            num_scalar_prefetch=2,
            grid=(N_ETILES,),
            in_specs=[
                pl.BlockSpec((TE, D_MODEL), row),
                pl.BlockSpec((1, D_MODEL, D_EXPERT), wsel),
                pl.BlockSpec((1, D_MODEL, D_EXPERT), wsel),
                pl.BlockSpec((1, D_EXPERT, D_MODEL), wsel),
            ],
            out_specs=pl.BlockSpec((TE, D_MODEL), row),
            scratch_shapes=[
                pltpu.VMEM((D_MODEL, D_EXPERT), _bf16),
                pltpu.VMEM((D_MODEL, D_EXPERT), _bf16),
                pltpu.VMEM((D_EXPERT, D_MODEL), _bf16),
            ],
        ),
        out_shape=jax.ShapeDtypeStruct((N_SLOTS, D_MODEL), _f32),
        input_output_aliases={2: 0},
        compiler_params=pltpu.CompilerParams(
            dimension_semantics=("arbitrary",), vmem_limit_bytes=VMEM_LIMIT),
        name="experts",
    )(tile_expert, n_used, xs, wg, wu, wd)


def _combine_kernel(slot_ref, x1_ref, route_ref, g2_ref, b2_ref, y_hbm, out_p_ref, out_s_ref,
                    y0_buf, y1_buf, sems):
    i = pl.program_id(0)
    tok0 = i * TM

    def row_copy(k, buf, src_row, dst_row):
        return pltpu.make_async_copy(y_hbm.at[pl.ds(src_row, 1)], buf.at[pl.ds(dst_row, 1)], sems.at[k])

    def start(r, carry):
        row_copy(0, y0_buf, slot_ref[tok0 + r], r).start()
        row_copy(1, y1_buf, slot_ref[N_TOK + tok0 + r], r).start()
        return carry

    lax.fori_loop(0, TM, start, 0, unroll=8)

    def wait(r, carry):
        row_copy(0, y0_buf, 0, 0).wait()
        row_copy(1, y1_buf, 0, 0).wait()
        return carry

    lax.fori_loop(0, TM, wait, 0, unroll=8)

    route = route_ref[...]
    moe = route[:, 2:3] * y0_buf[...] + route[:, 3:4] * y1_buf[...]
    out = _layer_norm(ALPHA * x1_ref[...] + moe, g2_ref[...], b2_ref[...])

    @pl.when(i < N_PROMPT_TILES)
    def _():
        out_p_ref[...] = out

    @pl.when(i >= N_PROMPT_TILES)
    def _():
        out_s_ref[...] = out


def _combine_call(slot_flat, x1, route, g2, b2, y):
    tok = lambda i, s: (i, 0)
    const = lambda i, s: (0, 0)
    return pl.pallas_call(
        _combine_kernel,
        grid_spec=pltpu.PrefetchScalarGridSpec(
            num_scalar_prefetch=1,
            grid=(N_TILES,),
            in_specs=[
                pl.BlockSpec((TM, D_MODEL), tok),
                pl.BlockSpec((TM, LANES), tok),
                pl.BlockSpec((1, D_MODEL), const),
                pl.BlockSpec((1, D_MODEL), const),
                pl.BlockSpec(memory_space=pl.ANY),
            ],
            out_specs=[
                pl.BlockSpec((TM, D_MODEL), lambda i, s: (jnp.minimum(i, N_PROMPT_TILES - 1), 0)),
                pl.BlockSpec((TM, D_MODEL), lambda i, s: (jnp.maximum(i - N_PROMPT_TILES, 0), 0)),
            ],
            scratch_shapes=[
                pltpu.VMEM((TM, D_MODEL), _f32),
                pltpu.VMEM((TM, D_MODEL), _f32),
                pltpu.SemaphoreType.DMA((2,)),
            ],
        ),
        out_shape=[jax.ShapeDtypeStruct((N_PROMPT, D_MODEL), _f32),
                   jax.ShapeDtypeStruct((N_SAMPLE, D_MODEL), _f32)],
        compiler_params=pltpu.CompilerParams(dimension_semantics=("arbitrary",)),
        name="combine",
    )(slot_flat, x1, route, g2, b2, y)


def kernel(x_prompt, x_sample, state_conv, state_pool, w_in, conv_w, sgu_ln_g, sgu_ln_b, sgu_w, sgu_b, pool_w, pool_scale, w_out, ln1_g, ln1_b, router_group_w, router_group_b, router_expert_w, router_expert_b, expert_w_gate, expert_w_up, expert_w_down, ln2_g, ln2_b):
    x_p = x_prompt.reshape(N_PROMPT, D_MODEL)
    x_s = x_sample.reshape(N_SAMPLE, D_MODEL)
    xs = jnp.zeros((N_SLOTS, D_MODEL), _f32)
    reps = CHUNK // DEC_SEQ

    conv_p, pool_p, conv_s, pool_s, v_s = [], [], [], [], []
    for l in range(DEPTH):
        w_r = jnp.concatenate([router_group_w[l], router_expert_w[l]], axis=1)
        w_r = jnp.pad(w_r, ((0, 0), (0, LANES - w_r.shape[1])))
        wr_hi = w_r.astype(_bf16)
        wr_lo = (w_r - wr_hi.astype(_f32)).astype(_bf16)
        b_r = jnp.concatenate([router_group_b[l], router_expert_b[l]])
        b_r = jnp.pad(b_r, (0, LANES - b_r.shape[0])).reshape(1, LANES)
        ws_s = jnp.tile(sgu_w[l][:, :DEC_SEQ, :DEC_SEQ], (1, reps, reps))
        bias_p = sgu_b[l].T
        bias_s = jnp.tile(sgu_b[l][:, :DEC_SEQ].T, (reps, 1))
        zst = jnp.pad(state_conv[l], ((0, 0), (CONV_HALO - 2, 0), (0, 0)))
        pst = jnp.pad(state_pool[l], ((0, 0), (POOL_HALO - POOL_BUF, 0), (0, 0)))

        x1, route, ztail, ptail, zs, psamp, vns = _mixer_call(
            x_p, x_s, w_in[l].astype(_bf16), w_out[l].astype(_bf16), conv_w[l],
            sgu_ln_g[l].reshape(1, -1), sgu_ln_b[l].reshape(1, -1), sgu_w[l], ws_s, bias_p, bias_s,
            pool_w[l], pool_scale[l].reshape(1, -1), ln1_g[l].reshape(1, -1), ln1_b[l].reshape(1, -1),
            wr_hi, wr_lo, b_r, zst, pst)

        ids = route[:, 0:2].astype(jnp.int32).T.reshape(ASSIGN_ROWS, LANES)
        slot, tile_expert, n_used = _slots_call(ids)
        slot_flat = slot.reshape(N_ASSIGN)
        xs = _dispatch_call(slot_flat, x1, xs)
        y = _experts_call(tile_expert.reshape(LANES), n_used.reshape(LANES)[:1], xs,
                          expert_w_gate[l], expert_w_up[l], expert_w_down[l])
        x_p, x_s = _combine_call(slot_flat, x1, route, ln2_g[l].reshape(1, -1), ln2_b[l].reshape(1, -1), y)
        xs = y

        conv_p.append(ztail[:BATCH, CONV_HALO - 2:])
        pool_p.append(ptail[:BATCH, POOL_HALO - POOL_BUF:])
        conv_s.append(zs[:, DEC_SEQ - 2:])
        pool_s.append(jnp.concatenate([state_pool[l][:, DEC_SEQ:], psamp], axis=1))
        v_s.append(vns)

    y_prompt = x_p.reshape(BATCH, SEQ, D_MODEL)
    y_sample = x_s.reshape(DEC_BATCH, DEC_SEQ, D_MODEL)
    return (y_prompt, y_sample, jnp.stack(conv_p), jnp.stack(pool_p), jnp.stack(conv_s),
            jnp.stack(pool_s), jnp.stack(v_s))
```

```python
import functools

import jax
import jax.numpy as jnp
from jax import lax
from jax.experimental import pallas as pl
from jax.experimental.pallas import tpu as pltpu

D_MODEL = 2048
BATCH = 4
SEQ = 2048
DEPTH = 4
DEC_BATCH = 128
DEC_SEQ = 8
PAST_LEN = 16384
HEAD_DIM = 128
CONV_WIDTH = 768
POOL_WIDTH = 512
CHUNK_WIDTH = 768
CHUNK = 128
N_CHUNK_HEADS = 6
POOL_WINDOWS = (2, 4, 8, 16)
POOL_GROUP_DIM = 128
POOL_BUF = 15
IN_WIDTH = 4352
N_EXPERT_GROUPS = 4
EXPERTS_PER_GROUP = 4
N_EXPERTS = 16
D_EXPERT = 512
ALPHA = (2 * DEPTH) ** 0.25
LN_EPS = 1e-5

N_PROMPT = BATCH * SEQ
N_SAMPLE = DEC_BATCH * DEC_SEQ
N_TOK = N_PROMPT + N_SAMPLE
N_ASSIGN = 2 * N_TOK

LANES = 128
SUBLANES = 8
CONV_HALO = 8
POOL_HALO = 16

TM = 256
TILES_PER_SEQ = SEQ // TM
N_PROMPT_TILES = N_PROMPT // TM
N_SAMPLE_TILES = N_SAMPLE // TM
N_TILES = N_PROMPT_TILES + N_SAMPLE_TILES
SEQS_PER_TILE = TM // DEC_SEQ

TE = 256
N_SLOTS = N_ASSIGN + N_EXPERTS * TE
N_ETILES = N_SLOTS // TE
ASSIGN_ROWS = N_ASSIGN // LANES

CUT_B, CUT_C, CUT_H, CUT_U, CUT_V, CUT_P = 0, 768, 1536, 2304, 3072, 3840

VMEM_LIMIT = 60 * 1024 * 1024

_f32 = jnp.float32
_bf16 = jnp.bfloat16


def _dot(a, b):
    return jnp.dot(a, b, preferred_element_type=_f32)


def _layer_norm(r, g, b):
    mu = jnp.mean(r, axis=-1, keepdims=True)
    c = r - mu
    var = jnp.mean(c * c, axis=-1, keepdims=True)
    return c * lax.rsqrt(var + LN_EPS) * g + b


def _mixer_body(i, xb, w_in_ref, conv_w_ref, sg_ref, sb_ref, ws_ref, bias_ref, pw_ref, ps_ref,
                zext_ref, pext_ref, ybuf_ref, seq_len, pos, chunk_len):
    n_seq = TM // seq_len

    z = _dot(xb, w_in_ref[:, CUT_C:CUT_H]) * _dot(xb, w_in_ref[:, CUT_H:CUT_U])
    zext_ref[:, CONV_HALO:, :] = z.reshape(n_seq, seq_len, CONV_WIDTH)
    cw = conv_w_ref[...]
    conv = (cw[0:1, :] * zext_ref[:, CONV_HALO - 2:CONV_HALO - 2 + seq_len, :]
            + cw[1:2, :] * zext_ref[:, CONV_HALO - 1:CONV_HALO - 1 + seq_len, :]
            + cw[2:3, :] * zext_ref[:, CONV_HALO:, :])
    y_conv = _dot(xb, w_in_ref[:, CUT_B:CUT_C]) * conv.reshape(TM, CONV_WIDTH)
    ybuf_ref[:, 0:CONV_WIDTH] = y_conv.astype(_bf16)

    v = _dot(xb, w_in_ref[:, CUT_V:CUT_P])
    vn = _layer_norm(v, sg_ref[...], sb_ref[...])
    vnb = vn.astype(_bf16)
    r_i = lax.broadcasted_iota(jnp.int32, (CHUNK, CHUNK), 0)
    c_i = lax.broadcasted_iota(jnp.int32, (CHUNK, CHUNK), 1)
    shift = chunk_len.bit_length() - 1
    mask = (c_i <= r_i) & ((r_i >> shift) == (c_i >> shift))
    for h in range(N_CHUNK_HEADS):
        m_h = jnp.where(mask, ws_ref[h], 0.0).astype(_bf16)
        b_h = bias_ref[:, h:h + 1]
        cols = slice(h * HEAD_DIM, (h + 1) * HEAD_DIM)
        u_h = _dot(xb, w_in_ref[:, CUT_U + h * HEAD_DIM:CUT_U + (h + 1) * HEAD_DIM])
        for c in range(TM // CHUNK):
            rows = slice(c * CHUNK, (c + 1) * CHUNK)
            mixed = _dot(m_h, vnb[rows, cols]) + b_h
            ybuf_ref[rows, CONV_WIDTH + h * HEAD_DIM:CONV_WIDTH + (h + 1) * HEAD_DIM] = (
                u_h[rows, :] * mixed).astype(_bf16)

    p = _dot(xb, w_in_ref[:, CUT_P:IN_WIDTH])
    pext_ref[:, POOL_HALO:, :] = p.reshape(n_seq, seq_len, POOL_WIDTH)
    posf = (pos + 1).astype(_f32)
    for g, w in enumerate(POOL_WINDOWS):
        cols = slice(g * POOL_GROUP_DIM, (g + 1) * POOL_GROUP_DIM)
        s = pext_ref[:, POOL_HALO:, cols]
        for j in range(1, w):
            s = s + pext_ref[:, POOL_HALO - j:POOL_HALO - j + seq_len, cols]
        cnt = jnp.minimum(float(w), posf)
        diff = s.reshape(TM, POOL_GROUP_DIM) / cnt - p[:, cols]
        y_g = _dot(diff.astype(_bf16), pw_ref[g].astype(_bf16)) * ps_ref[:, cols]
        ybuf_ref[:, CONV_WIDTH + CHUNK_WIDTH + g * POOL_GROUP_DIM:
                 CONV_WIDTH + CHUNK_WIDTH + (g + 1) * POOL_GROUP_DIM] = y_g.astype(_bf16)
    return z, p, vn


def _route(x1, wr_ref, br_ref):
    x_hi = x1.astype(_bf16)
    x_lo = (x1 - x_hi.astype(_f32)).astype(_bf16)
    parts = _dot(x_hi, wr_ref[...]) + _dot(x_lo, wr_ref[...])
    logits = parts[:, :LANES] + parts[:, LANES:] + br_ref[...]
    lane = lax.broadcasted_iota(jnp.int32, logits.shape, 1)
    neg = -jnp.inf
    big = jnp.int32(1 << 20)

    gmask = lane < N_EXPERT_GROUPS
    gl = jnp.where(gmask, logits, neg)
    ge = jnp.exp(gl - jnp.max(gl, axis=1, keepdims=True))
    gp = ge / jnp.sum(ge, axis=1, keepdims=True)
    g_val = jnp.max(gp, axis=1, keepdims=True)
    g_idx = jnp.min(jnp.where(gmask & (gp == g_val), lane, big), axis=1, keepdims=True)

    e_lane = lane - N_EXPERT_GROUPS
    emask = (e_lane >= 0) & (e_lane < N_EXPERTS) & ((e_lane >> (EXPERTS_PER_GROUP.bit_length() - 1)) == g_idx)
    el = jnp.where(emask, logits, neg)
    ee = jnp.exp(el - jnp.max(el, axis=1, keepdims=True))
    ep = ee / jnp.sum(ee, axis=1, keepdims=True)
    v1 = jnp.max(jnp.where(emask, ep, -1.0), axis=1, keepdims=True)
    i1 = jnp.min(jnp.where(emask & (ep == v1), e_lane, big), axis=1, keepdims=True)
    mask2 = emask & (e_lane != i1)
    v2 = jnp.max(jnp.where(mask2, ep, -1.0), axis=1, keepdims=True)
    i2 = jnp.min(jnp.where(mask2 & (ep == v2), e_lane, big), axis=1, keepdims=True)
    tot = v1 + v2
    gate1 = g_val * (v1 / tot)
    gate2 = g_val * (v2 / tot)
    return jnp.where(lane == 0, i1.astype(_f32),
                     jnp.where(lane == 1, i2.astype(_f32),
                               jnp.where(lane == 2, gate1, jnp.where(lane == 3, gate2, 0.0))))


def _mixer_kernel(xp_ref, xsamp_ref, w_in_ref, w_out_ref, conv_w_ref, sg_ref, sb_ref, ws_p_ref, ws_s_ref,
                  bias_p_ref, bias_s_ref, pw_ref, ps_ref, g1_ref, b1_ref, wr_ref,
                  br_ref, zst_ref, pst_ref,
                  x1_ref, route_ref, ztail_ref, ptail_ref, zs_ref, psamp_ref, vns_ref,
                  zext_p, pext_p, zext_s, pext_s, ybuf_ref):
    i = pl.program_id(0)
    x = jnp.where(i < N_PROMPT_TILES, xp_ref[...], xsamp_ref[...])
    xb = x.astype(_bf16)
    row = lax.broadcasted_iota(jnp.int32, (TM, 1), 0)

    @pl.when(i < N_PROMPT_TILES)
    def _prompt():
        t0 = (i % TILES_PER_SEQ) * TM

        @pl.when(t0 == 0)
        def _():
            zext_p[:, 0:CONV_HALO, :] = jnp.zeros((1, CONV_HALO, CONV_WIDTH), _f32)
            pext_p[:, 0:POOL_HALO, :] = jnp.zeros((1, POOL_HALO, POOL_WIDTH), _f32)

        z, p, _ = _mixer_body(i, xb, w_in_ref, conv_w_ref, sg_ref, sb_ref, ws_p_ref, bias_p_ref,
                              pw_ref, ps_ref, zext_p, pext_p, ybuf_ref, TM, t0 + row, CHUNK)
        z_last = z[TM - CONV_HALO:, :].reshape(1, CONV_HALO, CONV_WIDTH)
        p_last = p[TM - POOL_HALO:, :].reshape(1, POOL_HALO, POOL_WIDTH)
        zext_p[:, 0:CONV_HALO, :] = z_last
        pext_p[:, 0:POOL_HALO, :] = p_last
        ztail_ref[...] = z_last
        ptail_ref[...] = p_last

    @pl.when(i >= N_PROMPT_TILES)
    def _sample():
        zext_s[:, 0:CONV_HALO, :] = zst_ref[...]
        pext_s[:, 0:POOL_HALO, :] = pst_ref[...]
        z, p, vn = _mixer_body(i, xb, w_in_ref, conv_w_ref, sg_ref, sb_ref, ws_s_ref, bias_s_ref,
                               pw_ref, ps_ref, zext_s, pext_s, ybuf_ref, DEC_SEQ,
                               PAST_LEN + (row & (DEC_SEQ - 1)), DEC_SEQ)
        zs_ref[...] = z.reshape(SEQS_PER_TILE, DEC_SEQ, CONV_WIDTH)
        psamp_ref[...] = p.reshape(SEQS_PER_TILE, DEC_SEQ, POOL_WIDTH)
        vns_ref[...] = vn.reshape(SEQS_PER_TILE, DEC_SEQ, CHUNK_WIDTH)
        ztail_ref[...] = jnp.zeros(ztail_ref.shape, _f32)
        ptail_ref[...] = jnp.zeros(ptail_ref.shape, _f32)

    mix = _dot(ybuf_ref[...], w_out_ref[...])
    x1 = _layer_norm(ALPHA * x + mix, g1_ref[...], b1_ref[...])
    x1_ref[...] = x1
    route_ref[...] = _route(x1, wr_ref, br_ref)


def _const_spec(shape):
    nd = len(shape)
    return pl.BlockSpec(shape, lambda i: (0,) * nd, pipeline_mode=pl.Buffered(1))


def _mixer_call(x_p, x_s, w_in_b, w_out_b, conv_w, sg, sb, ws_p, ws_s, bias_p, bias_s, pw, ps, g1, b1,
                wr, br, zst, pst):
    samp = lambda i: (jnp.maximum(i - N_PROMPT_TILES, 0), 0, 0)
    tail = lambda i: (i // TILES_PER_SEQ, 0, 0)
    tok = lambda i: (i, 0)
    in_specs = [
        pl.BlockSpec((TM, D_MODEL), lambda i: (jnp.minimum(i, N_PROMPT_TILES - 1), 0)),
        pl.BlockSpec((TM, D_MODEL), lambda i: (jnp.maximum(i - N_PROMPT_TILES, 0), 0)),
        _const_spec((D_MODEL, IN_WIDTH)),
        _const_spec((D_MODEL, D_MODEL)),
        _const_spec((3, CONV_WIDTH)),
        _const_spec((1, CHUNK_WIDTH)),
        _const_spec((1, CHUNK_WIDTH)),
        _const_spec((N_CHUNK_HEADS, CHUNK, CHUNK)),
        _const_spec((N_CHUNK_HEADS, CHUNK, CHUNK)),
        _const_spec((CHUNK, N_CHUNK_HEADS)),
        _const_spec((CHUNK, N_CHUNK_HEADS)),
        _const_spec((4, POOL_GROUP_DIM, POOL_GROUP_DIM)),
        _const_spec((1, POOL_WIDTH)),
        _const_spec((1, D_MODEL)),
        _const_spec((1, D_MODEL)),
        _const_spec((D_MODEL, 2 * LANES)),
        _const_spec((1, LANES)),
        pl.BlockSpec((SEQS_PER_TILE, CONV_HALO, CONV_WIDTH), samp),
        pl.BlockSpec((SEQS_PER_TILE, POOL_HALO, POOL_WIDTH), samp),
    ]
    out_specs = [
        pl.BlockSpec((TM, D_MODEL), tok),
        pl.BlockSpec((TM, LANES), tok),
        pl.BlockSpec((1, CONV_HALO, CONV_WIDTH), tail),
        pl.BlockSpec((1, POOL_HALO, POOL_WIDTH), tail),
        pl.BlockSpec((SEQS_PER_TILE, DEC_SEQ, CONV_WIDTH), samp),
        pl.BlockSpec((SEQS_PER_TILE, DEC_SEQ, POOL_WIDTH), samp),
        pl.BlockSpec((SEQS_PER_TILE, DEC_SEQ, CHUNK_WIDTH), samp),
    ]
    out_shape = [
        jax.ShapeDtypeStruct((N_TOK, D_MODEL), _f32),
        jax.ShapeDtypeStruct((N_TOK, LANES), _f32),
        jax.ShapeDtypeStruct((BATCH + 1, CONV_HALO, CONV_WIDTH), _f32),
        jax.ShapeDtypeStruct((BATCH + 1, POOL_HALO, POOL_WIDTH), _f32),
        jax.ShapeDtypeStruct((DEC_BATCH, DEC_SEQ, CONV_WIDTH), _f32),
        jax.ShapeDtypeStruct((DEC_BATCH, DEC_SEQ, POOL_WIDTH), _f32),
        jax.ShapeDtypeStruct((DEC_BATCH, DEC_SEQ, CHUNK_WIDTH), _f32),
    ]
    scratch = [
        pltpu.VMEM((1, CONV_HALO + TM, CONV_WIDTH), _f32),
        pltpu.VMEM((1, POOL_HALO + TM, POOL_WIDTH), _f32),
        pltpu.VMEM((SEQS_PER_TILE, CONV_HALO + DEC_SEQ, CONV_WIDTH), _f32),
        pltpu.VMEM((SEQS_PER_TILE, POOL_HALO + DEC_SEQ, POOL_WIDTH), _f32),
        pltpu.VMEM((TM, D_MODEL), _bf16),
    ]
    return pl.pallas_call(
        _mixer_kernel,
        grid=(N_TILES,),
        in_specs=in_specs,
        out_specs=out_specs,
        out_shape=out_shape,
        scratch_shapes=scratch,
        compiler_params=pltpu.CompilerParams(
            dimension_semantics=("arbitrary",), vmem_limit_bytes=VMEM_LIMIT),
        name="mixer",
    )(x_p, x_s, w_in_b, w_out_b, conv_w, sg, sb, ws_p, ws_s, bias_p, bias_s, pw, ps, g1, b1,
      wr, br, zst, pst)


def _slots_kernel(ids_ref, slot_ref, tile_expert_ref, n_used_ref):
    ids = ids_ref[...]
    r_i = lax.broadcasted_iota(jnp.int32, (LANES, LANES), 0)
    c_i = lax.broadcasted_iota(jnp.int32, (LANES, LANES), 1)
    upper = (r_i < c_i).astype(_bf16)
    rr = lax.broadcasted_iota(jnp.int32, (ASSIGN_ROWS, ASSIGN_ROWS), 0)
    rc = lax.broadcasted_iota(jnp.int32, (ASSIGN_ROWS, ASSIGN_ROWS), 1)
    lower = (rc < rr).astype(_bf16)
    tile_row0 = (lax.broadcasted_iota(jnp.int32, (1, LANES), 1) * TE).astype(_f32)

    slot = jnp.zeros((ASSIGN_ROWS, LANES), _f32)
    tile_expert = jnp.zeros((1, LANES), _f32)
    off = jnp.zeros((1, 1), _f32)
    for e in range(N_EXPERTS):
        m = (ids == e).astype(_f32)
        within = _dot(m.astype(_bf16), upper)
        rowsum = jnp.sum(m, axis=1, keepdims=True)
        rowpre = _dot(lower, jnp.broadcast_to(rowsum, (ASSIGN_ROWS, LANES)).astype(_bf16))
        cnt = jnp.sum(rowsum, axis=0, keepdims=True)
        padded = jnp.floor((cnt + (TE - 1)) * (1.0 / TE)) * TE
        slot = slot + m * (off + rowpre + within)
        in_seg = (tile_row0 >= off) & (tile_row0 < off + padded)
        tile_expert = tile_expert + jnp.where(in_seg, float(e), 0.0)
        off = off + padded
    tile_expert = jnp.where(tile_row0 >= off, float(N_EXPERTS - 1), tile_expert)
    slot_ref[...] = slot.astype(jnp.int32)
    tile_expert_ref[...] = tile_expert.astype(jnp.int32)
    n_used_ref[...] = jnp.broadcast_to(off * (1.0 / TE), (1, LANES)).astype(jnp.int32)


def _slots_call(ids):
    return pl.pallas_call(
        _slots_kernel,
        out_shape=[
            jax.ShapeDtypeStruct((ASSIGN_ROWS, LANES), jnp.int32),
            jax.ShapeDtypeStruct((1, LANES), jnp.int32),
            jax.ShapeDtypeStruct((1, LANES), jnp.int32),
        ],
        name="slots",
    )(ids)


def _dispatch_kernel(slot_ref, x1_ref, xs_in_hbm, xs_hbm, stage, sems):
    del xs_in_hbm
    i = pl.program_id(0)
    tok0 = i * TM
    b = i % 2

    def wait_all(slot):
        for k in range(2):
            pltpu.make_async_copy(stage.at[slot], xs_hbm.at[pl.ds(0, TM)], sems.at[slot, k]).wait()

    stage[b] = x1_ref[...]

    def start(r, carry):
        for k in range(2):
            pltpu.make_async_copy(stage.at[b, pl.ds(r, 1)],
                                  xs_hbm.at[pl.ds(slot_ref[k * N_TOK + tok0 + r], 1)],
                                  sems.at[b, k]).start(priority=k)
        return carry

    lax.fori_loop(0, TM, start, 0, unroll=16)

    @pl.when(i > 0)
    def _():
        wait_all(1 - b)

    @pl.when(i == pl.num_programs(0) - 1)
    def _():
        wait_all(b)


def _dispatch_call(slot_flat, x1, xs):
    return pl.pallas_call(
        _dispatch_kernel,
        grid_spec=pltpu.PrefetchScalarGridSpec(
            num_scalar_prefetch=1,
            grid=(N_TILES,),
            in_specs=[pl.BlockSpec((TM, D_MODEL), lambda i, s: (i, 0)), pl.BlockSpec(memory_space=pl.ANY)],
            out_specs=pl.BlockSpec(memory_space=pl.ANY),
            scratch_shapes=[pltpu.VMEM((2, TM, D_MODEL), _f32), pltpu.SemaphoreType.DMA((2, 2))],
        ),
        out_shape=jax.ShapeDtypeStruct((N_SLOTS, D_MODEL), _f32),
        input_output_aliases={2: 0},
        compiler_params=pltpu.CompilerParams(dimension_semantics=("arbitrary",)),
        name="dispatch",
    )(slot_flat, x1, xs)


def _experts_kernel(te_ref, nu_ref, xs_ref, wg_ref, wu_ref, wd_ref, y_ref, wg_b, wu_b, wd_b):
    j = pl.program_id(0)

    @pl.when(j < nu_ref[0])
    def _():
        prev = te_ref[jnp.maximum(j - 1, 0)]

        @pl.when((j == 0) | (te_ref[j] != prev))
        def _():
            wg_b[...] = wg_ref[0, 0].astype(_bf16)
            wu_b[...] = wu_ref[0, 0].astype(_bf16)
            wd_b[...] = wd_ref[0, 0].astype(_bf16)

        xb = xs_ref[...].astype(_bf16)
        hg = _dot(xb, wg_b[...])
        hu = _dot(xb, wu_b[...])
        a = hg / (1.0 + jnp.exp(-hg)) * hu
        y_ref[...] = _dot(a.astype(_bf16), wd_b[...])


def _experts_call(layer, tile_expert, n_used, xs, wg, wu, wd):
    row = lambda j, te, nu: (jnp.minimum(j, nu[0] - 1), 0)
    wsel = lambda j, te, nu: (layer, te[j], 0, 0)
    return pl.pallas_call(
        _experts_kernel,
        grid_spec=pltpu.PrefetchScalarGridSpec(
            num_scalar_prefetch=2,
            grid=(N_ETILES,),
            in_specs=[
                pl.BlockSpec((TE, D_MODEL), row),
                pl.BlockSpec((1, 1, D_MODEL, D_EXPERT), wsel),
                pl.BlockSpec((1, 1, D_MODEL, D_EXPERT), wsel),
                pl.BlockSpec((1, 1, D_EXPERT, D_MODEL), wsel),
            ],
            out_specs=pl.BlockSpec((TE, D_MODEL), row),
            scratch_shapes=[
                pltpu.VMEM((D_MODEL, D_EXPERT), _bf16),
                pltpu.VMEM((D_MODEL, D_EXPERT), _bf16),
                pltpu.VMEM((D_EXPERT, D_MODEL), _bf16),
            ],
        ),
        out_shape=jax.ShapeDtypeStruct((N_SLOTS, D_MODEL), _f32),
        input_output_aliases={2: 0},
        compiler_params=pltpu.CompilerParams(
            dimension_semantics=("arbitrary",), vmem_limit_bytes=VMEM_LIMIT),
        name="experts",
    )(tile_expert, n_used, xs, wg, wu, wd)


def _combine_kernel(slot_ref, x1_ref, route_ref, g2_ref, b2_ref, y_hbm, out_p_ref, out_s_ref,
                    ybuf, sems):
    i = pl.program_id(0)
    b = i % 2

    def gather(tile, slot):
        tok0 = tile * TM

        def start(r, carry):
            for k in range(2):
                pltpu.make_async_copy(y_hbm.at[pl.ds(slot_ref[k * N_TOK + tok0 + r], 1)],
                                      ybuf.at[slot, k, pl.ds(r, 1)], sems.at[slot, k]).start(priority=k)
            return carry

        lax.fori_loop(0, TM, start, 0, unroll=16)

    @pl.when(i == 0)
    def _():
        gather(0, 0)

    @pl.when(i + 1 < pl.num_programs(0))
    def _():
        gather(i + 1, 1 - b)

    for k in range(2):
        pltpu.make_async_copy(y_hbm.at[pl.ds(0, TM)], ybuf.at[b, k], sems.at[b, k]).wait()

    route = route_ref[...]
    moe = route[:, 2:3] * ybuf[b, 0] + route[:, 3:4] * ybuf[b, 1]
    out = _layer_norm(ALPHA * x1_ref[...] + moe, g2_ref[...], b2_ref[...])

    @pl.when(i < N_PROMPT_TILES)
    def _():
        out_p_ref[...] = out

    @pl.when(i >= N_PROMPT_TILES)
    def _():
        out_s_ref[...] = out


def _combine_call(slot_flat, x1, route, g2, b2, y):
    tok = lambda i, s: (i, 0)
    const = lambda i, s: (0, 0)
    return pl.pallas_call(
        _combine_kernel,
        grid_spec=pltpu.PrefetchScalarGridSpec(
            num_scalar_prefetch=1,
            grid=(N_TILES,),
            in_specs=[
                pl.BlockSpec((TM, D_MODEL), tok),
                pl.BlockSpec((TM, LANES), tok),
                pl.BlockSpec((1, D_MODEL), const),
                pl.BlockSpec((1, D_MODEL), const),
                pl.BlockSpec(memory_space=pl.ANY),
            ],
            out_specs=[
                pl.BlockSpec((TM, D_MODEL), lambda i, s: (jnp.minimum(i, N_PROMPT_TILES - 1), 0)),
                pl.BlockSpec((TM, D_MODEL), lambda i, s: (jnp.maximum(i - N_PROMPT_TILES, 0), 0)),
            ],
            scratch_shapes=[
                pltpu.VMEM((2, 2, TM, D_MODEL), _f32),
                pltpu.SemaphoreType.DMA((2, 2)),
            ],
        ),
        out_shape=[jax.ShapeDtypeStruct((N_PROMPT, D_MODEL), _f32),
                   jax.ShapeDtypeStruct((N_SAMPLE, D_MODEL), _f32)],
        compiler_params=pltpu.CompilerParams(dimension_semantics=("arbitrary",)),
        name="combine",
    )(slot_flat, x1, route, g2, b2, y)


def kernel(x_prompt, x_sample, state_conv, state_pool, w_in, conv_w, sgu_ln_g, sgu_ln_b, sgu_w, sgu_b, pool_w, pool_scale, w_out, ln1_g, ln1_b, router_group_w, router_group_b, router_expert_w, router_expert_b, expert_w_gate, expert_w_up, expert_w_down, ln2_g, ln2_b):
    x_p = x_prompt.reshape(N_PROMPT, D_MODEL)
    x_s = x_sample.reshape(N_SAMPLE, D_MODEL)
    xs = jnp.zeros((N_SLOTS, D_MODEL), _f32)
    reps = CHUNK // DEC_SEQ

    conv_p, pool_p, conv_s, pool_s, v_s = [], [], [], [], []
    for l in range(DEPTH):
        w_r = jnp.concatenate([router_group_w[l], router_expert_w[l]], axis=1)
        w_r = jnp.pad(w_r, ((0, 0), (0, LANES - w_r.shape[1])))
        wr_hi = w_r.astype(_bf16)
        wr = jnp.concatenate([wr_hi, (w_r - wr_hi.astype(_f32)).astype(_bf16)], axis=1)
        b_r = jnp.concatenate([router_group_b[l], router_expert_b[l]])
        b_r = jnp.pad(b_r, (0, LANES - b_r.shape[0])).reshape(1, LANES)
        ws_s = jnp.tile(sgu_w[l][:, :DEC_SEQ, :DEC_SEQ], (1, reps, reps))
        bias_p = sgu_b[l].T
        bias_s = jnp.tile(sgu_b[l][:, :DEC_SEQ].T, (reps, 1))
        zst = jnp.pad(state_conv[l], ((0, 0), (CONV_HALO - 2, 0), (0, 0)))
        pst = jnp.pad(state_pool[l], ((0, 0), (POOL_HALO - POOL_BUF, 0), (0, 0)))

        x1, route, ztail, ptail, zs, psamp, vns = _mixer_call(
            x_p, x_s, w_in[l].astype(_bf16), w_out[l].astype(_bf16), conv_w[l],
            sgu_ln_g[l].reshape(1, -1), sgu_ln_b[l].reshape(1, -1), sgu_w[l], ws_s, bias_p, bias_s,
            pool_w[l], pool_scale[l].reshape(1, -1), ln1_g[l].reshape(1, -1), ln1_b[l].reshape(1, -1),
            wr, b_r, zst, pst)

        ids = route[:, 0:2].astype(jnp.int32).T.reshape(ASSIGN_ROWS, LANES)
        slot, tile_expert, n_used = _slots_call(ids)
        slot_flat = slot.reshape(N_ASSIGN)
        xs = _dispatch_call(slot_flat, x1, xs)
        y = _experts_call(l, tile_expert.reshape(LANES), n_used.reshape(LANES)[:1], xs,
                          expert_w_gate, expert_w_up, expert_w_down)
        x_p, x_s = _combine_call(slot_flat, x1, route, ln2_g[l].reshape(1, -1), ln2_b[l].reshape(1, -1), y)
        xs = y

        conv_p.append(ztail[:BATCH, CONV_HALO - 2:])
        pool_p.append(ptail[:BATCH, POOL_HALO - POOL_BUF:])
        conv_s.append(zs[:, DEC_SEQ - 2:])
        pool_s.append(jnp.concatenate([state_pool[l][:, DEC_SEQ:], psamp], axis=1))
        v_s.append(vns)

    y_prompt = x_p.reshape(BATCH, SEQ, D_MODEL)
    y_sample = x_s.reshape(DEC_BATCH, DEC_SEQ, D_MODEL)
    return (y_prompt, y_sample, jnp.stack(conv_p), jnp.stack(pool_p), jnp.stack(conv_s),
            jnp.stack(pool_s), jnp.stack(v_s))
```

```python
import functools

import jax
import jax.numpy as jnp
from jax import lax
from jax.experimental import pallas as pl
from jax.experimental.pallas import tpu as pltpu

D_MODEL = 2048
BATCH = 4
SEQ = 2048
DEPTH = 4
DEC_BATCH = 128
DEC_SEQ = 8
PAST_LEN = 16384
HEAD_DIM = 128
CONV_WIDTH = 768
POOL_WIDTH = 512
CHUNK_WIDTH = 768
CHUNK = 128
N_CHUNK_HEADS = 6
POOL_WINDOWS = (2, 4, 8, 16)
POOL_GROUP_DIM = 128
POOL_BUF = 15
IN_WIDTH = 4352
N_EXPERT_GROUPS = 4
EXPERTS_PER_GROUP = 4
N_EXPERTS = 16
D_EXPERT = 512
ALPHA = (2 * DEPTH) ** 0.25
LN_EPS = 1e-5

N_PROMPT = BATCH * SEQ
N_SAMPLE = DEC_BATCH * DEC_SEQ
N_TOK = N_PROMPT + N_SAMPLE
N_ASSIGN = 2 * N_TOK

LANES = 128
SUBLANES = 8
CONV_HALO = 8
POOL_HALO = 16

TM = 256
TILES_PER_SEQ = SEQ // TM
N_PROMPT_TILES = N_PROMPT // TM
N_SAMPLE_TILES = N_SAMPLE // TM
N_TILES = N_PROMPT_TILES + N_SAMPLE_TILES
SEQS_PER_TILE = TM // DEC_SEQ

TE = 512
N_SLOTS = N_ASSIGN + N_EXPERTS * TE
N_ETILES = N_SLOTS // TE
ASSIGN_ROWS = N_ASSIGN // LANES

CUT_B, CUT_C, CUT_H, CUT_U, CUT_V, CUT_P = 0, 768, 1536, 2304, 3072, 3840

VMEM_LIMIT = 60 * 1024 * 1024

_f32 = jnp.float32
_bf16 = jnp.bfloat16


def _dot(a, b):
    return jnp.dot(a, b, preferred_element_type=_f32)


def _layer_norm(r, g, b):
    mu = jnp.mean(r, axis=-1, keepdims=True)
    c = r - mu
    var = jnp.mean(c * c, axis=-1, keepdims=True)
    return c * lax.rsqrt(var + LN_EPS) * g + b


def _mixer_body(i, xb, w_in_ref, conv_w_ref, sg_ref, sb_ref, ws_ref, bias_ref, pw_ref, ps_ref,
                zext_ref, pext_ref, ybuf_ref, seq_len, pos, chunk_len):
    n_seq = TM // seq_len

    z = _dot(xb, w_in_ref[:, CUT_C:CUT_H]) * _dot(xb, w_in_ref[:, CUT_H:CUT_U])
    zext_ref[:, CONV_HALO:, :] = z.reshape(n_seq, seq_len, CONV_WIDTH)
    cw = conv_w_ref[...]
    conv = (cw[0:1, :] * zext_ref[:, CONV_HALO - 2:CONV_HALO - 2 + seq_len, :]
            + cw[1:2, :] * zext_ref[:, CONV_HALO - 1:CONV_HALO - 1 + seq_len, :]
            + cw[2:3, :] * zext_ref[:, CONV_HALO:, :])
    y_conv = _dot(xb, w_in_ref[:, CUT_B:CUT_C]) * conv.reshape(TM, CONV_WIDTH)
    ybuf_ref[:, 0:CONV_WIDTH] = y_conv.astype(_bf16)

    v = _dot(xb, w_in_ref[:, CUT_V:CUT_P])
    vn = _layer_norm(v, sg_ref[...], sb_ref[...])
    vnb = vn.astype(_bf16)
    r_i = lax.broadcasted_iota(jnp.int32, (CHUNK, CHUNK), 0)
    c_i = lax.broadcasted_iota(jnp.int32, (CHUNK, CHUNK), 1)
    shift = chunk_len.bit_length() - 1
    mask = (c_i <= r_i) & ((r_i >> shift) == (c_i >> shift))
    for h in range(N_CHUNK_HEADS):
        m_h = jnp.where(mask, ws_ref[h], 0.0).astype(_bf16)
        b_h = bias_ref[:, h:h + 1]
        cols = slice(h * HEAD_DIM, (h + 1) * HEAD_DIM)
        u_h = _dot(xb, w_in_ref[:, CUT_U + h * HEAD_DIM:CUT_U + (h + 1) * HEAD_DIM])
        for c in range(TM // CHUNK):
            rows = slice(c * CHUNK, (c + 1) * CHUNK)
            mixed = _dot(m_h, vnb[rows, cols]) + b_h
            ybuf_ref[rows, CONV_WIDTH + h * HEAD_DIM:CONV_WIDTH + (h + 1) * HEAD_DIM] = (
                u_h[rows, :] * mixed).astype(_bf16)

    p = _dot(xb, w_in_ref[:, CUT_P:IN_WIDTH])
    pext_ref[:, POOL_HALO:, :] = p.reshape(n_seq, seq_len, POOL_WIDTH)
    posf = (pos + 1).astype(_f32)
    for g, w in enumerate(POOL_WINDOWS):
        cols = slice(g * POOL_GROUP_DIM, (g + 1) * POOL_GROUP_DIM)
        s = pext_ref[:, POOL_HALO:, cols]
        for j in range(1, w):
            s = s + pext_ref[:, POOL_HALO - j:POOL_HALO - j + seq_len, cols]
        cnt = jnp.minimum(float(w), posf)
        diff = s.reshape(TM, POOL_GROUP_DIM) / cnt - p[:, cols]
        y_g = _dot(diff.astype(_bf16), pw_ref[g].astype(_bf16)) * ps_ref[:, cols]
        ybuf_ref[:, CONV_WIDTH + CHUNK_WIDTH + g * POOL_GROUP_DIM:
                 CONV_WIDTH + CHUNK_WIDTH + (g + 1) * POOL_GROUP_DIM] = y_g.astype(_bf16)
    return z, p, vn


def _route(x1, wr_ref, br_ref):
    x_hi = x1.astype(_bf16)
    x_lo = (x1 - x_hi.astype(_f32)).astype(_bf16)
    w_hi = wr_ref[:, :LANES]
    logits = _dot(x_hi, w_hi) + _dot(x_lo, w_hi) + _dot(x_hi, wr_ref[:, LANES:]) + br_ref[...]
    lane = lax.broadcasted_iota(jnp.int32, logits.shape, 1)
    neg = -jnp.inf
    big = jnp.int32(1 << 20)

    gmask = lane < N_EXPERT_GROUPS
    gl = jnp.where(gmask, logits, neg)
    ge = jnp.exp(gl - jnp.max(gl, axis=1, keepdims=True))
    gp = ge / jnp.sum(ge, axis=1, keepdims=True)
    g_val = jnp.max(gp, axis=1, keepdims=True)
    g_idx = jnp.min(jnp.where(gmask & (gp == g_val), lane, big), axis=1, keepdims=True)

    e_lane = lane - N_EXPERT_GROUPS
    emask = (e_lane >= 0) & (e_lane < N_EXPERTS) & ((e_lane >> (EXPERTS_PER_GROUP.bit_length() - 1)) == g_idx)
    el = jnp.where(emask, logits, neg)
    ee = jnp.exp(el - jnp.max(el, axis=1, keepdims=True))
    ep = ee / jnp.sum(ee, axis=1, keepdims=True)
    v1 = jnp.max(jnp.where(emask, ep, -1.0), axis=1, keepdims=True)
    i1 = jnp.min(jnp.where(emask & (ep == v1), e_lane, big), axis=1, keepdims=True)
    mask2 = emask & (e_lane != i1)
    v2 = jnp.max(jnp.where(mask2, ep, -1.0), axis=1, keepdims=True)
    i2 = jnp.min(jnp.where(mask2 & (ep == v2), e_lane, big), axis=1, keepdims=True)
    tot = v1 + v2
    gate1 = g_val * (v1 / tot)
    gate2 = g_val * (v2 / tot)
    return jnp.where(lane == 0, i1.astype(_f32),
                     jnp.where(lane == 1, i2.astype(_f32),
                               jnp.where(lane == 2, gate1, jnp.where(lane == 3, gate2, 0.0))))


def _mixer_kernel(xp_ref, xsamp_ref, w_in_ref, w_out_ref, conv_w_ref, sg_ref, sb_ref, ws_p_ref, ws_s_ref,
                  bias_p_ref, bias_s_ref, pw_ref, ps_ref, g1_ref, b1_ref, wr_ref,
                  br_ref, zst_ref, pst_ref,
                  x1_ref, route_ref, ztail_ref, ptail_ref, zs_ref, psamp_ref, vns_ref,
                  zext_p, pext_p, zext_s, pext_s, ybuf_ref):
    i = pl.program_id(0)
    x = jnp.where(i < N_PROMPT_TILES, xp_ref[...], xsamp_ref[...])
    xb = x.astype(_bf16)
    row = lax.broadcasted_iota(jnp.int32, (TM, 1), 0)

    @pl.when(i < N_PROMPT_TILES)
    def _prompt():
        t0 = (i % TILES_PER_SEQ) * TM

        @pl.when(t0 == 0)
        def _():
            zext_p[:, 0:CONV_HALO, :] = jnp.zeros((1, CONV_HALO, CONV_WIDTH), _f32)
            pext_p[:, 0:POOL_HALO, :] = jnp.zeros((1, POOL_HALO, POOL_WIDTH), _f32)

        z, p, _ = _mixer_body(i, xb, w_in_ref, conv_w_ref, sg_ref, sb_ref, ws_p_ref, bias_p_ref,
                              pw_ref, ps_ref, zext_p, pext_p, ybuf_ref, TM, t0 + row, CHUNK)
        z_last = z[TM - CONV_HALO:, :].reshape(1, CONV_HALO, CONV_WIDTH)
        p_last = p[TM - POOL_HALO:, :].reshape(1, POOL_HALO, POOL_WIDTH)
        zext_p[:, 0:CONV_HALO, :] = z_last
        pext_p[:, 0:POOL_HALO, :] = p_last
        ztail_ref[...] = z_last
        ptail_ref[...] = p_last

    @pl.when(i >= N_PROMPT_TILES)
    def _sample():
        zext_s[:, 0:CONV_HALO, :] = zst_ref[...]
        pext_s[:, 0:POOL_HALO, :] = pst_ref[...]
        z, p, vn = _mixer_body(i, xb, w_in_ref, conv_w_ref, sg_ref, sb_ref, ws_s_ref, bias_s_ref,
                               pw_ref, ps_ref, zext_s, pext_s, ybuf_ref, DEC_SEQ,
                               PAST_LEN + (row & (DEC_SEQ - 1)), DEC_SEQ)
        zs_ref[...] = z.reshape(SEQS_PER_TILE, DEC_SEQ, CONV_WIDTH)
        psamp_ref[...] = p.reshape(SEQS_PER_TILE, DEC_SEQ, POOL_WIDTH)
        vns_ref[...] = vn.reshape(SEQS_PER_TILE, DEC_SEQ, CHUNK_WIDTH)
        ztail_ref[...] = jnp.zeros(ztail_ref.shape, _f32)
        ptail_ref[...] = jnp.zeros(ptail_ref.shape, _f32)

    mix = _dot(ybuf_ref[...], w_out_ref[...])
    x1 = _layer_norm(ALPHA * x + mix, g1_ref[...], b1_ref[...])
    x1_ref[...] = x1
    route_ref[...] = _route(x1, wr_ref, br_ref)


def _layer_spec(layer, shape):
    nd = len(shape)
    return pl.BlockSpec((None,) + shape, lambda i: (layer,) + (0,) * nd, pipeline_mode=pl.Buffered(1))


def _mixer_call(layer, x_p, x_s, w_in_b, w_out_b, conv_w, sg, sb, ws_p, ws_s, bias_p, bias_s, pw, ps,
                g1, b1, wr, br, zst, pst):
    samp = lambda i: (jnp.maximum(i - N_PROMPT_TILES, 0), 0, 0)
    samp_l = lambda i: (layer, jnp.maximum(i - N_PROMPT_TILES, 0), 0, 0)
    tail = lambda i: (i // TILES_PER_SEQ, 0, 0)
    tok = lambda i: (i, 0)
    in_specs = [
        pl.BlockSpec((TM, D_MODEL), lambda i: (jnp.minimum(i, N_PROMPT_TILES - 1), 0)),
        pl.BlockSpec((TM, D_MODEL), lambda i: (jnp.maximum(i - N_PROMPT_TILES, 0), 0)),
        _layer_spec(layer, (D_MODEL, IN_WIDTH)),
        _layer_spec(layer, (D_MODEL, D_MODEL)),
        _layer_spec(layer, (3, CONV_WIDTH)),
        _layer_spec(layer, (1, CHUNK_WIDTH)),
        _layer_spec(layer, (1, CHUNK_WIDTH)),
        _layer_spec(layer, (N_CHUNK_HEADS, CHUNK, CHUNK)),
        _layer_spec(layer, (N_CHUNK_HEADS, CHUNK, CHUNK)),
        _layer_spec(layer, (CHUNK, N_CHUNK_HEADS)),
        _layer_spec(layer, (CHUNK, N_CHUNK_HEADS)),
        _layer_spec(layer, (4, POOL_GROUP_DIM, POOL_GROUP_DIM)),
        _layer_spec(layer, (1, POOL_WIDTH)),
        _layer_spec(layer, (1, D_MODEL)),
        _layer_spec(layer, (1, D_MODEL)),
        _layer_spec(layer, (D_MODEL, 2 * LANES)),
        _layer_spec(layer, (1, LANES)),
        pl.BlockSpec((None, SEQS_PER_TILE, CONV_HALO, CONV_WIDTH), samp_l),
        pl.BlockSpec((None, SEQS_PER_TILE, POOL_HALO, POOL_WIDTH), samp_l),
    ]
    out_specs = [
        pl.BlockSpec((TM, D_MODEL), tok),
        pl.BlockSpec((TM, LANES), tok),
        pl.BlockSpec((1, CONV_HALO, CONV_WIDTH), tail),
        pl.BlockSpec((1, POOL_HALO, POOL_WIDTH), tail),
        pl.BlockSpec((SEQS_PER_TILE, DEC_SEQ, CONV_WIDTH), samp),
        pl.BlockSpec((SEQS_PER_TILE, DEC_SEQ, POOL_WIDTH), samp),
        pl.BlockSpec((SEQS_PER_TILE, DEC_SEQ, CHUNK_WIDTH), samp),
    ]
    out_shape = [
        jax.ShapeDtypeStruct((N_TOK, D_MODEL), _f32),
        jax.ShapeDtypeStruct((N_TOK, LANES), _f32),
        jax.ShapeDtypeStruct((BATCH + 1, CONV_HALO, CONV_WIDTH), _f32),
        jax.ShapeDtypeStruct((BATCH + 1, POOL_HALO, POOL_WIDTH), _f32),
        jax.ShapeDtypeStruct((DEC_BATCH, DEC_SEQ, CONV_WIDTH), _f32),
        jax.ShapeDtypeStruct((DEC_BATCH, DEC_SEQ, POOL_WIDTH), _f32),
        jax.ShapeDtypeStruct((DEC_BATCH, DEC_SEQ, CHUNK_WIDTH), _f32),
    ]
    scratch = [
        pltpu.VMEM((1, CONV_HALO + TM, CONV_WIDTH), _f32),
        pltpu.VMEM((1, POOL_HALO + TM, POOL_WIDTH), _f32),
        pltpu.VMEM((SEQS_PER_TILE, CONV_HALO + DEC_SEQ, CONV_WIDTH), _f32),
        pltpu.VMEM((SEQS_PER_TILE, POOL_HALO + DEC_SEQ, POOL_WIDTH), _f32),
        pltpu.VMEM((TM, D_MODEL), _bf16),
    ]
    return pl.pallas_call(
        _mixer_kernel,
        grid=(N_TILES,),
        in_specs=in_specs,
        out_specs=out_specs,
        out_shape=out_shape,
        scratch_shapes=scratch,
        compiler_params=pltpu.CompilerParams(
            dimension_semantics=("arbitrary",), vmem_limit_bytes=VMEM_LIMIT),
        name="mixer",
    )(x_p, x_s, w_in_b, w_out_b, conv_w, sg, sb, ws_p, ws_s, bias_p, bias_s, pw, ps, g1, b1,
      wr, br, zst, pst)


def _slots_kernel(ids_ref, slot_ref, tile_expert_ref, n_used_ref):
    ids = ids_ref[...]
    r_i = lax.broadcasted_iota(jnp.int32, (LANES, LANES), 0)
    c_i = lax.broadcasted_iota(jnp.int32, (LANES, LANES), 1)
    upper = (r_i < c_i).astype(_bf16)
    rr = lax.broadcasted_iota(jnp.int32, (ASSIGN_ROWS, ASSIGN_ROWS), 0)
    rc = lax.broadcasted_iota(jnp.int32, (ASSIGN_ROWS, ASSIGN_ROWS), 1)
    lower = (rc < rr).astype(_bf16)
    tile_row0 = (lax.broadcasted_iota(jnp.int32, (1, LANES), 1) * TE).astype(_f32)

    slot = jnp.zeros((ASSIGN_ROWS, LANES), _f32)
    tile_expert = jnp.zeros((1, LANES), _f32)
    off = jnp.zeros((1, 1), _f32)
    for e in range(N_EXPERTS):
        m = (ids == e).astype(_f32)
        within = _dot(m.astype(_bf16), upper)
        rowsum = jnp.sum(m, axis=1, keepdims=True)
        rowpre = _dot(lower, jnp.broadcast_to(rowsum, (ASSIGN_ROWS, LANES)).astype(_bf16))
        cnt = jnp.sum(rowsum, axis=0, keepdims=True)
        padded = jnp.floor((cnt + (TE - 1)) * (1.0 / TE)) * TE
        slot = slot + m * (off + rowpre + within)
        in_seg = (tile_row0 >= off) & (tile_row0 < off + padded)
        tile_expert = tile_expert + jnp.where(in_seg, float(e), 0.0)
        off = off + padded
    tile_expert = jnp.where(tile_row0 >= off, float(N_EXPERTS - 1), tile_expert)
    slot_ref[...] = slot.astype(jnp.int32)
    tile_expert_ref[...] = tile_expert.astype(jnp.int32)
    n_used_ref[...] = jnp.broadcast_to(off * (1.0 / TE), (1, LANES)).astype(jnp.int32)


def _slots_call(ids):
    return pl.pallas_call(
        _slots_kernel,
        out_shape=[
            jax.ShapeDtypeStruct((ASSIGN_ROWS, LANES), jnp.int32),
            jax.ShapeDtypeStruct((1, LANES), jnp.int32),
            jax.ShapeDtypeStruct((1, LANES), jnp.int32),
        ],
        name="slots",
    )(ids)


def _dispatch_kernel(slot_ref, x1_ref, xs_in_hbm, xs_hbm, stage, sems):
    del xs_in_hbm
    i = pl.program_id(0)
    tok0 = i * TM
    b = i % 2

    def wait_all(slot):
        for k in range(2):
            pltpu.make_async_copy(stage.at[slot], xs_hbm.at[pl.ds(0, TM)], sems.at[slot, k]).wait()

    stage[b] = x1_ref[...]

    def start(r, carry):
        for k in range(2):
            pltpu.make_async_copy(stage.at[b, pl.ds(r, 1)],
                                  xs_hbm.at[pl.ds(slot_ref[k * N_TOK + tok0 + r], 1)],
                                  sems.at[b, k]).start(priority=k)
        return carry

    lax.fori_loop(0, TM, start, 0, unroll=16)

    @pl.when(i > 0)
    def _():
        wait_all(1 - b)

    @pl.when(i == pl.num_programs(0) - 1)
    def _():
        wait_all(b)


def _dispatch_call(slot_flat, x1, xs):
    return pl.pallas_call(
        _dispatch_kernel,
        grid_spec=pltpu.PrefetchScalarGridSpec(
            num_scalar_prefetch=1,
            grid=(N_TILES,),
            in_specs=[pl.BlockSpec((TM, D_MODEL), lambda i, s: (i, 0)), pl.BlockSpec(memory_space=pl.ANY)],
            out_specs=pl.BlockSpec(memory_space=pl.ANY),
            scratch_shapes=[pltpu.VMEM((2, TM, D_MODEL), _f32), pltpu.SemaphoreType.DMA((2, 2))],
        ),
        out_shape=jax.ShapeDtypeStruct((N_SLOTS, D_MODEL), _f32),
        input_output_aliases={2: 0},
        compiler_params=pltpu.CompilerParams(dimension_semantics=("arbitrary",)),
        name="dispatch",
    )(slot_flat, x1, xs)


def _experts_kernel(te_ref, nu_ref, xs_ref, wg_ref, wu_ref, wd_ref, y_ref, wg_b, wu_b, wd_b):
    j = pl.program_id(0)

    @pl.when(j < nu_ref[0])
    def _():
        prev = te_ref[jnp.maximum(j - 1, 0)]

        @pl.when((j == 0) | (te_ref[j] != prev))
        def _():
            wg_b[...] = wg_ref[0, 0].astype(_bf16)
            wu_b[...] = wu_ref[0, 0].astype(_bf16)
            wd_b[...] = wd_ref[0, 0].astype(_bf16)

        xb = xs_ref[...].astype(_bf16)
        hg = _dot(xb, wg_b[...])
        hu = _dot(xb, wu_b[...])
        a = hg / (1.0 + jnp.exp(-hg)) * hu
        y_ref[...] = _dot(a.astype(_bf16), wd_b[...])


def _experts_call(layer, tile_expert, n_used, xs, wg, wu, wd):
    row = lambda j, te, nu: (jnp.minimum(j, nu[0] - 1), 0)
    wsel = lambda j, te, nu: (layer, te[j], 0, 0)
    return pl.pallas_call(
        _experts_kernel,
        grid_spec=pltpu.PrefetchScalarGridSpec(
            num_scalar_prefetch=2,
            grid=(N_ETILES,),
            in_specs=[
                pl.BlockSpec((TE, D_MODEL), row),
                pl.BlockSpec((1, 1, D_MODEL, D_EXPERT), wsel),
                pl.BlockSpec((1, 1, D_MODEL, D_EXPERT), wsel),
                pl.BlockSpec((1, 1, D_EXPERT, D_MODEL), wsel),
            ],
            out_specs=pl.BlockSpec((TE, D_MODEL), row),
            scratch_shapes=[
                pltpu.VMEM((D_MODEL, D_EXPERT), _bf16),
                pltpu.VMEM((D_MODEL, D_EXPERT), _bf16),
                pltpu.VMEM((D_EXPERT, D_MODEL), _bf16),
            ],
        ),
        out_shape=jax.ShapeDtypeStruct((N_SLOTS, D_MODEL), _f32),
        input_output_aliases={2: 0},
        compiler_params=pltpu.CompilerParams(
            dimension_semantics=("arbitrary",), vmem_limit_bytes=VMEM_LIMIT),
        name="experts",
    )(tile_expert, n_used, xs, wg, wu, wd)


def _combine_kernel(slot_ref, x1_ref, route_ref, g2_ref, b2_ref, y_hbm, out_p_ref, out_s_ref,
                    ybuf, sems):
    i = pl.program_id(0)
    b = i % 2

    def gather(tile, slot):
        tok0 = tile * TM

        def start(r, carry):
            for k in range(2):
                pltpu.make_async_copy(y_hbm.at[pl.ds(slot_ref[k * N_TOK + tok0 + r], 1)],
                                      ybuf.at[slot, k, pl.ds(r, 1)], sems.at[slot, k]).start(priority=k)
            return carry

        lax.fori_loop(0, TM, start, 0, unroll=16)

    @pl.when(i == 0)
    def _():
        gather(0, 0)

    @pl.when(i + 1 < pl.num_programs(0))
    def _():
        gather(i + 1, 1 - b)

    for k in range(2):
        pltpu.make_async_copy(y_hbm.at[pl.ds(0, TM)], ybuf.at[b, k], sems.at[b, k]).wait()

    route = route_ref[...]
    moe = route[:, 2:3] * ybuf[b, 0] + route[:, 3:4] * ybuf[b, 1]
    out = _layer_norm(ALPHA * x1_ref[...] + moe, g2_ref[...], b2_ref[...])

    @pl.when(i < N_PROMPT_TILES)
    def _():
        out_p_ref[...] = out

    @pl.when(i >= N_PROMPT_TILES)
    def _():
        out_s_ref[...] = out


def _combine_call(slot_flat, x1, route, g2, b2, y):
    tok = lambda i, s: (i, 0)
    const = lambda i, s: (0, 0)
    return pl.pallas_call(
        _combine_kernel,
        grid_spec=pltpu.PrefetchScalarGridSpec(
            num_scalar_prefetch=1,
            grid=(N_TILES,),
            in_specs=[
                pl.BlockSpec((TM, D_MODEL), tok),
                pl.BlockSpec((TM, LANES), tok),
                pl.BlockSpec((1, D_MODEL), const),
                pl.BlockSpec((1, D_MODEL), const),
                pl.BlockSpec(memory_space=pl.ANY),
            ],
            out_specs=[
                pl.BlockSpec((TM, D_MODEL), lambda i, s: (jnp.minimum(i, N_PROMPT_TILES - 1), 0)),
                pl.BlockSpec((TM, D_MODEL), lambda i, s: (jnp.maximum(i - N_PROMPT_TILES, 0), 0)),
            ],
            scratch_shapes=[
                pltpu.VMEM((2, 2, TM, D_MODEL), _f32),
                pltpu.SemaphoreType.DMA((2, 2)),
            ],
        ),
        out_shape=[jax.ShapeDtypeStruct((N_PROMPT, D_MODEL), _f32),
                   jax.ShapeDtypeStruct((N_SAMPLE, D_MODEL), _f32)],
        compiler_params=pltpu.CompilerParams(dimension_semantics=("arbitrary",)),
        name="combine",
    )(slot_flat, x1, route, g2, b2, y)


def kernel(x_prompt, x_sample, state_conv, state_pool, w_in, conv_w, sgu_ln_g, sgu_ln_b, sgu_w, sgu_b, pool_w, pool_scale, w_out, ln1_g, ln1_b, router_group_w, router_group_b, router_expert_w, router_expert_b, expert_w_gate, expert_w_up, expert_w_down, ln2_g, ln2_b):
    x_p = x_prompt.reshape(N_PROMPT, D_MODEL)
    x_s = x_sample.reshape(N_SAMPLE, D_MODEL)
    xs = jnp.zeros((N_SLOTS, D_MODEL), _f32)
    reps = CHUNK // DEC_SEQ
    row = lambda a: a.reshape(DEPTH, 1, -1)

    w_r = jnp.concatenate([router_group_w, router_expert_w], axis=2)
    w_r = jnp.pad(w_r, ((0, 0), (0, 0), (0, LANES - w_r.shape[2])))
    wr_hi = w_r.astype(_bf16)
    wr = jnp.concatenate([wr_hi, (w_r - wr_hi.astype(_f32)).astype(_bf16)], axis=2)
    b_r = jnp.concatenate([router_group_b, router_expert_b], axis=1)
    b_r = row(jnp.pad(b_r, ((0, 0), (0, LANES - b_r.shape[1]))))
    ws_s = jnp.tile(sgu_w[:, :, :DEC_SEQ, :DEC_SEQ], (1, 1, reps, reps))
    bias_p = jnp.swapaxes(sgu_b, 1, 2)
    bias_s = jnp.tile(bias_p[:, :DEC_SEQ], (1, reps, 1))
    zst = jnp.pad(state_conv, ((0, 0), (0, 0), (CONV_HALO - 2, 0), (0, 0)))
    pst = jnp.pad(state_pool, ((0, 0), (0, 0), (POOL_HALO - POOL_BUF, 0), (0, 0)))
    mixer_params = (w_in.astype(_bf16), w_out.astype(_bf16), conv_w, row(sgu_ln_g), row(sgu_ln_b), sgu_w,
                    ws_s, bias_p, bias_s, pool_w, row(pool_scale), row(ln1_g), row(ln1_b), wr, b_r, zst, pst)
    g2, b2 = row(ln2_g), row(ln2_b)

    ztails, ptails, zss, psamps, v_s = [], [], [], [], []
    for l in range(DEPTH):
        x1, route, ztail, ptail, zs, psamp, vns = _mixer_call(l, x_p, x_s, *mixer_params)
        ids = route[:, 0:2].astype(jnp.int32).T.reshape(ASSIGN_ROWS, LANES)
        slot, tile_expert, n_used = _slots_call(ids)
        slot_flat = slot.reshape(N_ASSIGN)
        xs = _dispatch_call(slot_flat, x1, xs)
        y = _experts_call(l, tile_expert.reshape(LANES), n_used.reshape(LANES)[:1], xs,
                          expert_w_gate, expert_w_up, expert_w_down)
        x_p, x_s = _combine_call(slot_flat, x1, route, g2[l], b2[l], y)
        xs = y
        ztails.append(ztail)
        ptails.append(ptail)
        zss.append(zs)
        psamps.append(psamp)
        v_s.append(vns)

    y_prompt = x_p.reshape(BATCH, SEQ, D_MODEL)
    y_sample = x_s.reshape(DEC_BATCH, DEC_SEQ, D_MODEL)
    new_conv_prompt = jnp.stack(ztails)[:, :BATCH, CONV_HALO - 2:]
    new_pool_prompt = jnp.stack(ptails)[:, :BATCH, POOL_HALO - POOL_BUF:]
    new_conv_sample = jnp.stack(zss)[:, :, DEC_SEQ - 2:]
    new_pool_sample = jnp.concatenate([state_pool[:, :, DEC_SEQ:], jnp.stack(psamps)], axis=2)
    return (y_prompt, y_sample, new_conv_prompt, new_pool_prompt, new_conv_sample, new_pool_sample,
            jnp.stack(v_s))
```

```python
import functools

import jax
import jax.numpy as jnp
from jax import lax
from jax.experimental import pallas as pl
from jax.experimental.pallas import tpu as pltpu

D_MODEL = 2048
BATCH = 4
SEQ = 2048
DEPTH = 4
DEC_BATCH = 128
DEC_SEQ = 8
PAST_LEN = 16384
HEAD_DIM = 128
CONV_WIDTH = 768
POOL_WIDTH = 512
CHUNK_WIDTH = 768
CHUNK = 128
N_CHUNK_HEADS = 6
POOL_WINDOWS = (2, 4, 8, 16)
POOL_GROUP_DIM = 128
POOL_BUF = 15
IN_WIDTH = 4352
N_EXPERT_GROUPS = 4
EXPERTS_PER_GROUP = 4
N_EXPERTS = 16
D_EXPERT = 512
ALPHA = (2 * DEPTH) ** 0.25
LN_EPS = 1e-5

N_PROMPT = BATCH * SEQ
N_SAMPLE = DEC_BATCH * DEC_SEQ
N_TOK = N_PROMPT + N_SAMPLE
N_ASSIGN = 2 * N_TOK

LANES = 128
SUBLANES = 8
CONV_HALO = 8
POOL_HALO = 16

TM = 256
TILES_PER_SEQ = SEQ // TM
N_PROMPT_TILES = N_PROMPT // TM
N_SAMPLE_TILES = N_SAMPLE // TM
N_TILES = N_PROMPT_TILES + N_SAMPLE_TILES
SEQS_PER_TILE = TM // DEC_SEQ

TE = 256
N_SLOTS = N_ASSIGN + N_EXPERTS * TE
N_ETILES = N_SLOTS // TE
ASSIGN_ROWS = N_ASSIGN // LANES

CUT_B, CUT_C, CUT_H, CUT_U, CUT_V, CUT_P = 0, 768, 1536, 2304, 3072, 3840

VMEM_LIMIT = 62 * 1024 * 1024

_f32 = jnp.float32
_bf16 = jnp.bfloat16


def _dot(a, b):
    return jnp.dot(a, b, preferred_element_type=_f32)


def _layer_norm(r, g, b):
    mu = jnp.mean(r, axis=-1, keepdims=True)
    c = r - mu
    var = jnp.mean(c * c, axis=-1, keepdims=True)
    return c * lax.rsqrt(var + LN_EPS) * g + b


class _RowGather:
    def __init__(self, start_row, first_row):
        self._start_row = start_row
        self._done = first_row

    def upto(self, row_end):
        for r in range(self._done, row_end):
            self._start_row(r)
        self._done = max(self._done, row_end)


class _NoGather:
    def upto(self, row_end):
        del row_end


BRANCH_GATHER_ROWS = 192


def _mixer_body(xb, w_in_ref, conv_w_ref, sg_ref, sb_ref, ws_ref, bias_ref, pw_ref, ps_ref,
                zext_ref, pext_ref, ybuf_ref, seq_len, pos, chunk_len, gather):
    n_seq = TM // seq_len

    z = _dot(xb, w_in_ref[:, CUT_C:CUT_H]) * _dot(xb, w_in_ref[:, CUT_H:CUT_U])
    zext_ref[:, CONV_HALO:, :] = z.reshape(n_seq, seq_len, CONV_WIDTH)
    gather.upto(16)
    cw = conv_w_ref[...]
    conv = (cw[0:1, :] * zext_ref[:, CONV_HALO - 2:CONV_HALO - 2 + seq_len, :]
            + cw[1:2, :] * zext_ref[:, CONV_HALO - 1:CONV_HALO - 1 + seq_len, :]
            + cw[2:3, :] * zext_ref[:, CONV_HALO:, :])
    y_conv = _dot(xb, w_in_ref[:, CUT_B:CUT_C]) * conv.reshape(TM, CONV_WIDTH)
    ybuf_ref[:, 0:CONV_WIDTH] = y_conv.astype(_bf16)
    gather.upto(32)

    v = _dot(xb, w_in_ref[:, CUT_V:CUT_P])
    vn = _layer_norm(v, sg_ref[...], sb_ref[...])
    vnb = vn.astype(_bf16)
    gather.upto(48)
    r_i = lax.broadcasted_iota(jnp.int32, (CHUNK, CHUNK), 0)
    c_i = lax.broadcasted_iota(jnp.int32, (CHUNK, CHUNK), 1)
    shift = chunk_len.bit_length() - 1
    mask = (c_i <= r_i) & ((r_i >> shift) == (c_i >> shift))
    u = _dot(xb, w_in_ref[:, CUT_U:CUT_V])
    gather.upto(64)
    n_chunks = TM // CHUNK
    for h in range(N_CHUNK_HEADS):
        m_h = jnp.where(mask, ws_ref[h], 0.0).astype(_bf16)
        b_h = bias_ref[:, h:h + 1]
        cols = slice(h * HEAD_DIM, (h + 1) * HEAD_DIM)
        vn_h = jnp.concatenate([vnb[c * CHUNK:(c + 1) * CHUNK, cols] for c in range(n_chunks)], axis=1)
        mixed = _dot(m_h, vn_h)
        for c in range(n_chunks):
            rows = slice(c * CHUNK, (c + 1) * CHUNK)
            ybuf_ref[rows, CONV_WIDTH + h * HEAD_DIM:CONV_WIDTH + (h + 1) * HEAD_DIM] = (
                u[rows, cols] * (mixed[:, c * HEAD_DIM:(c + 1) * HEAD_DIM] + b_h)).astype(_bf16)
        gather.upto(64 + 8 * (h + 1))

    p = _dot(xb, w_in_ref[:, CUT_P:IN_WIDTH])
    pext_ref[:, POOL_HALO:, :] = p.reshape(n_seq, seq_len, POOL_WIDTH)
    gather.upto(128)
    posf = (pos + 1).astype(_f32)
    for g, w in enumerate(POOL_WINDOWS):
        cols = slice(g * POOL_GROUP_DIM, (g + 1) * POOL_GROUP_DIM)
        s = pext_ref[:, POOL_HALO:, cols]
        for j in range(1, w):
            s = s + pext_ref[:, POOL_HALO - j:POOL_HALO - j + seq_len, cols]
        cnt = jnp.minimum(float(w), posf)
        diff = s.reshape(TM, POOL_GROUP_DIM) / cnt - p[:, cols]
        y_g = _dot(diff.astype(_bf16), pw_ref[g].astype(_bf16)) * ps_ref[:, cols]
        ybuf_ref[:, CONV_WIDTH + CHUNK_WIDTH + g * POOL_GROUP_DIM:
                 CONV_WIDTH + CHUNK_WIDTH + (g + 1) * POOL_GROUP_DIM] = y_g.astype(_bf16)
        gather.upto(128 + 16 * (g + 1))
    assert 128 + 16 * len(POOL_WINDOWS) == BRANCH_GATHER_ROWS
    return z, p, vn


def _route(x1, wr_ref, br_ref):
    x_hi = x1.astype(_bf16)
    x_lo = (x1 - x_hi.astype(_f32)).astype(_bf16)
    hi_parts = _dot(x_hi, wr_ref[...])
    logits = hi_parts[:, :LANES] + hi_parts[:, LANES:] + _dot(x_lo, wr_ref[:, :LANES]) + br_ref[...]
    lane = lax.broadcasted_iota(jnp.int32, logits.shape, 1)
    neg = -jnp.inf
    big = jnp.int32(1 << 20)

    gmask = lane < N_EXPERT_GROUPS
    gl = jnp.where(gmask, logits, neg)
    ge = jnp.exp(gl - jnp.max(gl, axis=1, keepdims=True))
    gp = ge / jnp.sum(ge, axis=1, keepdims=True)
    g_val = jnp.max(gp, axis=1, keepdims=True)
    g_idx = jnp.min(jnp.where(gmask & (gp == g_val), lane, big), axis=1, keepdims=True)

    e_lane = lane - N_EXPERT_GROUPS
    emask = (e_lane >= 0) & (e_lane < N_EXPERTS) & ((e_lane >> (EXPERTS_PER_GROUP.bit_length() - 1)) == g_idx)
    el = jnp.where(emask, logits, neg)
    ee = jnp.exp(el - jnp.max(el, axis=1, keepdims=True))
    ep = ee / jnp.sum(ee, axis=1, keepdims=True)
    v1 = jnp.max(jnp.where(emask, ep, -1.0), axis=1, keepdims=True)
    i1 = jnp.min(jnp.where(emask & (ep == v1), e_lane, big), axis=1, keepdims=True)
    mask2 = emask & (e_lane != i1)
    v2 = jnp.max(jnp.where(mask2, ep, -1.0), axis=1, keepdims=True)
    i2 = jnp.min(jnp.where(mask2 & (ep == v2), e_lane, big), axis=1, keepdims=True)
    tot = v1 + v2
    gate1 = g_val * (v1 / tot)
    gate2 = g_val * (v2 / tot)
    return jnp.where(lane == 0, i1.astype(_f32),
                     jnp.where(lane == 1, i2.astype(_f32),
                               jnp.where(lane == 2, gate1, jnp.where(lane == 3, gate2, 0.0))))


N_MIXER_COMMON_REFS = 29


def _mixer_kernel(fused, *refs):
    if fused:
        slot_ref, x1p_ref, routep_ref, y_hbm, g2_ref, b2_ref = refs[:6]
        refs = refs[6:]
    else:
        xp_ref, xsamp_ref = refs[:2]
        refs = refs[2:]
    (w_in_ref, w_out_ref, conv_w_ref, sg_ref, sb_ref, ws_p_ref, ws_s_ref, bias_p_ref, bias_s_ref,
     pw_ref, ps_ref, g1_ref, b1_ref, wr_ref, br_ref, zst_ref, pst_ref,
     x1_ref, route_ref, ztail_ref, ptail_ref, zs_ref, psamp_ref, vns_ref,
     zext_p, pext_p, zext_s, pext_s, ybuf_ref) = refs[:N_MIXER_COMMON_REFS]
    i = pl.program_id(0)
    last = pl.num_programs(0) - 1

    if fused:
        gbuf, sems = refs[N_MIXER_COMMON_REFS:]
        b = i % 2

        def row_starter(tile, half):
            tok0 = tile * TM

            def start_row(r):
                for k in range(2):
                    pltpu.make_async_copy(y_hbm.at[pl.ds(slot_ref[k * N_TOK + tok0 + r], 1)],
                                          gbuf.at[half, k, pl.ds(r, 1)], sems.at[half, k]).start(priority=k)
            return start_row

        def wait_rows(half):
            for k in range(2):
                pltpu.make_async_copy(y_hbm.at[pl.ds(0, TM)], gbuf.at[half, k], sems.at[half, k]).wait()

        @pl.when(i == 0)
        def _():
            first = row_starter(0, 0)

            def body(r, carry):
                first(r)
                return carry

            lax.fori_loop(0, TM, body, 0, unroll=16)

        wait_rows(b)
        routep = routep_ref[...]
        moe = routep[:, 2:3] * gbuf[b, 0] + routep[:, 3:4] * gbuf[b, 1]
        x = _layer_norm(ALPHA * x1p_ref[...] + moe, g2_ref[...], b2_ref[...])
        next_rows = row_starter(jnp.minimum(i + 1, last), 1 - b)
        new_gather = lambda first_row: _RowGather(next_rows, first_row)
    else:
        x = jnp.where(i < N_PROMPT_TILES, xp_ref[...], xsamp_ref[...])
        new_gather = lambda first_row: _NoGather()

    xb = x.astype(_bf16)
    row = lax.broadcasted_iota(jnp.int32, (TM, 1), 0)

    @pl.when(i < N_PROMPT_TILES)
    def _prompt():
        t0 = (i % TILES_PER_SEQ) * TM

        @pl.when(t0 == 0)
        def _():
            zext_p[:, 0:CONV_HALO, :] = jnp.zeros((1, CONV_HALO, CONV_WIDTH), _f32)
            pext_p[:, 0:POOL_HALO, :] = jnp.zeros((1, POOL_HALO, POOL_WIDTH), _f32)

        z, p, _ = _mixer_body(xb, w_in_ref, conv_w_ref, sg_ref, sb_ref, ws_p_ref, bias_p_ref,
                              pw_ref, ps_ref, zext_p, pext_p, ybuf_ref, TM, t0 + row, CHUNK, new_gather(0))
        z_last = z[TM - CONV_HALO:, :].reshape(1, CONV_HALO, CONV_WIDTH)
        p_last = p[TM - POOL_HALO:, :].reshape(1, POOL_HALO, POOL_WIDTH)
        zext_p[:, 0:CONV_HALO, :] = z_last
        pext_p[:, 0:POOL_HALO, :] = p_last
        ztail_ref[...] = z_last
        ptail_ref[...] = p_last

    @pl.when(i >= N_PROMPT_TILES)
    def _sample():
        zext_s[:, 0:CONV_HALO, :] = zst_ref[...]
        pext_s[:, 0:POOL_HALO, :] = pst_ref[...]
        z, p, vn = _mixer_body(xb, w_in_ref, conv_w_ref, sg_ref, sb_ref, ws_s_ref, bias_s_ref,
                               pw_ref, ps_ref, zext_s, pext_s, ybuf_ref, DEC_SEQ,
                               PAST_LEN + (row & (DEC_SEQ - 1)), DEC_SEQ, new_gather(0))
        zs_ref[...] = z.reshape(SEQS_PER_TILE, DEC_SEQ, CONV_WIDTH)
        psamp_ref[...] = p.reshape(SEQS_PER_TILE, DEC_SEQ, POOL_WIDTH)
        vns_ref[...] = vn.reshape(SEQS_PER_TILE, DEC_SEQ, CHUNK_WIDTH)
        ztail_ref[...] = jnp.zeros(ztail_ref.shape, _f32)
        ptail_ref[...] = jnp.zeros(ptail_ref.shape, _f32)

    tail_gather = new_gather(BRANCH_GATHER_ROWS)
    mix = _dot(ybuf_ref[...], w_out_ref[...])
    tail_gather.upto(BRANCH_GATHER_ROWS + (TM - BRANCH_GATHER_ROWS) // 2)
    x1 = _layer_norm(ALPHA * x + mix, g1_ref[...], b1_ref[...])
    x1_ref[...] = x1
    tail_gather.upto(TM)
    route_ref[...] = _route(x1, wr_ref, br_ref)

    if fused:
        @pl.when(i == last)
        def _():
            wait_rows(1 - b)


def _layer_spec(layer, shape):
    nd = len(shape)
    return pl.BlockSpec((None,) + shape, lambda i, *_: (layer,) + (0,) * nd, pipeline_mode=pl.Buffered(1))


def _mixer_call(layer, x_src, params):
    fused = len(x_src) == 6
    samp = lambda i, *_: (jnp.maximum(i - N_PROMPT_TILES, 0), 0, 0)
    samp_l = lambda i, *_: (layer, jnp.maximum(i - N_PROMPT_TILES, 0), 0, 0)
    tail = lambda i, *_: (i // TILES_PER_SEQ, 0, 0)
    tok = lambda i, *_: (i, 0)
    if fused:
        x_specs = [
            pl.BlockSpec((TM, D_MODEL), tok),
            pl.BlockSpec((TM, LANES), tok),
            pl.BlockSpec(memory_space=pl.ANY),
            _layer_spec(layer - 1, (1, D_MODEL)),
            _layer_spec(layer - 1, (1, D_MODEL)),
        ]
    else:
        x_specs = [
            pl.BlockSpec((TM, D_MODEL), lambda i, *_: (jnp.minimum(i, N_PROMPT_TILES - 1), 0)),
            pl.BlockSpec((TM, D_MODEL), lambda i, *_: (jnp.maximum(i - N_PROMPT_TILES, 0), 0)),
        ]
    in_specs = x_specs + [
        _layer_spec(layer, (D_MODEL, IN_WIDTH)),
        _layer_spec(layer, (D_MODEL, D_MODEL)),
        _layer_spec(layer, (3, CONV_WIDTH)),
        _layer_spec(layer, (1, CHUNK_WIDTH)),
        _layer_spec(layer, (1, CHUNK_WIDTH)),
        _layer_spec(layer, (N_CHUNK_HEADS, CHUNK, CHUNK)),
        _layer_spec(layer, (N_CHUNK_HEADS, CHUNK, CHUNK)),
        _layer_spec(layer, (CHUNK, N_CHUNK_HEADS)),
        _layer_spec(layer, (CHUNK, N_CHUNK_HEADS)),
        _layer_spec(layer, (4, POOL_GROUP_DIM, POOL_GROUP_DIM)),
        _layer_spec(layer, (1, POOL_WIDTH)),
        _layer_spec(layer, (1, D_MODEL)),
        _layer_spec(layer, (1, D_MODEL)),
        _layer_spec(layer, (D_MODEL, 2 * LANES)),
        _layer_spec(layer, (1, LANES)),
        pl.BlockSpec((None, SEQS_PER_TILE, CONV_HALO, CONV_WIDTH), samp_l),
        pl.BlockSpec((None, SEQS_PER_TILE, POOL_HALO, POOL_WIDTH), samp_l),
    ]
    out_specs = [
        pl.BlockSpec((TM, D_MODEL), tok),
        pl.BlockSpec((TM, LANES), tok),
        pl.BlockSpec((1, CONV_HALO, CONV_WIDTH), tail),
        pl.BlockSpec((1, POOL_HALO, POOL_WIDTH), tail),
        pl.BlockSpec((SEQS_PER_TILE, DEC_SEQ, CONV_WIDTH), samp),
        pl.BlockSpec((SEQS_PER_TILE, DEC_SEQ, POOL_WIDTH), samp),
        pl.BlockSpec((SEQS_PER_TILE, DEC_SEQ, CHUNK_WIDTH), samp),
    ]
    out_shape = [
        jax.ShapeDtypeStruct((N_TOK, D_MODEL), _f32),
        jax.ShapeDtypeStruct((N_TOK, LANES), _f32),
        jax.ShapeDtypeStruct((BATCH + 1, CONV_HALO, CONV_WIDTH), _f32),
        jax.ShapeDtypeStruct((BATCH + 1, POOL_HALO, POOL_WIDTH), _f32),
        jax.ShapeDtypeStruct((DEC_BATCH, DEC_SEQ, CONV_WIDTH), _f32),
        jax.ShapeDtypeStruct((DEC_BATCH, DEC_SEQ, POOL_WIDTH), _f32),
        jax.ShapeDtypeStruct((DEC_BATCH, DEC_SEQ, CHUNK_WIDTH), _f32),
    ]
    scratch = [
        pltpu.VMEM((1, CONV_HALO + TM, CONV_WIDTH), _f32),
        pltpu.VMEM((1, POOL_HALO + TM, POOL_WIDTH), _f32),
        pltpu.VMEM((SEQS_PER_TILE, CONV_HALO + DEC_SEQ, CONV_WIDTH), _f32),
        pltpu.VMEM((SEQS_PER_TILE, POOL_HALO + DEC_SEQ, POOL_WIDTH), _f32),
        pltpu.VMEM((TM, D_MODEL), _bf16),
    ]
    assert len(in_specs) - len(x_specs) + len(out_specs) + len(scratch) == N_MIXER_COMMON_REFS
    if fused:
        scratch += [pltpu.VMEM((2, 2, TM, D_MODEL), _f32), pltpu.SemaphoreType.DMA((2, 2))]
    return pl.pallas_call(
        functools.partial(_mixer_kernel, fused),
        grid_spec=pltpu.PrefetchScalarGridSpec(
            num_scalar_prefetch=1 if fused else 0,
            grid=(N_TILES,),
            in_specs=in_specs,
            out_specs=out_specs,
            scratch_shapes=scratch,
        ),
        out_shape=out_shape,
        compiler_params=pltpu.CompilerParams(
            dimension_semantics=("arbitrary",), vmem_limit_bytes=VMEM_LIMIT),
        name="mixer",
    )(*x_src, *params)


def _slots_kernel(ids_ref, slot_ref, tile_expert_ref, n_used_ref):
    ids = ids_ref[...]
    r_i = lax.broadcasted_iota(jnp.int32, (LANES, LANES), 0)
    c_i = lax.broadcasted_iota(jnp.int32, (LANES, LANES), 1)
    upper = (r_i < c_i).astype(_bf16)
    rr = lax.broadcasted_iota(jnp.int32, (ASSIGN_ROWS, ASSIGN_ROWS), 0)
    rc = lax.broadcasted_iota(jnp.int32, (ASSIGN_ROWS, ASSIGN_ROWS), 1)
    lower = (rc < rr).astype(_bf16)
    tile_row0 = (lax.broadcasted_iota(jnp.int32, (1, LANES), 1) * TE).astype(_f32)

    slot = jnp.zeros((ASSIGN_ROWS, LANES), _f32)
    tile_expert = jnp.zeros((1, LANES), _f32)
    off = jnp.zeros((1, 1), _f32)
    for e in range(N_EXPERTS):
        m = (ids == e).astype(_f32)
        within = _dot(m.astype(_bf16), upper)
        rowsum = jnp.sum(m, axis=1, keepdims=True)
        rowpre = _dot(lower, jnp.broadcast_to(rowsum, (ASSIGN_ROWS, LANES)).astype(_bf16))
        cnt = jnp.sum(rowsum, axis=0, keepdims=True)
        padded = jnp.floor((cnt + (TE - 1)) * (1.0 / TE)) * TE
        slot = slot + m * (off + rowpre + within)
        in_seg = (tile_row0 >= off) & (tile_row0 < off + padded)
        tile_expert = tile_expert + jnp.where(in_seg, float(e), 0.0)
        off = off + padded
    tile_expert = jnp.where(tile_row0 >= off, float(N_EXPERTS - 1), tile_expert)
    slot_ref[...] = slot.astype(jnp.int32)
    tile_expert_ref[...] = tile_expert.astype(jnp.int32)
    n_used_ref[...] = jnp.broadcast_to(off * (1.0 / TE), (1, LANES)).astype(jnp.int32)


def _slots_call(ids):
    return pl.pallas_call(
        _slots_kernel,
        out_shape=[
            jax.ShapeDtypeStruct((ASSIGN_ROWS, LANES), jnp.int32),
            jax.ShapeDtypeStruct((1, LANES), jnp.int32),
            jax.ShapeDtypeStruct((1, LANES), jnp.int32),
        ],
        name="slots",
    )(ids)


def _dispatch_kernel(slot_ref, x1_ref, xs_in_hbm, xs_hbm, stage, sems):
    del xs_in_hbm
    i = pl.program_id(0)
    tok0 = i * TM
    b = i % 2

    def wait_all(slot):
        for k in range(2):
            pltpu.make_async_copy(stage.at[slot], xs_hbm.at[pl.ds(0, TM)], sems.at[slot, k]).wait()

    stage[b] = x1_ref[...]

    def start(r, carry):
        for k in range(2):
            pltpu.make_async_copy(stage.at[b, pl.ds(r, 1)],
                                  xs_hbm.at[pl.ds(slot_ref[k * N_TOK + tok0 + r], 1)],
                                  sems.at[b, k]).start(priority=k)
        return carry

    lax.fori_loop(0, TM, start, 0, unroll=16)

    @pl.when(i > 0)
    def _():
        wait_all(1 - b)

    @pl.when(i == pl.num_programs(0) - 1)
    def _():
        wait_all(b)


def _dispatch_call(slot_flat, x1, xs):
    return pl.pallas_call(
        _dispatch_kernel,
        grid_spec=pltpu.PrefetchScalarGridSpec(
            num_scalar_prefetch=1,
            grid=(N_TILES,),
            in_specs=[pl.BlockSpec((TM, D_MODEL), lambda i, s: (i, 0)), pl.BlockSpec(memory_space=pl.ANY)],
            out_specs=pl.BlockSpec(memory_space=pl.ANY),
            scratch_shapes=[pltpu.VMEM((2, TM, D_MODEL), _f32), pltpu.SemaphoreType.DMA((2, 2))],
        ),
        out_shape=jax.ShapeDtypeStruct((N_SLOTS, D_MODEL), _f32),
        input_output_aliases={2: 0},
        compiler_params=pltpu.CompilerParams(dimension_semantics=("arbitrary",)),
        name="dispatch",
    )(slot_flat, x1, xs)


def _experts_kernel(te_ref, nu_ref, xs_ref, wg_ref, wu_ref, wd_ref, y_ref, wg_b, wu_b, wd_b):
    j = pl.program_id(0)

    @pl.when(j < nu_ref[0])
    def _():
        prev = te_ref[jnp.maximum(j - 1, 0)]

        @pl.when((j == 0) | (te_ref[j] != prev))
        def _():
            wg_b[...] = wg_ref[0, 0].astype(_bf16)
            wu_b[...] = wu_ref[0, 0].astype(_bf16)
            wd_b[...] = wd_ref[0, 0].astype(_bf16)

        xb = xs_ref[...].astype(_bf16)
        hg = _dot(xb, wg_b[...])
        hu = _dot(xb, wu_b[...])
        a = hg / (1.0 + jnp.exp(-hg)) * hu
        y_ref[...] = _dot(a.astype(_bf16), wd_b[...])


def _experts_call(layer, tile_expert, n_used, xs, wg, wu, wd):
    row = lambda j, te, nu: (jnp.minimum(j, nu[0] - 1), 0)
    wsel = lambda j, te, nu: (layer, te[j], 0, 0)
    return pl.pallas_call(
        _experts_kernel,
        grid_spec=pltpu.PrefetchScalarGridSpec(
            num_scalar_prefetch=2,
            grid=(N_ETILES,),
            in_specs=[
                pl.BlockSpec((TE, D_MODEL), row),
                pl.BlockSpec((1, 1, D_MODEL, D_EXPERT), wsel),
                pl.BlockSpec((1, 1, D_MODEL, D_EXPERT), wsel),
                pl.BlockSpec((1, 1, D_EXPERT, D_MODEL), wsel),
            ],
            out_specs=pl.BlockSpec((TE, D_MODEL), row),
            scratch_shapes=[
                pltpu.VMEM((D_MODEL, D_EXPERT), _bf16),
                pltpu.VMEM((D_MODEL, D_EXPERT), _bf16),
                pltpu.VMEM((D_EXPERT, D_MODEL), _bf16),
            ],
        ),
        out_shape=jax.ShapeDtypeStruct((N_SLOTS, D_MODEL), _f32),
        input_output_aliases={2: 0},
        compiler_params=pltpu.CompilerParams(
            dimension_semantics=("arbitrary",), vmem_limit_bytes=VMEM_LIMIT),
        name="experts",
    )(tile_expert, n_used, xs, wg, wu, wd)


def _combine_kernel(slot_ref, x1_ref, route_ref, g2_ref, b2_ref, y_hbm, out_p_ref, out_s_ref,
                    ybuf, sems):
    i = pl.program_id(0)
    b = i % 2

    def gather(tile, slot):
        tok0 = tile * TM

        def start(r, carry):
            for k in range(2):
                pltpu.make_async_copy(y_hbm.at[pl.ds(slot_ref[k * N_TOK + tok0 + r], 1)],
                                      ybuf.at[slot, k, pl.ds(r, 1)], sems.at[slot, k]).start(priority=k)
            return carry

        lax.fori_loop(0, TM, start, 0, unroll=16)

    @pl.when(i == 0)
    def _():
        gather(0, 0)

    @pl.when(i + 1 < pl.num_programs(0))
    def _():
        gather(i + 1, 1 - b)

    for k in range(2):
        pltpu.make_async_copy(y_hbm.at[pl.ds(0, TM)], ybuf.at[b, k], sems.at[b, k]).wait()

    route = route_ref[...]
    moe = route[:, 2:3] * ybuf[b, 0] + route[:, 3:4] * ybuf[b, 1]
    out = _layer_norm(ALPHA * x1_ref[...] + moe, g2_ref[...], b2_ref[...])

    @pl.when(i < N_PROMPT_TILES)
    def _():
        out_p_ref[...] = out

    @pl.when(i >= N_PROMPT_TILES)
    def _():
        out_s_ref[...] = out


def _combine_call(slot_flat, x1, route, g2, b2, y):
    tok = lambda i, s: (i, 0)
    const = lambda i, s: (0, 0)
    return pl.pallas_call(
        _combine_kernel,
        grid_spec=pltpu.PrefetchScalarGridSpec(
            num_scalar_prefetch=1,
            grid=(N_TILES,),
            in_specs=[
                pl.BlockSpec((TM, D_MODEL), tok),
                pl.BlockSpec((TM, LANES), tok),
                pl.BlockSpec((1, D_MODEL), const),
                pl.BlockSpec((1, D_MODEL), const),
                pl.BlockSpec(memory_space=pl.ANY),
            ],
            out_specs=[
                pl.BlockSpec((TM, D_MODEL), lambda i, s: (jnp.minimum(i, N_PROMPT_TILES - 1), 0)),
                pl.BlockSpec((TM, D_MODEL), lambda i, s: (jnp.maximum(i - N_PROMPT_TILES, 0), 0)),
            ],
            scratch_shapes=[
                pltpu.VMEM((2, 2, TM, D_MODEL), _f32),
                pltpu.SemaphoreType.DMA((2, 2)),
            ],
        ),
        out_shape=[jax.ShapeDtypeStruct((N_PROMPT, D_MODEL), _f32),
                   jax.ShapeDtypeStruct((N_SAMPLE, D_MODEL), _f32)],
        compiler_params=pltpu.CompilerParams(dimension_semantics=("arbitrary",)),
        name="combine",
    )(slot_flat, x1, route, g2, b2, y)


def kernel(x_prompt, x_sample, state_conv, state_pool, w_in, conv_w, sgu_ln_g, sgu_ln_b, sgu_w, sgu_b, pool_w, pool_scale, w_out, ln1_g, ln1_b, router_group_w, router_group_b, router_expert_w, router_expert_b, expert_w_gate, expert_w_up, expert_w_down, ln2_g, ln2_b):
    xs = jnp.zeros((N_SLOTS, D_MODEL), _f32)
    reps = CHUNK // DEC_SEQ
    row = lambda a: a.reshape(DEPTH, 1, -1)

    w_r = jnp.concatenate([router_group_w, router_expert_w], axis=2)
    w_r = jnp.pad(w_r, ((0, 0), (0, 0), (0, LANES - w_r.shape[2])))
    wr_hi = w_r.astype(_bf16)
    wr = jnp.concatenate([wr_hi, (w_r - wr_hi.astype(_f32)).astype(_bf16)], axis=2)
    b_r = jnp.concatenate([router_group_b, router_expert_b], axis=1)
    b_r = row(jnp.pad(b_r, ((0, 0), (0, LANES - b_r.shape[1]))))
    ws_s = jnp.tile(sgu_w[:, :, :DEC_SEQ, :DEC_SEQ], (1, 1, reps, reps))
    bias_p = jnp.swapaxes(sgu_b, 1, 2)
    bias_s = jnp.tile(bias_p[:, :DEC_SEQ], (1, reps, 1))
    zst = jnp.pad(state_conv, ((0, 0), (0, 0), (CONV_HALO - 2, 0), (0, 0)))
    pst = jnp.pad(state_pool, ((0, 0), (0, 0), (POOL_HALO - POOL_BUF, 0), (0, 0)))
    mixer_params = (w_in.astype(_bf16), w_out.astype(_bf16), conv_w, row(sgu_ln_g), row(sgu_ln_b), sgu_w,
                    ws_s, bias_p, bias_s, pool_w, row(pool_scale), row(ln1_g), row(ln1_b), wr, b_r, zst, pst)
    g2, b2 = row(ln2_g), row(ln2_b)

    ztails, ptails, zss, psamps, v_s = [], [], [], [], []
    x_src = (x_prompt.reshape(N_PROMPT, D_MODEL), x_sample.reshape(N_SAMPLE, D_MODEL))
    for l in range(DEPTH):
        x1, route, ztail, ptail, zs, psamp, vns = _mixer_call(l, x_src, mixer_params)
        ids = route[:, 0:2].astype(jnp.int32).T.reshape(ASSIGN_ROWS, LANES)
        slot, tile_expert, n_used = _slots_call(ids)
        slot_flat = slot.reshape(N_ASSIGN)
        xs = _dispatch_call(slot_flat, x1, xs)
        y = _experts_call(l, tile_expert.reshape(LANES), n_used.reshape(LANES)[:1], xs,
                          expert_w_gate, expert_w_up, expert_w_down)
        x_src = (slot_flat, x1, route, y, g2, b2)
        xs = y
        ztails.append(ztail)
        ptails.append(ptail)
        zss.append(zs)
        psamps.append(psamp)
        v_s.append(vns)

    x_p, x_s = _combine_call(slot_flat, x1, route, g2[DEPTH - 1], b2[DEPTH - 1], y)
    y_prompt = x_p.reshape(BATCH, SEQ, D_MODEL)
    y_sample = x_s.reshape(DEC_BATCH, DEC_SEQ, D_MODEL)
    new_conv_prompt = jnp.stack(ztails)[:, :BATCH, CONV_HALO - 2:]
    new_pool_prompt = jnp.stack(ptails)[:, :BATCH, POOL_HALO - POOL_BUF:]
    new_conv_sample = jnp.stack(zss)[:, :, DEC_SEQ - 2:]
    new_pool_sample = jnp.concatenate([state_pool[:, :, DEC_SEQ:], jnp.stack(psamps)], axis=2)
    return (y_prompt, y_sample, new_conv_prompt, new_pool_prompt, new_conv_sample, new_pool_sample,
            jnp.stack(v_s))
```

```python
import functools

import jax
import jax.numpy as jnp
from jax import lax
from jax.experimental import pallas as pl
from jax.experimental.pallas import tpu as pltpu

D_MODEL = 2048
BATCH = 4
SEQ = 2048
DEPTH = 4
DEC_BATCH = 128
DEC_SEQ = 8
PAST_LEN = 16384
HEAD_DIM = 128
CONV_WIDTH = 768
POOL_WIDTH = 512
CHUNK_WIDTH = 768
CHUNK = 128
N_CHUNK_HEADS = 6
POOL_WINDOWS = (2, 4, 8, 16)
POOL_GROUP_DIM = 128
POOL_BUF = 15
IN_WIDTH = 4352
N_EXPERT_GROUPS = 4
EXPERTS_PER_GROUP = 4
N_EXPERTS = 16
D_EXPERT = 512
ALPHA = (2 * DEPTH) ** 0.25
LN_EPS = 1e-5

N_PROMPT = BATCH * SEQ
N_SAMPLE = DEC_BATCH * DEC_SEQ
N_TOK = N_PROMPT + N_SAMPLE
N_ASSIGN = 2 * N_TOK

LANES = 128
SUBLANES = 8
CONV_HALO = 8
POOL_HALO = 16

TM = 256
TILES_PER_SEQ = SEQ // TM
N_PROMPT_TILES = N_PROMPT // TM
N_SAMPLE_TILES = N_SAMPLE // TM
N_TILES = N_PROMPT_TILES + N_SAMPLE_TILES
SEQS_PER_TILE = TM // DEC_SEQ

TE = 256
N_SLOTS = N_ASSIGN + N_EXPERTS * TE
N_ETILES = N_SLOTS // TE
ASSIGN_ROWS = N_ASSIGN // LANES

CUT_B, CUT_C, CUT_H, CUT_U, CUT_V, CUT_P = 0, 768, 1536, 2304, 3072, 3840

VMEM_LIMIT = 62 * 1024 * 1024

_f32 = jnp.float32
_bf16 = jnp.bfloat16


def _dot(a, b):
    return jnp.dot(a, b, preferred_element_type=_f32)


def _layer_norm(r, g, b):
    mu = jnp.mean(r, axis=-1, keepdims=True)
    c = r - mu
    var = jnp.mean(c * c, axis=-1, keepdims=True)
    return c * lax.rsqrt(var + LN_EPS) * g + b


class _RowGather:
    def __init__(self, start_row, first_row):
        self._start_row = start_row
        self._done = first_row

    def upto(self, row_end):
        for r in range(self._done, row_end):
            self._start_row(r)
        self._done = max(self._done, row_end)


class _NoGather:
    def upto(self, row_end):
        del row_end


BRANCH_GATHER_ROWS = 192


def _mixer_body(xb, w_in_ref, conv_w_ref, sg_ref, sb_ref, ws_ref, bias_ref, pw_ref, ps_ref,
                zext_ref, pext_ref, ybuf_ref, seq_len, pos, chunk_len, gather):
    n_seq = TM // seq_len

    z = _dot(xb, w_in_ref[:, CUT_C:CUT_H]) * _dot(xb, w_in_ref[:, CUT_H:CUT_U])
    zext_ref[:, CONV_HALO:, :] = z.reshape(n_seq, seq_len, CONV_WIDTH)
    gather.upto(16)
    cw = conv_w_ref[...]
    conv = (cw[0:1, :] * zext_ref[:, CONV_HALO - 2:CONV_HALO - 2 + seq_len, :]
            + cw[1:2, :] * zext_ref[:, CONV_HALO - 1:CONV_HALO - 1 + seq_len, :]
            + cw[2:3, :] * zext_ref[:, CONV_HALO:, :])
    y_conv = _dot(xb, w_in_ref[:, CUT_B:CUT_C]) * conv.reshape(TM, CONV_WIDTH)
    ybuf_ref[:, 0:CONV_WIDTH] = y_conv.astype(_bf16)
    gather.upto(32)

    v = _dot(xb, w_in_ref[:, CUT_V:CUT_P])
    vn = _layer_norm(v, sg_ref[...], sb_ref[...])
    vnb = vn.astype(_bf16)
    gather.upto(48)
    r_i = lax.broadcasted_iota(jnp.int32, (CHUNK, CHUNK), 0)
    c_i = lax.broadcasted_iota(jnp.int32, (CHUNK, CHUNK), 1)
    shift = chunk_len.bit_length() - 1
    mask = (c_i <= r_i) & ((r_i >> shift) == (c_i >> shift))
    u = _dot(xb, w_in_ref[:, CUT_U:CUT_V])
    gather.upto(64)
    n_chunks = TM // CHUNK
    for h in range(N_CHUNK_HEADS):
        m_h = jnp.where(mask, ws_ref[h], 0.0).astype(_bf16)
        b_h = bias_ref[:, h:h + 1]
        cols = slice(h * HEAD_DIM, (h + 1) * HEAD_DIM)
        vn_h = jnp.concatenate([vnb[c * CHUNK:(c + 1) * CHUNK, cols] for c in range(n_chunks)], axis=1)
        mixed = _dot(m_h, vn_h)
        for c in range(n_chunks):
            rows = slice(c * CHUNK, (c + 1) * CHUNK)
            ybuf_ref[rows, CONV_WIDTH + h * HEAD_DIM:CONV_WIDTH + (h + 1) * HEAD_DIM] = (
                u[rows, cols] * (mixed[:, c * HEAD_DIM:(c + 1) * HEAD_DIM] + b_h)).astype(_bf16)
        gather.upto(64 + 8 * (h + 1))

    p = _dot(xb, w_in_ref[:, CUT_P:IN_WIDTH])
    pext_ref[:, POOL_HALO:, :] = p.reshape(n_seq, seq_len, POOL_WIDTH)
    gather.upto(128)
    posf = (pos + 1).astype(_f32)
    for g, w in enumerate(POOL_WINDOWS):
        cols = slice(g * POOL_GROUP_DIM, (g + 1) * POOL_GROUP_DIM)
        s = pext_ref[:, POOL_HALO:, cols]
        for j in range(1, w):
            s = s + pext_ref[:, POOL_HALO - j:POOL_HALO - j + seq_len, cols]
        cnt = jnp.minimum(float(w), posf)
        diff = s.reshape(TM, POOL_GROUP_DIM) / cnt - p[:, cols]
        y_g = _dot(diff.astype(_bf16), pw_ref[g].astype(_bf16)) * ps_ref[:, cols]
        ybuf_ref[:, CONV_WIDTH + CHUNK_WIDTH + g * POOL_GROUP_DIM:
                 CONV_WIDTH + CHUNK_WIDTH + (g + 1) * POOL_GROUP_DIM] = y_g.astype(_bf16)
        gather.upto(128 + 16 * (g + 1))
    assert 128 + 16 * len(POOL_WINDOWS) == BRANCH_GATHER_ROWS
    return z, p, vn


def _route(x1, wr_ref, br_ref):
    x_hi = x1.astype(_bf16)
    x_lo = (x1 - x_hi.astype(_f32)).astype(_bf16)
    hi_parts = _dot(x_hi, wr_ref[...])
    logits = hi_parts[:, :LANES] + hi_parts[:, LANES:] + _dot(x_lo, wr_ref[:, :LANES]) + br_ref[...]
    lane = lax.broadcasted_iota(jnp.int32, logits.shape, 1)
    neg = -jnp.inf
    big = jnp.int32(1 << 20)

    gmask = lane < N_EXPERT_GROUPS
    gl = jnp.where(gmask, logits, neg)
    ge = jnp.exp(gl - jnp.max(gl, axis=1, keepdims=True))
    gp = ge / jnp.sum(ge, axis=1, keepdims=True)
    g_val = jnp.max(gp, axis=1, keepdims=True)
    g_idx = jnp.min(jnp.where(gmask & (gp == g_val), lane, big), axis=1, keepdims=True)

    e_lane = lane - N_EXPERT_GROUPS
    emask = (e_lane >= 0) & (e_lane < N_EXPERTS) & ((e_lane >> (EXPERTS_PER_GROUP.bit_length() - 1)) == g_idx)
    el = jnp.where(emask, logits, neg)
    ee = jnp.exp(el - jnp.max(el, axis=1, keepdims=True))
    ep = ee / jnp.sum(ee, axis=1, keepdims=True)
    v1 = jnp.max(jnp.where(emask, ep, -1.0), axis=1, keepdims=True)
    i1 = jnp.min(jnp.where(emask & (ep == v1), e_lane, big), axis=1, keepdims=True)
    mask2 = emask & (e_lane != i1)
    v2 = jnp.max(jnp.where(mask2, ep, -1.0), axis=1, keepdims=True)
    i2 = jnp.min(jnp.where(mask2 & (ep == v2), e_lane, big), axis=1, keepdims=True)
    tot = v1 + v2
    gate1 = g_val * (v1 / tot)
    gate2 = g_val * (v2 / tot)
    return jnp.where(lane == 0, i1.astype(_f32),
                     jnp.where(lane == 1, i2.astype(_f32),
                               jnp.where(lane == 2, gate1, jnp.where(lane == 3, gate2, 0.0))))


N_MIXER_COMMON_REFS = 29


def _mixer_kernel(fused, *refs):
    if fused:
        slot_ref, x1p_ref, routep_ref, y_hbm, g2_ref, b2_ref = refs[:6]
        refs = refs[6:]
    else:
        xp_ref, xsamp_ref = refs[:2]
        refs = refs[2:]
    (w_in_ref, w_out_ref, conv_w_ref, sg_ref, sb_ref, ws_p_ref, ws_s_ref, bias_p_ref, bias_s_ref,
     pw_ref, ps_ref, g1_ref, b1_ref, wr_ref, br_ref, zst_ref, pst_ref,
     x1_ref, route_ref, ztail_ref, ptail_ref, zs_ref, psamp_ref, vns_ref,
     zext_p, pext_p, zext_s, pext_s, ybuf_ref) = refs[:N_MIXER_COMMON_REFS]
    i = pl.program_id(0)
    last = pl.num_programs(0) - 1

    if fused:
        gbuf, sems = refs[N_MIXER_COMMON_REFS:]
        b = i % 2

        def row_starter(tile, half):
            tok0 = tile * TM

            def start_row(r):
                for k in range(2):
                    pltpu.make_async_copy(y_hbm.at[pl.ds(slot_ref[k * N_TOK + tok0 + r], 1)],
                                          gbuf.at[half, k, pl.ds(r, 1)], sems.at[half, k]).start(priority=k)
            return start_row

        def wait_rows(half):
            for k in range(2):
                pltpu.make_async_copy(y_hbm.at[pl.ds(0, TM)], gbuf.at[half, k], sems.at[half, k]).wait()

        @pl.when(i == 0)
        def _():
            first = row_starter(0, 0)

            def body(r, carry):
                first(r)
                return carry

            lax.fori_loop(0, TM, body, 0, unroll=16)

        wait_rows(b)
        routep = routep_ref[...]
        moe = routep[:, 2:3] * gbuf[b, 0] + routep[:, 3:4] * gbuf[b, 1]
        x = _layer_norm(ALPHA * x1p_ref[...] + moe, g2_ref[...], b2_ref[...])
        next_rows = row_starter(jnp.minimum(i + 1, last), 1 - b)
        new_gather = lambda first_row: _RowGather(next_rows, first_row)
    else:
        x = jnp.where(i < N_PROMPT_TILES, xp_ref[...], xsamp_ref[...])
        new_gather = lambda first_row: _NoGather()

    xb = x.astype(_bf16)
    row = lax.broadcasted_iota(jnp.int32, (TM, 1), 0)

    @pl.when(i < N_PROMPT_TILES)
    def _prompt():
        t0 = (i % TILES_PER_SEQ) * TM

        @pl.when(t0 == 0)
        def _():
            zext_p[:, 0:CONV_HALO, :] = jnp.zeros((1, CONV_HALO, CONV_WIDTH), _f32)
            pext_p[:, 0:POOL_HALO, :] = jnp.zeros((1, POOL_HALO, POOL_WIDTH), _f32)

        z, p, _ = _mixer_body(xb, w_in_ref, conv_w_ref, sg_ref, sb_ref, ws_p_ref, bias_p_ref,
                              pw_ref, ps_ref, zext_p, pext_p, ybuf_ref, TM, t0 + row, CHUNK, new_gather(0))
        z_last = z[TM - CONV_HALO:, :].reshape(1, CONV_HALO, CONV_WIDTH)
        p_last = p[TM - POOL_HALO:, :].reshape(1, POOL_HALO, POOL_WIDTH)
        zext_p[:, 0:CONV_HALO, :] = z_last
        pext_p[:, 0:POOL_HALO, :] = p_last
        ztail_ref[...] = z_last
        ptail_ref[...] = p_last

    @pl.when(i >= N_PROMPT_TILES)
    def _sample():
        zext_s[:, 0:CONV_HALO, :] = zst_ref[...]
        pext_s[:, 0:POOL_HALO, :] = pst_ref[...]
        z, p, vn = _mixer_body(xb, w_in_ref, conv_w_ref, sg_ref, sb_ref, ws_s_ref, bias_s_ref,
                               pw_ref, ps_ref, zext_s, pext_s, ybuf_ref, DEC_SEQ,
                               PAST_LEN + (row & (DEC_SEQ - 1)), DEC_SEQ, new_gather(0))
        zs_ref[...] = z.reshape(SEQS_PER_TILE, DEC_SEQ, CONV_WIDTH)
        psamp_ref[...] = p.reshape(SEQS_PER_TILE, DEC_SEQ, POOL_WIDTH)
        vns_ref[...] = vn.reshape(SEQS_PER_TILE, DEC_SEQ, CHUNK_WIDTH)
        ztail_ref[...] = jnp.zeros(ztail_ref.shape, _f32)
        ptail_ref[...] = jnp.zeros(ptail_ref.shape, _f32)

    tail_gather = new_gather(BRANCH_GATHER_ROWS)
    mix = _dot(ybuf_ref[...], w_out_ref[...])
    tail_gather.upto(BRANCH_GATHER_ROWS + (TM - BRANCH_GATHER_ROWS) // 2)
    x1 = _layer_norm(ALPHA * x + mix, g1_ref[...], b1_ref[...])
    x1_ref[...] = x1
    tail_gather.upto(TM)
    route_ref[...] = _route(x1, wr_ref, br_ref)

    if fused:
        @pl.when(i == last)
        def _():
            wait_rows(1 - b)


def _layer_spec(layer, shape):
    nd = len(shape)
    return pl.BlockSpec((None,) + shape, lambda i, *_: (layer,) + (0,) * nd, pipeline_mode=pl.Buffered(1))


def _mixer_call(layer, x_src, params):
    fused = len(x_src) == 6
    samp = lambda i, *_: (jnp.maximum(i - N_PROMPT_TILES, 0), 0, 0)
    samp_l = lambda i, *_: (layer, jnp.maximum(i - N_PROMPT_TILES, 0), 0, 0)
    tail = lambda i, *_: (i // TILES_PER_SEQ, 0, 0)
    tok = lambda i, *_: (i, 0)
    if fused:
        x_specs = [
            pl.BlockSpec((TM, D_MODEL), tok),
            pl.BlockSpec((TM, LANES), tok),
            pl.BlockSpec(memory_space=pl.ANY),
            _layer_spec(layer - 1, (1, D_MODEL)),
            _layer_spec(layer - 1, (1, D_MODEL)),
        ]
    else:
        x_specs = [
            pl.BlockSpec((TM, D_MODEL), lambda i, *_: (jnp.minimum(i, N_PROMPT_TILES - 1), 0)),
            pl.BlockSpec((TM, D_MODEL), lambda i, *_: (jnp.maximum(i - N_PROMPT_TILES, 0), 0)),
        ]
    in_specs = x_specs + [
        _layer_spec(layer, (D_MODEL, IN_WIDTH)),
        _layer_spec(layer, (D_MODEL, D_MODEL)),
        _layer_spec(layer, (3, CONV_WIDTH)),
        _layer_spec(layer, (1, CHUNK_WIDTH)),
        _layer_spec(layer, (1, CHUNK_WIDTH)),
        _layer_spec(layer, (N_CHUNK_HEADS, CHUNK, CHUNK)),
        _layer_spec(layer, (N_CHUNK_HEADS, CHUNK, CHUNK)),
        _layer_spec(layer, (CHUNK, N_CHUNK_HEADS)),
        _layer_spec(layer, (CHUNK, N_CHUNK_HEADS)),
        _layer_spec(layer, (4, POOL_GROUP_DIM, POOL_GROUP_DIM)),
        _layer_spec(layer, (1, POOL_WIDTH)),
        _layer_spec(layer, (1, D_MODEL)),
        _layer_spec(layer, (1, D_MODEL)),
        _layer_spec(layer, (D_MODEL, 2 * LANES)),
        _layer_spec(layer, (1, LANES)),
        pl.BlockSpec((None, SEQS_PER_TILE, CONV_HALO, CONV_WIDTH), samp_l),
        pl.BlockSpec((None, SEQS_PER_TILE, POOL_HALO, POOL_WIDTH), samp_l),
    ]
    out_specs = [
        pl.BlockSpec((TM, D_MODEL), tok),
        pl.BlockSpec((TM, LANES), tok),
        pl.BlockSpec((1, CONV_HALO, CONV_WIDTH), tail),
        pl.BlockSpec((1, POOL_HALO, POOL_WIDTH), tail),
        pl.BlockSpec((SEQS_PER_TILE, DEC_SEQ, CONV_WIDTH), samp),
        pl.BlockSpec((SEQS_PER_TILE, DEC_SEQ, POOL_WIDTH), samp),
        pl.BlockSpec((SEQS_PER_TILE, DEC_SEQ, CHUNK_WIDTH), samp),
    ]
    out_shape = [
        jax.ShapeDtypeStruct((N_TOK, D_MODEL), _f32),
        jax.ShapeDtypeStruct((N_TOK, LANES), _f32),
        jax.ShapeDtypeStruct((BATCH + 1, CONV_HALO, CONV_WIDTH), _f32),
        jax.ShapeDtypeStruct((BATCH + 1, POOL_HALO, POOL_WIDTH), _f32),
        jax.ShapeDtypeStruct((DEC_BATCH, DEC_SEQ, CONV_WIDTH), _f32),
        jax.ShapeDtypeStruct((DEC_BATCH, DEC_SEQ, POOL_WIDTH), _f32),
        jax.ShapeDtypeStruct((DEC_BATCH, DEC_SEQ, CHUNK_WIDTH), _f32),
    ]
    scratch = [
        pltpu.VMEM((1, CONV_HALO + TM, CONV_WIDTH), _f32),
        pltpu.VMEM((1, POOL_HALO + TM, POOL_WIDTH), _f32),
        pltpu.VMEM((SEQS_PER_TILE, CONV_HALO + DEC_SEQ, CONV_WIDTH), _f32),
        pltpu.VMEM((SEQS_PER_TILE, POOL_HALO + DEC_SEQ, POOL_WIDTH), _f32),
        pltpu.VMEM((TM, D_MODEL), _bf16),
    ]
    assert len(in_specs) - len(x_specs) + len(out_specs) + len(scratch) == N_MIXER_COMMON_REFS
    if fused:
        scratch += [pltpu.VMEM((2, 2, TM, D_MODEL), _f32), pltpu.SemaphoreType.DMA((2, 2))]
    return pl.pallas_call(
        functools.partial(_mixer_kernel, fused),
        grid_spec=pltpu.PrefetchScalarGridSpec(
            num_scalar_prefetch=1 if fused else 0,
            grid=(N_TILES,),
            in_specs=in_specs,
            out_specs=out_specs,
            scratch_shapes=scratch,
        ),
        out_shape=out_shape,
        compiler_params=pltpu.CompilerParams(
            dimension_semantics=("arbitrary",), vmem_limit_bytes=VMEM_LIMIT),
        name="mixer",
    )(*x_src, *params)


def _slots_kernel(ids_ref, slot_ref, tile_expert_ref, n_used_ref, count_ref, offset_ref):
    ids = ids_ref[...]
    lane = lax.broadcasted_iota(jnp.int32, (1, LANES), 1)
    counts = jnp.zeros((1, LANES), _f32)
    offsets = jnp.zeros((1, LANES), _f32)
    r_i = lax.broadcasted_iota(jnp.int32, (LANES, LANES), 0)
    c_i = lax.broadcasted_iota(jnp.int32, (LANES, LANES), 1)
    upper = (r_i < c_i).astype(_bf16)
    rr = lax.broadcasted_iota(jnp.int32, (ASSIGN_ROWS, ASSIGN_ROWS), 0)
    rc = lax.broadcasted_iota(jnp.int32, (ASSIGN_ROWS, ASSIGN_ROWS), 1)
    lower = (rc < rr).astype(_bf16)
    tile_row0 = (lax.broadcasted_iota(jnp.int32, (1, LANES), 1) * TE).astype(_f32)

    slot = jnp.zeros((ASSIGN_ROWS, LANES), _f32)
    tile_expert = jnp.zeros((1, LANES), _f32)
    off = jnp.zeros((1, 1), _f32)
    for e in range(N_EXPERTS):
        m = (ids == e).astype(_f32)
        within = _dot(m.astype(_bf16), upper)
        rowsum = jnp.sum(m, axis=1, keepdims=True)
        rowpre = _dot(lower, jnp.broadcast_to(rowsum, (ASSIGN_ROWS, LANES)).astype(_bf16))
        cnt = jnp.sum(rowsum, axis=0, keepdims=True)
        padded = jnp.floor((cnt + (TE - 1)) * (1.0 / TE)) * TE
        slot = slot + m * (off + rowpre + within)
        in_seg = (tile_row0 >= off) & (tile_row0 < off + padded)
        tile_expert = tile_expert + jnp.where(in_seg, float(e), 0.0)
        counts = counts + jnp.where(lane == e, cnt, 0.0)
        offsets = offsets + jnp.where(lane == e, off, 0.0)
        off = off + padded
    tile_expert = jnp.where(tile_row0 >= off, float(N_EXPERTS - 1), tile_expert)
    slot_ref[...] = slot.astype(jnp.int32)
    tile_expert_ref[...] = tile_expert.astype(jnp.int32)
    n_used_ref[...] = jnp.broadcast_to(off * (1.0 / TE), (1, LANES)).astype(jnp.int32)
    count_ref[...] = counts.astype(jnp.int32)
    offset_ref[...] = offsets.astype(jnp.int32)


def _slots_call(ids):
    return pl.pallas_call(
        _slots_kernel,
        out_shape=[jax.ShapeDtypeStruct((ASSIGN_ROWS, LANES), jnp.int32)]
        + [jax.ShapeDtypeStruct((1, LANES), jnp.int32)] * 4,
        name="slots",
    )(ids)


X_BUFS = 3
Y_BUFS = 2


def _experts_kernel(layer, te_ref, nu_ref, count_ref, offset_ref, slot_ref,
                    x1_hbm, wg_hbm, wu_hbm, wd_hbm, y_in_hbm, y_hbm,
                    tok_of_slot, xbuf, ybuf, wg_st, wu_st, wd_st, wg_b, wu_b, wd_b, sem_x, sem_y, sem_w):
    del y_in_hbm
    n_used = nu_ref[0]
    tile_shift = TE.bit_length() - 1

    for k in range(2):
        def fill(t, carry, k=k):
            tok_of_slot[slot_ref[k * N_TOK + t]] = t
            return carry
        lax.fori_loop(0, N_TOK, fill, 0, unroll=16)
    for e in range(N_EXPERTS):
        first_pad = offset_ref[e] + count_ref[e]
        seg_end = offset_ref[e] + (((count_ref[e] + (TE - 1)) >> tile_shift) << tile_shift)

        def pad(s, carry):
            tok_of_slot[s] = 0
            return carry
        lax.fori_loop(first_pad, seg_end, pad, 0)

    def gather(tile, buf):
        base = tile * TE
        for r in range(TE):
            pltpu.make_async_copy(x1_hbm.at[pl.ds(tok_of_slot[base + r], 1)],
                                  xbuf.at[buf, pl.ds(r, 1)], sem_x.at[buf]).start(priority=r % 2)

    def weight_copies(expert, ws):
        return [pltpu.make_async_copy(src.at[layer, expert], dst.at[ws], sem_w.at[ws])
                for src, dst in ((wg_hbm, wg_st), (wu_hbm, wu_st), (wd_hbm, wd_st))]

    def y_copy(tile, buf):
        return pltpu.make_async_copy(ybuf.at[buf], y_hbm.at[pl.ds(tile * TE, TE)], sem_y.at[buf])

    for c in weight_copies(te_ref[0], 0):
        c.start()
    gather(0, 0)

    @pl.when(n_used > 1)
    def _():
        gather(1, 1)

    def tile_step(j, seg):
        expert = te_ref[j]
        first = (j == 0) | (te_ref[jnp.maximum(j - 1, 0)] != expert)

        @pl.when(j + (X_BUFS - 1) < n_used)
        def _():
            gather(j + (X_BUFS - 1), (j + (X_BUFS - 1)) % X_BUFS)

        @pl.when(first)
        def _():
            ws = seg % 2
            for c in weight_copies(expert, ws):
                c.wait()
            wg_b[...] = wg_st[ws].astype(_bf16)
            wu_b[...] = wu_st[ws].astype(_bf16)
            wd_b[...] = wd_st[ws].astype(_bf16)
            next_first = j + ((count_ref[expert] + (TE - 1)) >> tile_shift)

            @pl.when(next_first < n_used)
            def _():
                for c in weight_copies(te_ref[next_first], 1 - ws):
                    c.start()

        xslot = j % X_BUFS
        pltpu.make_async_copy(x1_hbm.at[pl.ds(0, TE)], xbuf.at[xslot], sem_x.at[xslot]).wait()
        xb = xbuf[xslot].astype(_bf16)
        hg = _dot(xb, wg_b[...])
        hu = _dot(xb, wu_b[...])
        a = hg / (1.0 + jnp.exp(-hg)) * hu
        y = _dot(a.astype(_bf16), wd_b[...])

        yslot = j % Y_BUFS

        @pl.when(j >= Y_BUFS)
        def _():
            y_copy(j - Y_BUFS, yslot).wait()

        ybuf[yslot] = y
        y_copy(j, yslot).start()
        return seg + first.astype(jnp.int32)

    lax.fori_loop(0, n_used, tile_step, jnp.int32(0))

    for back in range(1, Y_BUFS + 1):
        @pl.when(n_used >= back)
        def _(back=back):
            y_copy(n_used - back, (n_used - back) % Y_BUFS).wait()


def _experts_call(layer, tile_expert, n_used, counts, offsets, slot_flat, x1, wg, wu, wd, y_buf):
    any_spec = pl.BlockSpec(memory_space=pl.ANY)
    n_prefetch = 5
    return pl.pallas_call(
        functools.partial(_experts_kernel, layer),
        grid_spec=pltpu.PrefetchScalarGridSpec(
            num_scalar_prefetch=n_prefetch,
            grid=(1,),
            in_specs=[any_spec] * 5,
            out_specs=any_spec,
            scratch_shapes=[
                pltpu.SMEM((N_SLOTS,), jnp.int32),
                pltpu.VMEM((X_BUFS, TE, D_MODEL), _f32),
                pltpu.VMEM((Y_BUFS, TE, D_MODEL), _f32),
                pltpu.VMEM((2, D_MODEL, D_EXPERT), _f32),
                pltpu.VMEM((2, D_MODEL, D_EXPERT), _f32),
                pltpu.VMEM((2, D_EXPERT, D_MODEL), _f32),
                pltpu.VMEM((D_MODEL, D_EXPERT), _bf16),
                pltpu.VMEM((D_MODEL, D_EXPERT), _bf16),
                pltpu.VMEM((D_EXPERT, D_MODEL), _bf16),
                pltpu.SemaphoreType.DMA((X_BUFS,)),
                pltpu.SemaphoreType.DMA((Y_BUFS,)),
                pltpu.SemaphoreType.DMA((2,)),
            ],
        ),
        out_shape=jax.ShapeDtypeStruct((N_SLOTS, D_MODEL), _f32),
        input_output_aliases={n_prefetch + 4: 0},
        compiler_params=pltpu.CompilerParams(
            dimension_semantics=("arbitrary",), vmem_limit_bytes=VMEM_LIMIT),
        name="experts",
    )(tile_expert, n_used, counts, offsets, slot_flat, x1, wg, wu, wd, y_buf)


def _combine_kernel(slot_ref, x1_ref, route_ref, g2_ref, b2_ref, y_hbm, out_p_ref, out_s_ref,
                    ybuf, sems):
    i = pl.program_id(0)
    b = i % 2

    def gather(tile, slot):
        tok0 = tile * TM

        def start(r, carry):
            for k in range(2):
                pltpu.make_async_copy(y_hbm.at[pl.ds(slot_ref[k * N_TOK + tok0 + r], 1)],
                                      ybuf.at[slot, k, pl.ds(r, 1)], sems.at[slot, k]).start(priority=k)
            return carry

        lax.fori_loop(0, TM, start, 0, unroll=16)

    @pl.when(i == 0)
    def _():
        gather(0, 0)

    @pl.when(i + 1 < pl.num_programs(0))
    def _():
        gather(i + 1, 1 - b)

    for k in range(2):
        pltpu.make_async_copy(y_hbm.at[pl.ds(0, TM)], ybuf.at[b, k], sems.at[b, k]).wait()

    route = route_ref[...]
    moe = route[:, 2:3] * ybuf[b, 0] + route[:, 3:4] * ybuf[b, 1]
    out = _layer_norm(ALPHA * x1_ref[...] + moe, g2_ref[...], b2_ref[...])

    @pl.when(i < N_PROMPT_TILES)
    def _():
        out_p_ref[...] = out

    @pl.when(i >= N_PROMPT_TILES)
    def _():
        out_s_ref[...] = out


def _combine_call(slot_flat, x1, route, g2, b2, y):
    tok = lambda i, s: (i, 0)
    const = lambda i, s: (0, 0)
    return pl.pallas_call(
        _combine_kernel,
        grid_spec=pltpu.PrefetchScalarGridSpec(
            num_scalar_prefetch=1,
            grid=(N_TILES,),
            in_specs=[
                pl.BlockSpec((TM, D_MODEL), tok),
                pl.BlockSpec((TM, LANES), tok),
                pl.BlockSpec((1, D_MODEL), const),
                pl.BlockSpec((1, D_MODEL), const),
                pl.BlockSpec(memory_space=pl.ANY),
            ],
            out_specs=[
                pl.BlockSpec((TM, D_MODEL), lambda i, s: (jnp.minimum(i, N_PROMPT_TILES - 1), 0)),
                pl.BlockSpec((TM, D_MODEL), lambda i, s: (jnp.maximum(i - N_PROMPT_TILES, 0), 0)),
            ],
            scratch_shapes=[
                pltpu.VMEM((2, 2, TM, D_MODEL), _f32),
                pltpu.SemaphoreType.DMA((2, 2)),
            ],
        ),
        out_shape=[jax.ShapeDtypeStruct((N_PROMPT, D_MODEL), _f32),
                   jax.ShapeDtypeStruct((N_SAMPLE, D_MODEL), _f32)],
        compiler_params=pltpu.CompilerParams(dimension_semantics=("arbitrary",)),
        name="combine",
    )(slot_flat, x1, route, g2, b2, y)


def kernel(x_prompt, x_sample, state_conv, state_pool, w_in, conv_w, sgu_ln_g, sgu_ln_b, sgu_w, sgu_b, pool_w, pool_scale, w_out, ln1_g, ln1_b, router_group_w, router_group_b, router_expert_w, router_expert_b, expert_w_gate, expert_w_up, expert_w_down, ln2_g, ln2_b):
    y_buf = jnp.zeros((N_SLOTS, D_MODEL), _f32)
    reps = CHUNK // DEC_SEQ
    row = lambda a: a.reshape(DEPTH, 1, -1)

    w_r = jnp.concatenate([router_group_w, router_expert_w], axis=2)
    w_r = jnp.pad(w_r, ((0, 0), (0, 0), (0, LANES - w_r.shape[2])))
    wr_hi = w_r.astype(_bf16)
    wr = jnp.concatenate([wr_hi, (w_r - wr_hi.astype(_f32)).astype(_bf16)], axis=2)
    b_r = jnp.concatenate([router_group_b, router_expert_b], axis=1)
    b_r = row(jnp.pad(b_r, ((0, 0), (0, LANES - b_r.shape[1]))))
    ws_s = jnp.tile(sgu_w[:, :, :DEC_SEQ, :DEC_SEQ], (1, 1, reps, reps))
    bias_p = jnp.swapaxes(sgu_b, 1, 2)
    bias_s = jnp.tile(bias_p[:, :DEC_SEQ], (1, reps, 1))
    zst = jnp.pad(state_conv, ((0, 0), (0, 0), (CONV_HALO - 2, 0), (0, 0)))
    pst = jnp.pad(state_pool, ((0, 0), (0, 0), (POOL_HALO - POOL_BUF, 0), (0, 0)))
    mixer_params = (w_in.astype(_bf16), w_out.astype(_bf16), conv_w, row(sgu_ln_g), row(sgu_ln_b), sgu_w,
                    ws_s, bias_p, bias_s, pool_w, row(pool_scale), row(ln1_g), row(ln1_b), wr, b_r, zst, pst)
    g2, b2 = row(ln2_g), row(ln2_b)

    ztails, ptails, zss, psamps, v_s = [], [], [], [], []
    x_src = (x_prompt.reshape(N_PROMPT, D_MODEL), x_sample.reshape(N_SAMPLE, D_MODEL))
    for l in range(DEPTH):
        x1, route, ztail, ptail, zs, psamp, vns = _mixer_call(l, x_src, mixer_params)
        ids = route[:, 0:2].astype(jnp.int32).T.reshape(ASSIGN_ROWS, LANES)
        slot, tile_expert, n_used, counts, offsets = _slots_call(ids)
        slot_flat = slot.reshape(N_ASSIGN)
        y = _experts_call(l, tile_expert.reshape(LANES), n_used.reshape(LANES)[:1], counts.reshape(LANES),
                          offsets.reshape(LANES), slot_flat, x1, expert_w_gate, expert_w_up, expert_w_down,
                          y_buf)
        x_src = (slot_flat, x1, route, y, g2, b2)
        y_buf = y
        ztails.append(ztail)
        ptails.append(ptail)
        zss.append(zs)
        psamps.append(psamp)
        v_s.append(vns)

    x_p, x_s = _combine_call(slot_flat, x1, route, g2[DEPTH - 1], b2[DEPTH - 1], y)
    y_prompt = x_p.reshape(BATCH, SEQ, D_MODEL)
    y_sample = x_s.reshape(DEC_BATCH, DEC_SEQ, D_MODEL)
    new_conv_prompt = jnp.stack(ztails)[:, :BATCH, CONV_HALO - 2:]
    new_pool_prompt = jnp.stack(ptails)[:, :BATCH, POOL_HALO - POOL_BUF:]
    new_conv_sample = jnp.stack(zss)[:, :, DEC_SEQ - 2:]
    new_pool_sample = jnp.concatenate([state_pool[:, :, DEC_SEQ:], jnp.stack(psamps)], axis=2)
    return (y_prompt, y_sample, new_conv_prompt, new_pool_prompt, new_conv_sample, new_pool_sample,
            jnp.stack(v_s))
```

```python
import functools

import jax
import jax.numpy as jnp
from jax import lax
from jax.experimental import pallas as pl
from jax.experimental.pallas import tpu as pltpu

D_MODEL = 2048
BATCH = 4
SEQ = 2048
DEPTH = 4
DEC_BATCH = 128
DEC_SEQ = 8
PAST_LEN = 16384
HEAD_DIM = 128
CONV_WIDTH = 768
POOL_WIDTH = 512
CHUNK_WIDTH = 768
CHUNK = 128
N_CHUNK_HEADS = 6
POOL_WINDOWS = (2, 4, 8, 16)
POOL_GROUP_DIM = 128
POOL_BUF = 15
IN_WIDTH = 4352
N_EXPERT_GROUPS = 4
EXPERTS_PER_GROUP = 4
N_EXPERTS = 16
D_EXPERT = 512
ALPHA = (2 * DEPTH) ** 0.25
LN_EPS = 1e-5

N_PROMPT = BATCH * SEQ
N_SAMPLE = DEC_BATCH * DEC_SEQ
N_TOK = N_PROMPT + N_SAMPLE
N_ASSIGN = 2 * N_TOK

LANES = 128
SUBLANES = 8
CONV_HALO = 8
POOL_HALO = 16

TM = 256
TILES_PER_SEQ = SEQ // TM
N_PROMPT_TILES = N_PROMPT // TM
N_SAMPLE_TILES = N_SAMPLE // TM
N_TILES = N_PROMPT_TILES + N_SAMPLE_TILES
SEQS_PER_TILE = TM // DEC_SEQ

TE = 256
N_SLOTS = N_ASSIGN + N_EXPERTS * TE
N_ETILES = N_SLOTS // TE
ASSIGN_ROWS = N_ASSIGN // LANES

CUT_B, CUT_C, CUT_H, CUT_U, CUT_V, CUT_P = 0, 768, 1536, 2304, 3072, 3840

VMEM_LIMIT = 62 * 1024 * 1024

_f32 = jnp.float32
_bf16 = jnp.bfloat16


def _dot(a, b):
    return jnp.dot(a, b, preferred_element_type=_f32)


def _layer_norm(r, g, b):
    mu = jnp.mean(r, axis=-1, keepdims=True)
    c = r - mu
    var = jnp.mean(c * c, axis=-1, keepdims=True)
    return c * lax.rsqrt(var + LN_EPS) * g + b


class _RowGather:
    def __init__(self, start_row, first_row):
        self._start_row = start_row
        self._done = first_row

    def upto(self, row_end):
        for r in range(self._done, row_end):
            self._start_row(r)
        self._done = max(self._done, row_end)


class _NoGather:
    def upto(self, row_end):
        del row_end


BRANCH_GATHER_ROWS = 192


def _mixer_body(xb, w_in_ref, conv_w_ref, sg_ref, sb_ref, ws_ref, bias_ref, pw_ref, ps_ref,
                zext_ref, pext_ref, ybuf_ref, seq_len, pos, chunk_len, gather):
    n_seq = TM // seq_len

    z = _dot(xb, w_in_ref[:, CUT_C:CUT_H]) * _dot(xb, w_in_ref[:, CUT_H:CUT_U])
    zext_ref[:, CONV_HALO:, :] = z.reshape(n_seq, seq_len, CONV_WIDTH)
    gather.upto(16)
    cw = conv_w_ref[...]
    conv = (cw[0:1, :] * zext_ref[:, CONV_HALO - 2:CONV_HALO - 2 + seq_len, :]
            + cw[1:2, :] * zext_ref[:, CONV_HALO - 1:CONV_HALO - 1 + seq_len, :]
            + cw[2:3, :] * zext_ref[:, CONV_HALO:, :])
    y_conv = _dot(xb, w_in_ref[:, CUT_B:CUT_C]) * conv.reshape(TM, CONV_WIDTH)
    ybuf_ref[:, 0:CONV_WIDTH] = y_conv.astype(_bf16)
    gather.upto(32)

    v = _dot(xb, w_in_ref[:, CUT_V:CUT_P])
    vn = _layer_norm(v, sg_ref[...], sb_ref[...])
    vnb = vn.astype(_bf16)
    gather.upto(48)
    r_i = lax.broadcasted_iota(jnp.int32, (CHUNK, CHUNK), 0)
    c_i = lax.broadcasted_iota(jnp.int32, (CHUNK, CHUNK), 1)
    shift = chunk_len.bit_length() - 1
    mask = (c_i <= r_i) & ((r_i >> shift) == (c_i >> shift))
    u = _dot(xb, w_in_ref[:, CUT_U:CUT_V])
    gather.upto(64)
    n_chunks = TM // CHUNK
    for h in range(N_CHUNK_HEADS):
        m_h = jnp.where(mask, ws_ref[h], 0.0).astype(_bf16)
        b_h = bias_ref[:, h:h + 1]
        cols = slice(h * HEAD_DIM, (h + 1) * HEAD_DIM)
        vn_h = jnp.concatenate([vnb[c * CHUNK:(c + 1) * CHUNK, cols] for c in range(n_chunks)], axis=1)
        mixed = _dot(m_h, vn_h)
        for c in range(n_chunks):
            rows = slice(c * CHUNK, (c + 1) * CHUNK)
            ybuf_ref[rows, CONV_WIDTH + h * HEAD_DIM:CONV_WIDTH + (h + 1) * HEAD_DIM] = (
                u[rows, cols] * (mixed[:, c * HEAD_DIM:(c + 1) * HEAD_DIM] + b_h)).astype(_bf16)
        gather.upto(64 + 8 * (h + 1))

    p = _dot(xb, w_in_ref[:, CUT_P:IN_WIDTH])
    pext_ref[:, POOL_HALO:, :] = p.reshape(n_seq, seq_len, POOL_WIDTH)
    gather.upto(128)
    posf = (pos + 1).astype(_f32)
    for g, w in enumerate(POOL_WINDOWS):
        cols = slice(g * POOL_GROUP_DIM, (g + 1) * POOL_GROUP_DIM)
        s = pext_ref[:, POOL_HALO:, cols]
        for j in range(1, w):
            s = s + pext_ref[:, POOL_HALO - j:POOL_HALO - j + seq_len, cols]
        cnt = jnp.minimum(float(w), posf)
        diff = s.reshape(TM, POOL_GROUP_DIM) / cnt - p[:, cols]
        y_g = _dot(diff.astype(_bf16), pw_ref[g].astype(_bf16)) * ps_ref[:, cols]
        ybuf_ref[:, CONV_WIDTH + CHUNK_WIDTH + g * POOL_GROUP_DIM:
                 CONV_WIDTH + CHUNK_WIDTH + (g + 1) * POOL_GROUP_DIM] = y_g.astype(_bf16)
        gather.upto(128 + 16 * (g + 1))
    assert 128 + 16 * len(POOL_WINDOWS) == BRANCH_GATHER_ROWS
    return z, p, vn


def _route(x1, wr_ref, br_ref):
    x_hi = x1.astype(_bf16)
    x_lo = (x1 - x_hi.astype(_f32)).astype(_bf16)
    hi_parts = _dot(x_hi, wr_ref[...])
    logits = hi_parts[:, :LANES] + hi_parts[:, LANES:] + _dot(x_lo, wr_ref[:, :LANES]) + br_ref[...]
    lane = lax.broadcasted_iota(jnp.int32, logits.shape, 1)
    neg = -jnp.inf
    big = jnp.int32(1 << 20)

    gmask = lane < N_EXPERT_GROUPS
    gl = jnp.where(gmask, logits, neg)
    ge = jnp.exp(gl - jnp.max(gl, axis=1, keepdims=True))
    gp = ge / jnp.sum(ge, axis=1, keepdims=True)
    g_val = jnp.max(gp, axis=1, keepdims=True)
    g_idx = jnp.min(jnp.where(gmask & (gp == g_val), lane, big), axis=1, keepdims=True)

    e_lane = lane - N_EXPERT_GROUPS
    emask = (e_lane >= 0) & (e_lane < N_EXPERTS) & ((e_lane >> (EXPERTS_PER_GROUP.bit_length() - 1)) == g_idx)
    el = jnp.where(emask, logits, neg)
    ee = jnp.exp(el - jnp.max(el, axis=1, keepdims=True))
    ep = ee / jnp.sum(ee, axis=1, keepdims=True)
    v1 = jnp.max(jnp.where(emask, ep, -1.0), axis=1, keepdims=True)
    i1 = jnp.min(jnp.where(emask & (ep == v1), e_lane, big), axis=1, keepdims=True)
    mask2 = emask & (e_lane != i1)
    v2 = jnp.max(jnp.where(mask2, ep, -1.0), axis=1, keepdims=True)
    i2 = jnp.min(jnp.where(mask2 & (ep == v2), e_lane, big), axis=1, keepdims=True)
    tot = v1 + v2
    gate1 = g_val * (v1 / tot)
    gate2 = g_val * (v2 / tot)
    return jnp.where(lane == 0, i1.astype(_f32),
                     jnp.where(lane == 1, i2.astype(_f32),
                               jnp.where(lane == 2, gate1, jnp.where(lane == 3, gate2, 0.0))))


N_MIXER_COMMON_REFS = 29


def _mixer_kernel(fused, *refs):
    if fused:
        slot_ref, x1p_ref, routep_ref, y_hbm, g2_ref, b2_ref = refs[:6]
        refs = refs[6:]
    else:
        xp_ref, xsamp_ref = refs[:2]
        refs = refs[2:]
    (w_in_ref, w_out_ref, conv_w_ref, sg_ref, sb_ref, ws_p_ref, ws_s_ref, bias_p_ref, bias_s_ref,
     pw_ref, ps_ref, g1_ref, b1_ref, wr_ref, br_ref, zst_ref, pst_ref,
     x1_ref, route_ref, ztail_ref, ptail_ref, zs_ref, psamp_ref, vns_ref,
     zext_p, pext_p, zext_s, pext_s, ybuf_ref) = refs[:N_MIXER_COMMON_REFS]
    i = pl.program_id(0)
    last = pl.num_programs(0) - 1

    if fused:
        gbuf, sems = refs[N_MIXER_COMMON_REFS:]
        b = i % 2

        def row_starter(tile, half):
            tok0 = tile * TM

            def start_row(r):
                for k in range(2):
                    pltpu.make_async_copy(y_hbm.at[pl.ds(slot_ref[k * N_TOK + tok0 + r], 1)],
                                          gbuf.at[half, k, pl.ds(r, 1)], sems.at[half, k]).start(priority=k)
            return start_row

        def wait_rows(half):
            for k in range(2):
                pltpu.make_async_copy(y_hbm.at[pl.ds(0, TM)], gbuf.at[half, k], sems.at[half, k]).wait()

        @pl.when(i == 0)
        def _():
            first = row_starter(0, 0)

            def body(r, carry):
                first(r)
                return carry

            lax.fori_loop(0, TM, body, 0, unroll=16)

        wait_rows(b)
        routep = routep_ref[...]
        moe = routep[:, 2:3] * gbuf[b, 0] + routep[:, 3:4] * gbuf[b, 1]
        x = _layer_norm(ALPHA * x1p_ref[...] + moe, g2_ref[...], b2_ref[...])
        next_rows = row_starter(jnp.minimum(i + 1, last), 1 - b)
        new_gather = lambda first_row: _RowGather(next_rows, first_row)
    else:
        x = jnp.where(i < N_PROMPT_TILES, xp_ref[...], xsamp_ref[...])
        new_gather = lambda first_row: _NoGather()

    xb = x.astype(_bf16)
    row = lax.broadcasted_iota(jnp.int32, (TM, 1), 0)

    @pl.when(i < N_PROMPT_TILES)
    def _prompt():
        t0 = (i % TILES_PER_SEQ) * TM

        @pl.when(t0 == 0)
        def _():
            zext_p[:, 0:CONV_HALO, :] = jnp.zeros((1, CONV_HALO, CONV_WIDTH), _f32)
            pext_p[:, 0:POOL_HALO, :] = jnp.zeros((1, POOL_HALO, POOL_WIDTH), _f32)

        z, p, _ = _mixer_body(xb, w_in_ref, conv_w_ref, sg_ref, sb_ref, ws_p_ref, bias_p_ref,
                              pw_ref, ps_ref, zext_p, pext_p, ybuf_ref, TM, t0 + row, CHUNK, new_gather(0))
        z_last = z[TM - CONV_HALO:, :].reshape(1, CONV_HALO, CONV_WIDTH)
        p_last = p[TM - POOL_HALO:, :].reshape(1, POOL_HALO, POOL_WIDTH)
        zext_p[:, 0:CONV_HALO, :] = z_last
        pext_p[:, 0:POOL_HALO, :] = p_last
        ztail_ref[...] = z_last
        ptail_ref[...] = p_last

    @pl.when(i >= N_PROMPT_TILES)
    def _sample():
        zext_s[:, 0:CONV_HALO, :] = zst_ref[...]
        pext_s[:, 0:POOL_HALO, :] = pst_ref[...]
        z, p, vn = _mixer_body(xb, w_in_ref, conv_w_ref, sg_ref, sb_ref, ws_s_ref, bias_s_ref,
                               pw_ref, ps_ref, zext_s, pext_s, ybuf_ref, DEC_SEQ,
                               PAST_LEN + (row & (DEC_SEQ - 1)), DEC_SEQ, new_gather(0))
        zs_ref[...] = z.reshape(SEQS_PER_TILE, DEC_SEQ, CONV_WIDTH)
        psamp_ref[...] = p.reshape(SEQS_PER_TILE, DEC_SEQ, POOL_WIDTH)
        vns_ref[...] = vn.reshape(SEQS_PER_TILE, DEC_SEQ, CHUNK_WIDTH)
        ztail_ref[...] = jnp.zeros(ztail_ref.shape, _f32)
        ptail_ref[...] = jnp.zeros(ptail_ref.shape, _f32)

    tail_gather = new_gather(BRANCH_GATHER_ROWS)
    mix = _dot(ybuf_ref[...], w_out_ref[...])
    tail_gather.upto(BRANCH_GATHER_ROWS + (TM - BRANCH_GATHER_ROWS) // 2)
    x1 = _layer_norm(ALPHA * x + mix, g1_ref[...], b1_ref[...])
    x1_ref[...] = x1
    tail_gather.upto(TM)
    route_ref[...] = _route(x1, wr_ref, br_ref)

    if fused:
        @pl.when(i == last)
        def _():
            wait_rows(1 - b)


def _layer_spec(layer, shape):
    nd = len(shape)
    return pl.BlockSpec((None,) + shape, lambda i, *_: (layer,) + (0,) * nd, pipeline_mode=pl.Buffered(1))


def _mixer_call(layer, x_src, params):
    fused = len(x_src) == 6
    samp = lambda i, *_: (jnp.maximum(i - N_PROMPT_TILES, 0), 0, 0)
    samp_l = lambda i, *_: (layer, jnp.maximum(i - N_PROMPT_TILES, 0), 0, 0)
    tail = lambda i, *_: (i // TILES_PER_SEQ, 0, 0)
    tok = lambda i, *_: (i, 0)
    if fused:
        x_specs = [
            pl.BlockSpec((TM, D_MODEL), tok),
            pl.BlockSpec((TM, LANES), tok),
            pl.BlockSpec(memory_space=pl.ANY),
            _layer_spec(layer - 1, (1, D_MODEL)),
            _layer_spec(layer - 1, (1, D_MODEL)),
        ]
    else:
        x_specs = [
            pl.BlockSpec((TM, D_MODEL), lambda i, *_: (jnp.minimum(i, N_PROMPT_TILES - 1), 0)),
            pl.BlockSpec((TM, D_MODEL), lambda i, *_: (jnp.maximum(i - N_PROMPT_TILES, 0), 0)),
        ]
    in_specs = x_specs + [
        _layer_spec(layer, (D_MODEL, IN_WIDTH)),
        _layer_spec(layer, (D_MODEL, D_MODEL)),
        _layer_spec(layer, (3, CONV_WIDTH)),
        _layer_spec(layer, (1, CHUNK_WIDTH)),
        _layer_spec(layer, (1, CHUNK_WIDTH)),
        _layer_spec(layer, (N_CHUNK_HEADS, CHUNK, CHUNK)),
        _layer_spec(layer, (N_CHUNK_HEADS, CHUNK, CHUNK)),
        _layer_spec(layer, (CHUNK, N_CHUNK_HEADS)),
        _layer_spec(layer, (CHUNK, N_CHUNK_HEADS)),
        _layer_spec(layer, (4, POOL_GROUP_DIM, POOL_GROUP_DIM)),
        _layer_spec(layer, (1, POOL_WIDTH)),
        _layer_spec(layer, (1, D_MODEL)),
        _layer_spec(layer, (1, D_MODEL)),
        _layer_spec(layer, (D_MODEL, 2 * LANES)),
        _layer_spec(layer, (1, LANES)),
        pl.BlockSpec((None, SEQS_PER_TILE, CONV_HALO, CONV_WIDTH), samp_l),
        pl.BlockSpec((None, SEQS_PER_TILE, POOL_HALO, POOL_WIDTH), samp_l),
    ]
    out_specs = [
        pl.BlockSpec((TM, D_MODEL), tok),
        pl.BlockSpec((TM, LANES), tok),
        pl.BlockSpec((1, CONV_HALO, CONV_WIDTH), tail),
        pl.BlockSpec((1, POOL_HALO, POOL_WIDTH), tail),
        pl.BlockSpec((SEQS_PER_TILE, DEC_SEQ, CONV_WIDTH), samp),
        pl.BlockSpec((SEQS_PER_TILE, DEC_SEQ, POOL_WIDTH), samp),
        pl.BlockSpec((SEQS_PER_TILE, DEC_SEQ, CHUNK_WIDTH), samp),
    ]
    out_shape = [
        jax.ShapeDtypeStruct((N_TOK, D_MODEL), _f32),
        jax.ShapeDtypeStruct((N_TOK, LANES), _f32),
        jax.ShapeDtypeStruct((BATCH + 1, CONV_HALO, CONV_WIDTH), _f32),
        jax.ShapeDtypeStruct((BATCH + 1, POOL_HALO, POOL_WIDTH), _f32),
        jax.ShapeDtypeStruct((DEC_BATCH, DEC_SEQ, CONV_WIDTH), _f32),
        jax.ShapeDtypeStruct((DEC_BATCH, DEC_SEQ, POOL_WIDTH), _f32),
        jax.ShapeDtypeStruct((DEC_BATCH, DEC_SEQ, CHUNK_WIDTH), _f32),
    ]
    scratch = [
        pltpu.VMEM((1, CONV_HALO + TM, CONV_WIDTH), _f32),
        pltpu.VMEM((1, POOL_HALO + TM, POOL_WIDTH), _f32),
        pltpu.VMEM((SEQS_PER_TILE, CONV_HALO + DEC_SEQ, CONV_WIDTH), _f32),
        pltpu.VMEM((SEQS_PER_TILE, POOL_HALO + DEC_SEQ, POOL_WIDTH), _f32),
        pltpu.VMEM((TM, D_MODEL), _bf16),
    ]
    assert len(in_specs) - len(x_specs) + len(out_specs) + len(scratch) == N_MIXER_COMMON_REFS
    if fused:
        scratch += [pltpu.VMEM((2, 2, TM, D_MODEL), _f32), pltpu.SemaphoreType.DMA((2, 2))]
    return pl.pallas_call(
        functools.partial(_mixer_kernel, fused),
        grid_spec=pltpu.PrefetchScalarGridSpec(
            num_scalar_prefetch=1 if fused else 0,
            grid=(N_TILES,),
            in_specs=in_specs,
            out_specs=out_specs,
            scratch_shapes=scratch,
        ),
        out_shape=out_shape,
        compiler_params=pltpu.CompilerParams(
            dimension_semantics=("arbitrary",), vmem_limit_bytes=VMEM_LIMIT),
        name="mixer",
    )(*x_src, *params)


def _slots_kernel(ids_ref, slot_ref, tile_expert_ref, n_used_ref, count_ref, offset_ref):
    ids = ids_ref[...]
    lane = lax.broadcasted_iota(jnp.int32, (1, LANES), 1)
    counts = jnp.zeros((1, LANES), _f32)
    offsets = jnp.zeros((1, LANES), _f32)
    r_i = lax.broadcasted_iota(jnp.int32, (LANES, LANES), 0)
    c_i = lax.broadcasted_iota(jnp.int32, (LANES, LANES), 1)
    upper = (r_i < c_i).astype(_bf16)
    rr = lax.broadcasted_iota(jnp.int32, (ASSIGN_ROWS, ASSIGN_ROWS), 0)
    rc = lax.broadcasted_iota(jnp.int32, (ASSIGN_ROWS, ASSIGN_ROWS), 1)
    lower = (rc < rr).astype(_bf16)
    tile_row0 = (lax.broadcasted_iota(jnp.int32, (1, LANES), 1) * TE).astype(_f32)

    slot = jnp.zeros((ASSIGN_ROWS, LANES), _f32)
    tile_expert = jnp.zeros((1, LANES), _f32)
    off = jnp.zeros((1, 1), _f32)
    for e in range(N_EXPERTS):
        m = (ids == e).astype(_f32)
        within = _dot(m.astype(_bf16), upper)
        rowsum = jnp.sum(m, axis=1, keepdims=True)
        rowpre = _dot(lower, jnp.broadcast_to(rowsum, (ASSIGN_ROWS, LANES)).astype(_bf16))
        cnt = jnp.sum(rowsum, axis=0, keepdims=True)
        padded = jnp.floor((cnt + (TE - 1)) * (1.0 / TE)) * TE
        slot = slot + m * (off + rowpre + within)
        in_seg = (tile_row0 >= off) & (tile_row0 < off + padded)
        tile_expert = tile_expert + jnp.where(in_seg, float(e), 0.0)
        counts = counts + jnp.where(lane == e, cnt, 0.0)
        offsets = offsets + jnp.where(lane == e, off, 0.0)
        off = off + padded
    tile_expert = jnp.where(tile_row0 >= off, float(N_EXPERTS - 1), tile_expert)
    slot_ref[...] = slot.astype(jnp.int32)
    tile_expert_ref[...] = tile_expert.astype(jnp.int32)
    n_used_ref[...] = jnp.broadcast_to(off * (1.0 / TE), (1, LANES)).astype(jnp.int32)
    count_ref[...] = counts.astype(jnp.int32)
    offset_ref[...] = offsets.astype(jnp.int32)


def _slots_call(ids):
    return pl.pallas_call(
        _slots_kernel,
        out_shape=[jax.ShapeDtypeStruct((ASSIGN_ROWS, LANES), jnp.int32)]
        + [jax.ShapeDtypeStruct((1, LANES), jnp.int32)] * 4,
        name="slots",
    )(ids)


X_BUFS = 3
Y_BUFS = 2
ROW_COPY_PRIORITY = 0
BLOCK_COPY_PRIORITY = 1


def _experts_kernel(layer, te_ref, nu_ref, count_ref, offset_ref, slot_ref,
                    x1_hbm, wg_hbm, wu_hbm, wd_hbm, y_in_hbm, y_hbm,
                    tok_of_slot, xbuf, ybuf, wg_st, wu_st, wd_st, wg_b, wu_b, wd_b, sem_x, sem_y, sem_w):
    del y_in_hbm
    n_used = nu_ref[0]
    tile_shift = TE.bit_length() - 1

    for k in range(2):
        def fill(t, carry, k=k):
            tok_of_slot[slot_ref[k * N_TOK + t]] = t
            return carry
        lax.fori_loop(0, N_TOK, fill, 0, unroll=16)
    for e in range(N_EXPERTS):
        first_pad = offset_ref[e] + count_ref[e]
        seg_end = offset_ref[e] + (((count_ref[e] + (TE - 1)) >> tile_shift) << tile_shift)

        def pad(s, carry):
            tok_of_slot[s] = 0
            return carry
        lax.fori_loop(first_pad, seg_end, pad, 0)

    def gather(tile, buf):
        base = tile * TE
        for r in range(TE):
            pltpu.make_async_copy(x1_hbm.at[pl.ds(tok_of_slot[base + r], 1)],
                                  xbuf.at[buf, pl.ds(r, 1)], sem_x.at[buf]).start(priority=ROW_COPY_PRIORITY)

    def weight_copies(expert, ws):
        return [pltpu.make_async_copy(src.at[layer, expert], dst.at[ws], sem_w.at[ws])
                for src, dst in ((wg_hbm, wg_st), (wu_hbm, wu_st), (wd_hbm, wd_st))]

    def y_copy(tile, buf):
        return pltpu.make_async_copy(ybuf.at[buf], y_hbm.at[pl.ds(tile * TE, TE)], sem_y.at[buf])

    for c in weight_copies(te_ref[0], 0):
        c.start(priority=BLOCK_COPY_PRIORITY)
    gather(0, 0)

    @pl.when(n_used > 1)
    def _():
        gather(1, 1)

    def tile_step(j, seg):
        expert = te_ref[j]
        first = (j == 0) | (te_ref[jnp.maximum(j - 1, 0)] != expert)

        @pl.when(j + (X_BUFS - 1) < n_used)
        def _():
            gather(j + (X_BUFS - 1), (j + (X_BUFS - 1)) % X_BUFS)

        @pl.when(first)
        def _():
            ws = seg % 2
            for c in weight_copies(expert, ws):
                c.wait()
            wg_b[...] = wg_st[ws].astype(_bf16)
            wu_b[...] = wu_st[ws].astype(_bf16)
            wd_b[...] = wd_st[ws].astype(_bf16)
            next_first = j + ((count_ref[expert] + (TE - 1)) >> tile_shift)

            @pl.when(next_first < n_used)
            def _():
                for c in weight_copies(te_ref[next_first], 1 - ws):
                    c.start(priority=BLOCK_COPY_PRIORITY)

        xslot = j % X_BUFS
        pltpu.make_async_copy(x1_hbm.at[pl.ds(0, TE)], xbuf.at[xslot], sem_x.at[xslot]).wait()
        xb = xbuf[xslot].astype(_bf16)
        hg = _dot(xb, wg_b[...])
        hu = _dot(xb, wu_b[...])
        a = hg / (1.0 + jnp.exp(-hg)) * hu
        y = _dot(a.astype(_bf16), wd_b[...])

        yslot = j % Y_BUFS

        @pl.when(j >= Y_BUFS)
        def _():
            y_copy(j - Y_BUFS, yslot).wait()

        ybuf[yslot] = y
        y_copy(j, yslot).start(priority=BLOCK_COPY_PRIORITY)
        return seg + first.astype(jnp.int32)

    lax.fori_loop(0, n_used, tile_step, jnp.int32(0))

    for back in range(1, Y_BUFS + 1):
        @pl.when(n_used >= back)
        def _(back=back):
            y_copy(n_used - back, (n_used - back) % Y_BUFS).wait()


def _experts_call(layer, tile_expert, n_used, counts, offsets, slot_flat, x1, wg, wu, wd, y_buf):
    any_spec = pl.BlockSpec(memory_space=pl.ANY)
    n_prefetch = 5
    return pl.pallas_call(
        functools.partial(_experts_kernel, layer),
        grid_spec=pltpu.PrefetchScalarGridSpec(
            num_scalar_prefetch=n_prefetch,
            grid=(1,),
            in_specs=[any_spec] * 5,
            out_specs=any_spec,
            scratch_shapes=[
                pltpu.SMEM((N_SLOTS,), jnp.int32),
                pltpu.VMEM((X_BUFS, TE, D_MODEL), _f32),
                pltpu.VMEM((Y_BUFS, TE, D_MODEL), _f32),
                pltpu.VMEM((2, D_MODEL, D_EXPERT), _f32),
                pltpu.VMEM((2, D_MODEL, D_EXPERT), _f32),
                pltpu.VMEM((2, D_EXPERT, D_MODEL), _f32),
                pltpu.VMEM((D_MODEL, D_EXPERT), _bf16),
                pltpu.VMEM((D_MODEL, D_EXPERT), _bf16),
                pltpu.VMEM((D_EXPERT, D_MODEL), _bf16),
                pltpu.SemaphoreType.DMA((X_BUFS,)),
                pltpu.SemaphoreType.DMA((Y_BUFS,)),
                pltpu.SemaphoreType.DMA((2,)),
            ],
        ),
        out_shape=jax.ShapeDtypeStruct((N_SLOTS, D_MODEL), _f32),
        input_output_aliases={n_prefetch + 4: 0},
        compiler_params=pltpu.CompilerParams(
            dimension_semantics=("arbitrary",), vmem_limit_bytes=VMEM_LIMIT),
        name="experts",
    )(tile_expert, n_used, counts, offsets, slot_flat, x1, wg, wu, wd, y_buf)


def _combine_kernel(slot_ref, x1_ref, route_ref, g2_ref, b2_ref, y_hbm, out_p_ref, out_s_ref,
                    ybuf, sems):
    i = pl.program_id(0)
    b = i % 2

    def gather(tile, slot):
        tok0 = tile * TM

        def start(r, carry):
            for k in range(2):
                pltpu.make_async_copy(y_hbm.at[pl.ds(slot_ref[k * N_TOK + tok0 + r], 1)],
                                      ybuf.at[slot, k, pl.ds(r, 1)], sems.at[slot, k]).start(priority=k)
            return carry

        lax.fori_loop(0, TM, start, 0, unroll=16)

    @pl.when(i == 0)
    def _():
        gather(0, 0)

    @pl.when(i + 1 < pl.num_programs(0))
    def _():
        gather(i + 1, 1 - b)

    for k in range(2):
        pltpu.make_async_copy(y_hbm.at[pl.ds(0, TM)], ybuf.at[b, k], sems.at[b, k]).wait()

    route = route_ref[...]
    moe = route[:, 2:3] * ybuf[b, 0] + route[:, 3:4] * ybuf[b, 1]
    out = _layer_norm(ALPHA * x1_ref[...] + moe, g2_ref[...], b2_ref[...])

    @pl.when(i < N_PROMPT_TILES)
    def _():
        out_p_ref[...] = out

    @pl.when(i >= N_PROMPT_TILES)
    def _():
        out_s_ref[...] = out


def _combine_call(slot_flat, x1, route, g2, b2, y):
    tok = lambda i, s: (i, 0)
    const = lambda i, s: (0, 0)
    return pl.pallas_call(
        _combine_kernel,
        grid_spec=pltpu.PrefetchScalarGridSpec(
            num_scalar_prefetch=1,
            grid=(N_TILES,),
            in_specs=[
                pl.BlockSpec((TM, D_MODEL), tok),
                pl.BlockSpec((TM, LANES), tok),
                pl.BlockSpec((1, D_MODEL), const),
                pl.BlockSpec((1, D_MODEL), const),
                pl.BlockSpec(memory_space=pl.ANY),
            ],
            out_specs=[
                pl.BlockSpec((TM, D_MODEL), lambda i, s: (jnp.minimum(i, N_PROMPT_TILES - 1), 0)),
                pl.BlockSpec((TM, D_MODEL), lambda i, s: (jnp.maximum(i - N_PROMPT_TILES, 0), 0)),
            ],
            scratch_shapes=[
                pltpu.VMEM((2, 2, TM, D_MODEL), _f32),
                pltpu.SemaphoreType.DMA((2, 2)),
            ],
        ),
        out_shape=[jax.ShapeDtypeStruct((N_PROMPT, D_MODEL), _f32),
                   jax.ShapeDtypeStruct((N_SAMPLE, D_MODEL), _f32)],
        compiler_params=pltpu.CompilerParams(dimension_semantics=("arbitrary",)),
        name="combine",
    )(slot_flat, x1, route, g2, b2, y)


def kernel(x_prompt, x_sample, state_conv, state_pool, w_in, conv_w, sgu_ln_g, sgu_ln_b, sgu_w, sgu_b, pool_w, pool_scale, w_out, ln1_g, ln1_b, router_group_w, router_group_b, router_expert_w, router_expert_b, expert_w_gate, expert_w_up, expert_w_down, ln2_g, ln2_b):
    y_buf = jnp.zeros((N_SLOTS, D_MODEL), _f32)
    reps = CHUNK // DEC_SEQ
    row = lambda a: a.reshape(DEPTH, 1, -1)

    w_r = jnp.concatenate([router_group_w, router_expert_w], axis=2)
    w_r = jnp.pad(w_r, ((0, 0), (0, 0), (0, LANES - w_r.shape[2])))
    wr_hi = w_r.astype(_bf16)
    wr = jnp.concatenate([wr_hi, (w_r - wr_hi.astype(_f32)).astype(_bf16)], axis=2)
    b_r = jnp.concatenate([router_group_b, router_expert_b], axis=1)
    b_r = row(jnp.pad(b_r, ((0, 0), (0, LANES - b_r.shape[1]))))
    ws_s = jnp.tile(sgu_w[:, :, :DEC_SEQ, :DEC_SEQ], (1, 1, reps, reps))
    bias_p = jnp.swapaxes(sgu_b, 1, 2)
    bias_s = jnp.tile(bias_p[:, :DEC_SEQ], (1, reps, 1))
    zst = jnp.pad(state_conv, ((0, 0), (0, 0), (CONV_HALO - 2, 0), (0, 0)))
    pst = jnp.pad(state_pool, ((0, 0), (0, 0), (POOL_HALO - POOL_BUF, 0), (0, 0)))
    mixer_params = (w_in.astype(_bf16), w_out.astype(_bf16), conv_w, row(sgu_ln_g), row(sgu_ln_b), sgu_w,
                    ws_s, bias_p, bias_s, pool_w, row(pool_scale), row(ln1_g), row(ln1_b), wr, b_r, zst, pst)
    g2, b2 = row(ln2_g), row(ln2_b)

    ztails, ptails, zss, psamps, v_s = [], [], [], [], []
    x_src = (x_prompt.reshape(N_PROMPT, D_MODEL), x_sample.reshape(N_SAMPLE, D_MODEL))
    for l in range(DEPTH):
        x1, route, ztail, ptail, zs, psamp, vns = _mixer_call(l, x_src, mixer_params)
        ids = route[:, 0:2].astype(jnp.int32).T.reshape(ASSIGN_ROWS, LANES)
        slot, tile_expert, n_used, counts, offsets = _slots_call(ids)
        slot_flat = slot.reshape(N_ASSIGN)
        y = _experts_call(l, tile_expert.reshape(LANES), n_used.reshape(LANES)[:1], counts.reshape(LANES),
                          offsets.reshape(LANES), slot_flat, x1, expert_w_gate, expert_w_up, expert_w_down,
                          y_buf)
        x_src = (slot_flat, x1, route, y, g2, b2)
        y_buf = y
        ztails.append(ztail)
        ptails.append(ptail)
        zss.append(zs)
        psamps.append(psamp)
        v_s.append(vns)

    x_p, x_s = _combine_call(slot_flat, x1, route, g2[DEPTH - 1], b2[DEPTH - 1], y)
    y_prompt = x_p.reshape(BATCH, SEQ, D_MODEL)
    y_sample = x_s.reshape(DEC_BATCH, DEC_SEQ, D_MODEL)
    new_conv_prompt = jnp.stack(ztails)[:, :BATCH, CONV_HALO - 2:]
    new_pool_prompt = jnp.stack(ptails)[:, :BATCH, POOL_HALO - POOL_BUF:]
    new_conv_sample = jnp.stack(zss)[:, :, DEC_SEQ - 2:]
    new_pool_sample = jnp.concatenate([state_pool[:, :, DEC_SEQ:], jnp.stack(psamps)], axis=2)
    return (y_prompt, y_sample, new_conv_prompt, new_pool_prompt, new_conv_sample, new_pool_sample,
            jnp.stack(v_s))
```

```python
import functools

import jax
import jax.numpy as jnp
from jax import lax
from jax.experimental import pallas as pl
from jax.experimental.pallas import tpu as pltpu

D_MODEL = 2048
BATCH = 4
SEQ = 2048
DEPTH = 4
DEC_BATCH = 128
DEC_SEQ = 8
PAST_LEN = 16384
HEAD_DIM = 128
CONV_WIDTH = 768
POOL_WIDTH = 512
CHUNK_WIDTH = 768
CHUNK = 128
N_CHUNK_HEADS = 6
POOL_WINDOWS = (2, 4, 8, 16)
POOL_GROUP_DIM = 128
POOL_BUF = 15
IN_WIDTH = 4352
N_EXPERT_GROUPS = 4
EXPERTS_PER_GROUP = 4
N_EXPERTS = 16
D_EXPERT = 512
ALPHA = (2 * DEPTH) ** 0.25
LN_EPS = 1e-5

N_PROMPT = BATCH * SEQ
N_SAMPLE = DEC_BATCH * DEC_SEQ
N_TOK = N_PROMPT + N_SAMPLE
N_ASSIGN = 2 * N_TOK

LANES = 128
SUBLANES = 8
CONV_HALO = 8
POOL_HALO = 16

TM = 256
TILES_PER_SEQ = SEQ // TM
N_PROMPT_TILES = N_PROMPT // TM
N_SAMPLE_TILES = N_SAMPLE // TM
N_TILES = N_PROMPT_TILES + N_SAMPLE_TILES
SEQS_PER_TILE = TM // DEC_SEQ

TE = 256
N_SLOTS = N_ASSIGN + N_EXPERTS * TE
N_ETILES = N_SLOTS // TE
ASSIGN_ROWS = N_ASSIGN // LANES

CUT_B, CUT_C, CUT_H, CUT_U, CUT_V, CUT_P = 0, 768, 1536, 2304, 3072, 3840

VMEM_LIMIT = 62 * 1024 * 1024

_f32 = jnp.float32
_bf16 = jnp.bfloat16


def _dot(a, b):
    return jnp.dot(a, b, preferred_element_type=_f32)


def _layer_norm(r, g, b):
    mu = jnp.mean(r, axis=-1, keepdims=True)
    c = r - mu
    var = jnp.mean(c * c, axis=-1, keepdims=True)
    return c * lax.rsqrt(var + LN_EPS) * g + b


def _mixer_body(xb, w_in_ref, conv_w_ref, sg_ref, sb_ref, ws_ref, bias_ref, pw_ref, ps_ref,
                zext_ref, pext_ref, ybuf_ref, seq_len, pos, chunk_len):
    n_seq = TM // seq_len

    z = _dot(xb, w_in_ref[:, CUT_C:CUT_H]) * _dot(xb, w_in_ref[:, CUT_H:CUT_U])
    zext_ref[:, CONV_HALO:, :] = z.reshape(n_seq, seq_len, CONV_WIDTH)
    cw = conv_w_ref[...]
    conv = (cw[0:1, :] * zext_ref[:, CONV_HALO - 2:CONV_HALO - 2 + seq_len, :]
            + cw[1:2, :] * zext_ref[:, CONV_HALO - 1:CONV_HALO - 1 + seq_len, :]
            + cw[2:3, :] * zext_ref[:, CONV_HALO:, :])
    y_conv = _dot(xb, w_in_ref[:, CUT_B:CUT_C]) * conv.reshape(TM, CONV_WIDTH)
    ybuf_ref[:, 0:CONV_WIDTH] = y_conv.astype(_bf16)

    v = _dot(xb, w_in_ref[:, CUT_V:CUT_P])
    vn = _layer_norm(v, sg_ref[...], sb_ref[...])
    vnb = vn.astype(_bf16)
    r_i = lax.broadcasted_iota(jnp.int32, (CHUNK, CHUNK), 0)
    c_i = lax.broadcasted_iota(jnp.int32, (CHUNK, CHUNK), 1)
    shift = chunk_len.bit_length() - 1
    mask = (c_i <= r_i) & ((r_i >> shift) == (c_i >> shift))
    u = _dot(xb, w_in_ref[:, CUT_U:CUT_V])
    n_chunks = TM // CHUNK
    for h in range(N_CHUNK_HEADS):
        m_h = jnp.where(mask, ws_ref[h], 0.0).astype(_bf16)
        b_h = bias_ref[:, h:h + 1]
        cols = slice(h * HEAD_DIM, (h + 1) * HEAD_DIM)
        vn_h = jnp.concatenate([vnb[c * CHUNK:(c + 1) * CHUNK, cols] for c in range(n_chunks)], axis=1)
        mixed = _dot(m_h, vn_h)
        for c in range(n_chunks):
            rows = slice(c * CHUNK, (c + 1) * CHUNK)
            ybuf_ref[rows, CONV_WIDTH + h * HEAD_DIM:CONV_WIDTH + (h + 1) * HEAD_DIM] = (
                u[rows, cols] * (mixed[:, c * HEAD_DIM:(c + 1) * HEAD_DIM] + b_h)).astype(_bf16)

    p = _dot(xb, w_in_ref[:, CUT_P:IN_WIDTH])
    pext_ref[:, POOL_HALO:, :] = p.reshape(n_seq, seq_len, POOL_WIDTH)
    posf = (pos + 1).astype(_f32)
    for g, w in enumerate(POOL_WINDOWS):
        cols = slice(g * POOL_GROUP_DIM, (g + 1) * POOL_GROUP_DIM)
        s = pext_ref[:, POOL_HALO:, cols]
        for j in range(1, w):
            s = s + pext_ref[:, POOL_HALO - j:POOL_HALO - j + seq_len, cols]
        cnt = jnp.minimum(float(w), posf)
        diff = s.reshape(TM, POOL_GROUP_DIM) / cnt - p[:, cols]
        y_g = _dot(diff.astype(_bf16), pw_ref[g].astype(_bf16)) * ps_ref[:, cols]
        ybuf_ref[:, CONV_WIDTH + CHUNK_WIDTH + g * POOL_GROUP_DIM:
                 CONV_WIDTH + CHUNK_WIDTH + (g + 1) * POOL_GROUP_DIM] = y_g.astype(_bf16)
    return z, p, vn


def _route(x1, wr_ref, br_ref):
    x_hi = x1.astype(_bf16)
    x_lo = (x1 - x_hi.astype(_f32)).astype(_bf16)
    hi_parts = _dot(x_hi, wr_ref[...])
    logits = hi_parts[:, :LANES] + hi_parts[:, LANES:] + _dot(x_lo, wr_ref[:, :LANES]) + br_ref[...]
    lane = lax.broadcasted_iota(jnp.int32, logits.shape, 1)
    neg = -jnp.inf
    big = jnp.int32(1 << 20)

    gmask = lane < N_EXPERT_GROUPS
    gl = jnp.where(gmask, logits, neg)
    g_max = jnp.max(gl, axis=1, keepdims=True)
    g_idx = jnp.min(jnp.where(gmask & (gl == g_max), lane, big), axis=1, keepdims=True)
    g_val = 1.0 / jnp.sum(jnp.exp(gl - g_max), axis=1, keepdims=True)

    e_lane = lane - N_EXPERT_GROUPS
    emask = (e_lane >= 0) & (e_lane < N_EXPERTS) & ((e_lane >> (EXPERTS_PER_GROUP.bit_length() - 1)) == g_idx)
    el = jnp.where(emask, logits, neg)
    e_max = jnp.max(el, axis=1, keepdims=True)
    i1 = jnp.min(jnp.where(emask & (el == e_max), e_lane, big), axis=1, keepdims=True)
    e_sum = jnp.sum(jnp.exp(el - e_max), axis=1, keepdims=True)
    mask2 = emask & (e_lane != i1)
    e_second = jnp.max(jnp.where(mask2, el, neg), axis=1, keepdims=True)
    i2 = jnp.min(jnp.where(mask2 & (el == e_second), e_lane, big), axis=1, keepdims=True)
    v1 = 1.0 / e_sum
    v2 = jnp.exp(e_second - e_max) / e_sum
    tot = v1 + v2
    gate1 = g_val * (v1 / tot)
    gate2 = g_val * (v2 / tot)
    return jnp.where(lane == 0, i1.astype(_f32),
                     jnp.where(lane == 1, i2.astype(_f32),
                               jnp.where(lane == 2, gate1, jnp.where(lane == 3, gate2, 0.0))))


def _y2_row(token, choice):
    tm_shift = TM.bit_length() - 1
    return ((token >> tm_shift) << (tm_shift + 1)) + choice * TM + (token & (TM - 1))


def _y2_token(y2_row):
    tm_shift = TM.bit_length() - 1
    return ((y2_row >> (tm_shift + 1)) << tm_shift) + (y2_row & (TM - 1))


def _combine(x1_ref, route_ref, y2_ref, g2_ref, b2_ref):
    route = route_ref[...]
    moe = route[:, 2:3] * y2_ref[0:TM, :] + route[:, 3:4] * y2_ref[TM:2 * TM, :]
    return _layer_norm(ALPHA * x1_ref[...] + moe, g2_ref[...], b2_ref[...])


def _mixer_kernel(fused, *refs):
    if fused:
        x1p_ref, routep_ref, y2_ref, g2_ref, b2_ref = refs[:5]
        refs = refs[5:]
    else:
        xp_ref, xsamp_ref = refs[:2]
        refs = refs[2:]
    (w_in_ref, w_out_ref, conv_w_ref, sg_ref, sb_ref, ws_p_ref, ws_s_ref, bias_p_ref, bias_s_ref,
     pw_ref, ps_ref, g1_ref, b1_ref, wr_ref, br_ref, zst_ref, pst_ref,
     x1_ref, route_ref, ztail_ref, ptail_ref, zs_ref, psamp_ref, vns_ref,
     zext_p, pext_p, zext_s, pext_s, ybuf_ref) = refs
    i = pl.program_id(0)

    if fused:
        x = _combine(x1p_ref, routep_ref, y2_ref, g2_ref, b2_ref)
    else:
        x = jnp.where(i < N_PROMPT_TILES, xp_ref[...], xsamp_ref[...])

    xb = x.astype(_bf16)
    row = lax.broadcasted_iota(jnp.int32, (TM, 1), 0)

    @pl.when(i < N_PROMPT_TILES)
    def _prompt():
        t0 = (i % TILES_PER_SEQ) * TM

        @pl.when(t0 == 0)
        def _():
            zext_p[:, 0:CONV_HALO, :] = jnp.zeros((1, CONV_HALO, CONV_WIDTH), _f32)
            pext_p[:, 0:POOL_HALO, :] = jnp.zeros((1, POOL_HALO, POOL_WIDTH), _f32)

        z, p, _ = _mixer_body(xb, w_in_ref, conv_w_ref, sg_ref, sb_ref, ws_p_ref, bias_p_ref,
                              pw_ref, ps_ref, zext_p, pext_p, ybuf_ref, TM, t0 + row, CHUNK)
        z_last = z[TM - CONV_HALO:, :].reshape(1, CONV_HALO, CONV_WIDTH)
        p_last = p[TM - POOL_HALO:, :].reshape(1, POOL_HALO, POOL_WIDTH)
        zext_p[:, 0:CONV_HALO, :] = z_last
        pext_p[:, 0:POOL_HALO, :] = p_last
        ztail_ref[...] = z_last
        ptail_ref[...] = p_last

    @pl.when(i >= N_PROMPT_TILES)
    def _sample():
        zext_s[:, 0:CONV_HALO, :] = zst_ref[...]
        pext_s[:, 0:POOL_HALO, :] = pst_ref[...]
        z, p, vn = _mixer_body(xb, w_in_ref, conv_w_ref, sg_ref, sb_ref, ws_s_ref, bias_s_ref,
                               pw_ref, ps_ref, zext_s, pext_s, ybuf_ref, DEC_SEQ,
                               PAST_LEN + (row & (DEC_SEQ - 1)), DEC_SEQ)
        zs_ref[...] = z.reshape(SEQS_PER_TILE, DEC_SEQ, CONV_WIDTH)
        psamp_ref[...] = p.reshape(SEQS_PER_TILE, DEC_SEQ, POOL_WIDTH)
        vns_ref[...] = vn.reshape(SEQS_PER_TILE, DEC_SEQ, CHUNK_WIDTH)
        ztail_ref[...] = jnp.zeros(ztail_ref.shape, _f32)
        ptail_ref[...] = jnp.zeros(ptail_ref.shape, _f32)

    mix = _dot(ybuf_ref[...], w_out_ref[...])
    x1 = _layer_norm(ALPHA * x + mix, g1_ref[...], b1_ref[...])
    x1_ref[...] = x1
    route_ref[...] = _route(x1, wr_ref, br_ref)


def _layer_spec(layer, shape):
    nd = len(shape)
    return pl.BlockSpec((None,) + shape, lambda i: (layer,) + (0,) * nd, pipeline_mode=pl.Buffered(1))


def _mixer_call(layer, x_src, params):
    fused = len(x_src) == 5
    samp = lambda i: (jnp.maximum(i - N_PROMPT_TILES, 0), 0, 0)
    samp_l = lambda i: (layer, jnp.maximum(i - N_PROMPT_TILES, 0), 0, 0)
    tail = lambda i: (i // TILES_PER_SEQ, 0, 0)
    tok = lambda i: (i, 0)
    if fused:
        x_specs = [
            pl.BlockSpec((TM, D_MODEL), tok),
            pl.BlockSpec((TM, LANES), tok),
            pl.BlockSpec((2 * TM, D_MODEL), tok),
            _layer_spec(layer - 1, (1, D_MODEL)),
            _layer_spec(layer - 1, (1, D_MODEL)),
        ]
    else:
        x_specs = [
            pl.BlockSpec((TM, D_MODEL), lambda i: (jnp.minimum(i, N_PROMPT_TILES - 1), 0)),
            pl.BlockSpec((TM, D_MODEL), lambda i: (jnp.maximum(i - N_PROMPT_TILES, 0), 0)),
        ]
    in_specs = x_specs + [
        _layer_spec(layer, (D_MODEL, IN_WIDTH)),
        _layer_spec(layer, (D_MODEL, D_MODEL)),
        _layer_spec(layer, (3, CONV_WIDTH)),
        _layer_spec(layer, (1, CHUNK_WIDTH)),
        _layer_spec(layer, (1, CHUNK_WIDTH)),
        _layer_spec(layer, (N_CHUNK_HEADS, CHUNK, CHUNK)),
        _layer_spec(layer, (N_CHUNK_HEADS, CHUNK, CHUNK)),
        _layer_spec(layer, (CHUNK, N_CHUNK_HEADS)),
        _layer_spec(layer, (CHUNK, N_CHUNK_HEADS)),
        _layer_spec(layer, (4, POOL_GROUP_DIM, POOL_GROUP_DIM)),
        _layer_spec(layer, (1, POOL_WIDTH)),
        _layer_spec(layer, (1, D_MODEL)),
        _layer_spec(layer, (1, D_MODEL)),
        _layer_spec(layer, (D_MODEL, 2 * LANES)),
        _layer_spec(layer, (1, LANES)),
        pl.BlockSpec((None, SEQS_PER_TILE, CONV_HALO, CONV_WIDTH), samp_l),
        pl.BlockSpec((None, SEQS_PER_TILE, POOL_HALO, POOL_WIDTH), samp_l),
    ]
    out_specs = [
        pl.BlockSpec((TM, D_MODEL), tok),
        pl.BlockSpec((TM, LANES), tok),
        pl.BlockSpec((1, CONV_HALO, CONV_WIDTH), tail),
        pl.BlockSpec((1, POOL_HALO, POOL_WIDTH), tail),
        pl.BlockSpec((SEQS_PER_TILE, DEC_SEQ, CONV_WIDTH), samp),
        pl.BlockSpec((SEQS_PER_TILE, DEC_SEQ, POOL_WIDTH), samp),
        pl.BlockSpec((SEQS_PER_TILE, DEC_SEQ, CHUNK_WIDTH), samp),
    ]
    out_shape = [
        jax.ShapeDtypeStruct((N_TOK, D_MODEL), _f32),
        jax.ShapeDtypeStruct((N_TOK, LANES), _f32),
        jax.ShapeDtypeStruct((BATCH + 1, CONV_HALO, CONV_WIDTH), _f32),
        jax.ShapeDtypeStruct((BATCH + 1, POOL_HALO, POOL_WIDTH), _f32),
        jax.ShapeDtypeStruct((DEC_BATCH, DEC_SEQ, CONV_WIDTH), _f32),
        jax.ShapeDtypeStruct((DEC_BATCH, DEC_SEQ, POOL_WIDTH), _f32),
        jax.ShapeDtypeStruct((DEC_BATCH, DEC_SEQ, CHUNK_WIDTH), _f32),
    ]
    scratch = [
        pltpu.VMEM((1, CONV_HALO + TM, CONV_WIDTH), _f32),
        pltpu.VMEM((1, POOL_HALO + TM, POOL_WIDTH), _f32),
        pltpu.VMEM((SEQS_PER_TILE, CONV_HALO + DEC_SEQ, CONV_WIDTH), _f32),
        pltpu.VMEM((SEQS_PER_TILE, POOL_HALO + DEC_SEQ, POOL_WIDTH), _f32),
        pltpu.VMEM((TM, D_MODEL), _bf16),
    ]
    return pl.pallas_call(
        functools.partial(_mixer_kernel, fused),
        grid=(N_TILES,),
        in_specs=in_specs,
        out_specs=out_specs,
        scratch_shapes=scratch,
        out_shape=out_shape,
        compiler_params=pltpu.CompilerParams(
            dimension_semantics=("arbitrary",), vmem_limit_bytes=VMEM_LIMIT),
        name="mixer",
    )(*x_src, *params)


def _slots_kernel(ids_ref, slot_ref, tile_expert_ref, n_used_ref, count_ref, offset_ref):
    ids = ids_ref[...]
    lane = lax.broadcasted_iota(jnp.int32, (1, LANES), 1)
    counts = jnp.zeros((1, LANES), _f32)
    offsets = jnp.zeros((1, LANES), _f32)
    r_i = lax.broadcasted_iota(jnp.int32, (LANES, LANES), 0)
    c_i = lax.broadcasted_iota(jnp.int32, (LANES, LANES), 1)
    upper = (r_i < c_i).astype(_bf16)
    rr = lax.broadcasted_iota(jnp.int32, (ASSIGN_ROWS, ASSIGN_ROWS), 0)
    rc = lax.broadcasted_iota(jnp.int32, (ASSIGN_ROWS, ASSIGN_ROWS), 1)
    lower = (rc < rr).astype(_bf16)
    tile_row0 = (lax.broadcasted_iota(jnp.int32, (1, LANES), 1) * TE).astype(_f32)

    slot = jnp.zeros((ASSIGN_ROWS, LANES), _f32)
    tile_expert = jnp.zeros((1, LANES), _f32)
    off = jnp.zeros((1, 1), _f32)
    for e in range(N_EXPERTS):
        m = (ids == e).astype(_f32)
        within = _dot(m.astype(_bf16), upper)
        rowsum = jnp.sum(m, axis=1, keepdims=True)
        rowpre = _dot(lower, jnp.broadcast_to(rowsum, (ASSIGN_ROWS, LANES)).astype(_bf16))
        cnt = jnp.sum(rowsum, axis=0, keepdims=True)
        padded = jnp.floor((cnt + (TE - 1)) * (1.0 / TE)) * TE
        slot = slot + m * (off + rowpre + within)
        in_seg = (tile_row0 >= off) & (tile_row0 < off + padded)
        tile_expert = tile_expert + jnp.where(in_seg, float(e), 0.0)
        counts = counts + jnp.where(lane == e, cnt, 0.0)
        offsets = offsets + jnp.where(lane == e, off, 0.0)
        off = off + padded
    tile_expert = jnp.where(tile_row0 >= off, float(N_EXPERTS - 1), tile_expert)
    slot_ref[...] = slot.astype(jnp.int32)
    tile_expert_ref[...] = tile_expert.astype(jnp.int32)
    n_used_ref[...] = jnp.broadcast_to(off * (1.0 / TE), (1, LANES)).astype(jnp.int32)
    count_ref[...] = counts.astype(jnp.int32)
    offset_ref[...] = offsets.astype(jnp.int32)


def _slots_call(ids):
    return pl.pallas_call(
        _slots_kernel,
        out_shape=[jax.ShapeDtypeStruct((ASSIGN_ROWS, LANES), jnp.int32)]
        + [jax.ShapeDtypeStruct((1, LANES), jnp.int32)] * 4,
        name="slots",
    )(ids)


X_BUFS = 3
Y_BUFS = 2
Y2_ROWS = N_ASSIGN + N_EXPERTS * TE


def _experts_kernel(layer, te_ref, nu_ref, count_ref, offset_ref, slot_ref,
                    x1_hbm, wg_hbm, wu_hbm, wd_hbm, y2_in_hbm, y2_hbm,
                    dst_of_slot, xbuf, ybuf, wg_st, wu_st, wd_st, wg_b, wu_b, wd_b, sem_x, sem_y, sem_w):
    del y2_in_hbm
    n_used = nu_ref[0]
    tile_shift = TE.bit_length() - 1

    for k in range(2):
        def fill(t, carry, k=k):
            dst_of_slot[slot_ref[k * N_TOK + t]] = _y2_row(t, k)
            return carry
        lax.fori_loop(0, N_TOK, fill, 0, unroll=16)
    dump_row = jnp.int32(N_ASSIGN)
    for e in range(N_EXPERTS):
        first_pad = offset_ref[e] + count_ref[e]
        seg_end = offset_ref[e] + (((count_ref[e] + (TE - 1)) >> tile_shift) << tile_shift)

        def pad(s, row):
            dst_of_slot[s] = row
            return row + 1
        dump_row = lax.fori_loop(first_pad, seg_end, pad, dump_row)

    def gather(tile, buf):
        base = tile * TE
        for r in range(TE):
            token = jnp.minimum(_y2_token(dst_of_slot[base + r]), N_TOK - 1)
            pltpu.make_async_copy(x1_hbm.at[pl.ds(token, 1)], xbuf.at[buf, pl.ds(r, 1)],
                                  sem_x.at[buf]).start(priority=r % 2)

    def scatter(tile, buf):
        base = tile * TE
        for r in range(TE):
            pltpu.make_async_copy(ybuf.at[buf, pl.ds(r, 1)], y2_hbm.at[pl.ds(dst_of_slot[base + r], 1)],
                                  sem_y.at[buf]).start(priority=r % 2)

    def wait_scatter(buf):
        pltpu.make_async_copy(ybuf.at[buf], y2_hbm.at[pl.ds(0, TE)], sem_y.at[buf]).wait()

    def weight_copies(expert, ws):
        return [pltpu.make_async_copy(src.at[layer, expert], dst.at[ws], sem_w.at[ws])
                for src, dst in ((wg_hbm, wg_st), (wu_hbm, wu_st), (wd_hbm, wd_st))]

    for c in weight_copies(te_ref[0], 0):
        c.start()
    gather(0, 0)

    @pl.when(n_used > 1)
    def _():
        gather(1, 1)

    def tile_step(j, seg):
        expert = te_ref[j]
        first = (j == 0) | (te_ref[jnp.maximum(j - 1, 0)] != expert)

        @pl.when(j + (X_BUFS - 1) < n_used)
        def _():
            gather(j + (X_BUFS - 1), (j + (X_BUFS - 1)) % X_BUFS)

        @pl.when(first)
        def _():
            ws = seg % 2
            for c in weight_copies(expert, ws):
                c.wait()
            wg_b[...] = wg_st[ws].astype(_bf16)
            wu_b[...] = wu_st[ws].astype(_bf16)
            wd_b[...] = wd_st[ws].astype(_bf16)
            next_first = j + ((count_ref[expert] + (TE - 1)) >> tile_shift)

            @pl.when(next_first < n_used)
            def _():
                for c in weight_copies(te_ref[next_first], 1 - ws):
                    c.start()

        xslot = j % X_BUFS
        pltpu.make_async_copy(x1_hbm.at[pl.ds(0, TE)], xbuf.at[xslot], sem_x.at[xslot]).wait()
        xb = xbuf[xslot].astype(_bf16)
        hg = _dot(xb, wg_b[...])
        hu = _dot(xb, wu_b[...])
        a = hg / (1.0 + jnp.exp(-hg)) * hu
        y = _dot(a.astype(_bf16), wd_b[...])

        yslot = j % Y_BUFS

        @pl.when(j >= Y_BUFS)
        def _():
            wait_scatter(yslot)

        ybuf[yslot] = y
        scatter(j, yslot)
        return seg + first.astype(jnp.int32)

    lax.fori_loop(0, n_used, tile_step, jnp.int32(0))

    for back in range(1, Y_BUFS + 1):
        @pl.when(n_used >= back)
        def _(back=back):
            wait_scatter((n_used - back) % Y_BUFS)


def _experts_call(layer, tile_expert, n_used, counts, offsets, slot_flat, x1, wg, wu, wd, y2_buf):
    any_spec = pl.BlockSpec(memory_space=pl.ANY)
    n_prefetch = 5
    return pl.pallas_call(
        functools.partial(_experts_kernel, layer),
        grid_spec=pltpu.PrefetchScalarGridSpec(
            num_scalar_prefetch=n_prefetch,
            grid=(1,),
            in_specs=[any_spec] * 5,
            out_specs=any_spec,
            scratch_shapes=[
                pltpu.SMEM((N_SLOTS,), jnp.int32),
                pltpu.VMEM((X_BUFS, TE, D_MODEL), _f32),
                pltpu.VMEM((Y_BUFS, TE, D_MODEL), _f32),
                pltpu.VMEM((2, D_MODEL, D_EXPERT), _f32),
                pltpu.VMEM((2, D_MODEL, D_EXPERT), _f32),
                pltpu.VMEM((2, D_EXPERT, D_MODEL), _f32),
                pltpu.VMEM((D_MODEL, D_EXPERT), _bf16),
                pltpu.VMEM((D_MODEL, D_EXPERT), _bf16),
                pltpu.VMEM((D_EXPERT, D_MODEL), _bf16),
                pltpu.SemaphoreType.DMA((X_BUFS,)),
                pltpu.SemaphoreType.DMA((Y_BUFS,)),
                pltpu.SemaphoreType.DMA((2,)),
            ],
        ),
        out_shape=jax.ShapeDtypeStruct((Y2_ROWS, D_MODEL), _f32),
        input_output_aliases={n_prefetch + 4: 0},
        compiler_params=pltpu.CompilerParams(
            dimension_semantics=("arbitrary",), vmem_limit_bytes=VMEM_LIMIT),
        name="experts",
    )(tile_expert, n_used, counts, offsets, slot_flat, x1, wg, wu, wd, y2_buf)


def _combine_kernel(x1_ref, route_ref, y2_ref, g2_ref, b2_ref, out_p_ref, out_s_ref):
    i = pl.program_id(0)
    out = _combine(x1_ref, route_ref, y2_ref, g2_ref, b2_ref)

    @pl.when(i < N_PROMPT_TILES)
    def _():
        out_p_ref[...] = out

    @pl.when(i >= N_PROMPT_TILES)
    def _():
        out_s_ref[...] = out


def _combine_call(x1, route, y2, g2, b2):
    tok = lambda i: (i, 0)
    const = lambda i: (0, 0)
    return pl.pallas_call(
        _combine_kernel,
        grid=(N_TILES,),
        in_specs=[
            pl.BlockSpec((TM, D_MODEL), tok),
            pl.BlockSpec((TM, LANES), tok),
            pl.BlockSpec((2 * TM, D_MODEL), tok),
            pl.BlockSpec((1, D_MODEL), const),
            pl.BlockSpec((1, D_MODEL), const),
        ],
        out_specs=[
            pl.BlockSpec((TM, D_MODEL), lambda i: (jnp.minimum(i, N_PROMPT_TILES - 1), 0)),
            pl.BlockSpec((TM, D_MODEL), lambda i: (jnp.maximum(i - N_PROMPT_TILES, 0), 0)),
        ],
        out_shape=[jax.ShapeDtypeStruct((N_PROMPT, D_MODEL), _f32),
                   jax.ShapeDtypeStruct((N_SAMPLE, D_MODEL), _f32)],
        compiler_params=pltpu.CompilerParams(dimension_semantics=("arbitrary",)),
        name="combine",
    )(x1, route, y2, g2, b2)


def kernel(x_prompt, x_sample, state_conv, state_pool, w_in, conv_w, sgu_ln_g, sgu_ln_b, sgu_w, sgu_b, pool_w, pool_scale, w_out, ln1_g, ln1_b, router_group_w, router_group_b, router_expert_w, router_expert_b, expert_w_gate, expert_w_up, expert_w_down, ln2_g, ln2_b):
    y2_buf = jnp.zeros((Y2_ROWS, D_MODEL), _f32)
    reps = CHUNK // DEC_SEQ
    row = lambda a: a.reshape(DEPTH, 1, -1)

    w_r = jnp.concatenate([router_group_w, router_expert_w], axis=2)
    w_r = jnp.pad(w_r, ((0, 0), (0, 0), (0, LANES - w_r.shape[2])))
    wr_hi = w_r.astype(_bf16)
    wr = jnp.concatenate([wr_hi, (w_r - wr_hi.astype(_f32)).astype(_bf16)], axis=2)
    b_r = jnp.concatenate([router_group_b, router_expert_b], axis=1)
    b_r = row(jnp.pad(b_r, ((0, 0), (0, LANES - b_r.shape[1]))))
    ws_s = jnp.tile(sgu_w[:, :, :DEC_SEQ, :DEC_SEQ], (1, 1, reps, reps))
    bias_p = jnp.swapaxes(sgu_b, 1, 2)
    bias_s = jnp.tile(bias_p[:, :DEC_SEQ], (1, reps, 1))
    zst = jnp.pad(state_conv, ((0, 0), (0, 0), (CONV_HALO - 2, 0), (0, 0)))
    pst = jnp.pad(state_pool, ((0, 0), (0, 0), (POOL_HALO - POOL_BUF, 0), (0, 0)))
    mixer_params = (w_in.astype(_bf16), w_out.astype(_bf16), conv_w, row(sgu_ln_g), row(sgu_ln_b), sgu_w,
                    ws_s, bias_p, bias_s, pool_w, row(pool_scale), row(ln1_g), row(ln1_b), wr, b_r, zst, pst)
    g2, b2 = row(ln2_g), row(ln2_b)

    ztails, ptails, zss, psamps, v_s = [], [], [], [], []
    x_src = (x_prompt.reshape(N_PROMPT, D_MODEL), x_sample.reshape(N_SAMPLE, D_MODEL))
    for l in range(DEPTH):
        x1, route, ztail, ptail, zs, psamp, vns = _mixer_call(l, x_src, mixer_params)
        ids = route[:, 0:2].astype(jnp.int32).T.reshape(ASSIGN_ROWS, LANES)
        slot, tile_expert, n_used, counts, offsets = _slots_call(ids)
        y2 = _experts_call(l, tile_expert.reshape(LANES), n_used.reshape(LANES)[:1], counts.reshape(LANES),
                           offsets.reshape(LANES), slot.reshape(N_ASSIGN), x1,
                           expert_w_gate, expert_w_up, expert_w_down, y2_buf)
        x_src = (x1, route, y2, g2, b2)
        y2_buf = y2
        ztails.append(ztail)
        ptails.append(ptail)
        zss.append(zs)
        psamps.append(psamp)
        v_s.append(vns)

    x_p, x_s = _combine_call(x1, route, y2, g2[DEPTH - 1], b2[DEPTH - 1])
    y_prompt = x_p.reshape(BATCH, SEQ, D_MODEL)
    y_sample = x_s.reshape(DEC_BATCH, DEC_SEQ, D_MODEL)
    new_conv_prompt = jnp.stack(ztails)[:, :BATCH, CONV_HALO - 2:]
    new_pool_prompt = jnp.stack(ptails)[:, :BATCH, POOL_HALO - POOL_BUF:]
    new_conv_sample = jnp.stack(zss)[:, :, DEC_SEQ - 2:]
    new_pool_sample = jnp.concatenate([state_pool[:, :, DEC_SEQ:], jnp.stack(psamps)], axis=2)
    return (y_prompt, y_sample, new_conv_prompt, new_pool_prompt, new_conv_sample, new_pool_sample,
            jnp.stack(v_s))
```

```python
import functools

import jax
import jax.numpy as jnp
from jax import lax
from jax.experimental import pallas as pl
from jax.experimental.pallas import tpu as pltpu

D_MODEL = 2048
BATCH = 4
SEQ = 2048
DEPTH = 4
DEC_BATCH = 128
DEC_SEQ = 8
PAST_LEN = 16384
HEAD_DIM = 128
CONV_WIDTH = 768
POOL_WIDTH = 512
CHUNK_WIDTH = 768
CHUNK = 128
N_CHUNK_HEADS = 6
POOL_WINDOWS = (2, 4, 8, 16)
POOL_GROUP_DIM = 128
POOL_BUF = 15
IN_WIDTH = 4352
N_EXPERT_GROUPS = 4
EXPERTS_PER_GROUP = 4
N_EXPERTS = 16
D_EXPERT = 512
ALPHA = (2 * DEPTH) ** 0.25
LN_EPS = 1e-5

N_PROMPT = BATCH * SEQ
N_SAMPLE = DEC_BATCH * DEC_SEQ
N_TOK = N_PROMPT + N_SAMPLE
N_ASSIGN = 2 * N_TOK

LANES = 128
SUBLANES = 8
CONV_HALO = 8
POOL_HALO = 16

TM = 256
TILES_PER_SEQ = SEQ // TM
N_PROMPT_TILES = N_PROMPT // TM
N_SAMPLE_TILES = N_SAMPLE // TM
N_TILES = N_PROMPT_TILES + N_SAMPLE_TILES
SEQS_PER_TILE = TM // DEC_SEQ

TE = 256
N_SLOTS = N_ASSIGN + N_EXPERTS * TE
N_ETILES = N_SLOTS // TE
ASSIGN_ROWS = N_ASSIGN // LANES

CUT_B, CUT_C, CUT_H, CUT_U, CUT_V, CUT_P = 0, 768, 1536, 2304, 3072, 3840

VMEM_LIMIT = 62 * 1024 * 1024

_f32 = jnp.float32
_bf16 = jnp.bfloat16


def _dot(a, b):
    return jnp.dot(a, b, preferred_element_type=_f32)


def _layer_norm(r, g, b):
    mu = jnp.mean(r, axis=-1, keepdims=True)
    c = r - mu
    var = jnp.mean(c * c, axis=-1, keepdims=True)
    return c * lax.rsqrt(var + LN_EPS) * g + b


class _RowGather:
    def __init__(self, start_row, first_row):
        self._start_row = start_row
        self._done = first_row

    def upto(self, row_end):
        for r in range(self._done, row_end):
            self._start_row(r)
        self._done = max(self._done, row_end)


class _NoGather:
    def upto(self, row_end):
        del row_end


BRANCH_GATHER_ROWS = 192


def _mixer_body(xb, w_in_ref, conv_w_ref, sg_ref, sb_ref, ws_ref, bias_ref, pw_ref, ps_ref,
                zext_ref, pext_ref, ybuf_ref, seq_len, pos, chunk_len, gather):
    n_seq = TM // seq_len

    z = _dot(xb, w_in_ref[:, CUT_C:CUT_H]) * _dot(xb, w_in_ref[:, CUT_H:CUT_U])
    zext_ref[:, CONV_HALO:, :] = z.reshape(n_seq, seq_len, CONV_WIDTH)
    gather.upto(16)
    cw = conv_w_ref[...]
    conv = (cw[0:1, :] * zext_ref[:, CONV_HALO - 2:CONV_HALO - 2 + seq_len, :]
            + cw[1:2, :] * zext_ref[:, CONV_HALO - 1:CONV_HALO - 1 + seq_len, :]
            + cw[2:3, :] * zext_ref[:, CONV_HALO:, :])
    y_conv = _dot(xb, w_in_ref[:, CUT_B:CUT_C]) * conv.reshape(TM, CONV_WIDTH)
    ybuf_ref[:, 0:CONV_WIDTH] = y_conv.astype(_bf16)
    gather.upto(32)

    v = _dot(xb, w_in_ref[:, CUT_V:CUT_P])
    vn = _layer_norm(v, sg_ref[...], sb_ref[...])
    vnb = vn.astype(_bf16)
    gather.upto(48)
    r_i = lax.broadcasted_iota(jnp.int32, (CHUNK, CHUNK), 0)
    c_i = lax.broadcasted_iota(jnp.int32, (CHUNK, CHUNK), 1)
    shift = chunk_len.bit_length() - 1
    mask = (c_i <= r_i) & ((r_i >> shift) == (c_i >> shift))
    u = _dot(xb, w_in_ref[:, CUT_U:CUT_V])
    gather.upto(64)
    n_chunks = TM // CHUNK
    for h in range(N_CHUNK_HEADS):
        m_h = jnp.where(mask, ws_ref[h], 0.0).astype(_bf16)
        b_h = bias_ref[:, h:h + 1]
        cols = slice(h * HEAD_DIM, (h + 1) * HEAD_DIM)
        vn_h = jnp.concatenate([vnb[c * CHUNK:(c + 1) * CHUNK, cols] for c in range(n_chunks)], axis=1)
        mixed = _dot(m_h, vn_h)
        for c in range(n_chunks):
            rows = slice(c * CHUNK, (c + 1) * CHUNK)
            ybuf_ref[rows, CONV_WIDTH + h * HEAD_DIM:CONV_WIDTH + (h + 1) * HEAD_DIM] = (
                u[rows, cols] * (mixed[:, c * HEAD_DIM:(c + 1) * HEAD_DIM] + b_h)).astype(_bf16)
        gather.upto(64 + 8 * (h + 1))

    p = _dot(xb, w_in_ref[:, CUT_P:IN_WIDTH])
    pext_ref[:, POOL_HALO:, :] = p.reshape(n_seq, seq_len, POOL_WIDTH)
    gather.upto(128)
    posf = (pos + 1).astype(_f32)
    for g, w in enumerate(POOL_WINDOWS):
        cols = slice(g * POOL_GROUP_DIM, (g + 1) * POOL_GROUP_DIM)
        s = pext_ref[:, POOL_HALO:, cols]
        for j in range(1, w):
            s = s + pext_ref[:, POOL_HALO - j:POOL_HALO - j + seq_len, cols]
        cnt = jnp.minimum(float(w), posf)
        diff = s.reshape(TM, POOL_GROUP_DIM) / cnt - p[:, cols]
        y_g = _dot(diff.astype(_bf16), pw_ref[g].astype(_bf16)) * ps_ref[:, cols]
        ybuf_ref[:, CONV_WIDTH + CHUNK_WIDTH + g * POOL_GROUP_DIM:
                 CONV_WIDTH + CHUNK_WIDTH + (g + 1) * POOL_GROUP_DIM] = y_g.astype(_bf16)
        gather.upto(128 + 16 * (g + 1))
    assert 128 + 16 * len(POOL_WINDOWS) == BRANCH_GATHER_ROWS
    return z, p, vn


def _route(x1, wr_ref, br_ref):
    x_hi = x1.astype(_bf16)
    x_lo = (x1 - x_hi.astype(_f32)).astype(_bf16)
    hi_parts = _dot(x_hi, wr_ref[...])
    logits = hi_parts[:, :LANES] + hi_parts[:, LANES:] + _dot(x_lo, wr_ref[:, :LANES]) + br_ref[...]
    lane = lax.broadcasted_iota(jnp.int32, logits.shape, 1)
    neg = -jnp.inf
    big = jnp.int32(1 << 20)

    gmask = lane < N_EXPERT_GROUPS
    gl = jnp.where(gmask, logits, neg)
    g_max = jnp.max(gl, axis=1, keepdims=True)
    g_idx = jnp.min(jnp.where(gmask & (gl == g_max), lane, big), axis=1, keepdims=True)
    g_val = 1.0 / jnp.sum(jnp.exp(gl - g_max), axis=1, keepdims=True)

    e_lane = lane - N_EXPERT_GROUPS
    emask = (e_lane >= 0) & (e_lane < N_EXPERTS) & ((e_lane >> (EXPERTS_PER_GROUP.bit_length() - 1)) == g_idx)
    el = jnp.where(emask, logits, neg)
    e_max = jnp.max(el, axis=1, keepdims=True)
    i1 = jnp.min(jnp.where(emask & (el == e_max), e_lane, big), axis=1, keepdims=True)
    e_sum = jnp.sum(jnp.exp(el - e_max), axis=1, keepdims=True)
    mask2 = emask & (e_lane != i1)
    e_second = jnp.max(jnp.where(mask2, el, neg), axis=1, keepdims=True)
    i2 = jnp.min(jnp.where(mask2 & (el == e_second), e_lane, big), axis=1, keepdims=True)
    v1 = 1.0 / e_sum
    v2 = jnp.exp(e_second - e_max) / e_sum
    tot = v1 + v2
    gate1 = g_val * (v1 / tot)
    gate2 = g_val * (v2 / tot)
    return jnp.where(lane == 0, i1.astype(_f32),
                     jnp.where(lane == 1, i2.astype(_f32),
                               jnp.where(lane == 2, gate1, jnp.where(lane == 3, gate2, 0.0))))


N_MIXER_COMMON_REFS = 29


def _mixer_kernel(fused, *refs):
    if fused:
        slot_ref, x1p_ref, routep_ref, y_hbm, g2_ref, b2_ref = refs[:6]
        refs = refs[6:]
    else:
        xp_ref, xsamp_ref = refs[:2]
        refs = refs[2:]
    (w_in_ref, w_out_ref, conv_w_ref, sg_ref, sb_ref, ws_p_ref, ws_s_ref, bias_p_ref, bias_s_ref,
     pw_ref, ps_ref, g1_ref, b1_ref, wr_ref, br_ref, zst_ref, pst_ref,
     x1_ref, route_ref, ztail_ref, ptail_ref, zs_ref, psamp_ref, vns_ref,
     zext_p, pext_p, zext_s, pext_s, ybuf_ref) = refs[:N_MIXER_COMMON_REFS]
    i = pl.program_id(0)
    last = pl.num_programs(0) - 1

    if fused:
        gbuf, sems = refs[N_MIXER_COMMON_REFS:]
        b = i % 2

        def row_starter(tile, half):
            tok0 = tile * TM

            def start_row(r):
                for k in range(2):
                    pltpu.make_async_copy(y_hbm.at[pl.ds(slot_ref[k * N_TOK + tok0 + r], 1)],
                                          gbuf.at[half, k, pl.ds(r, 1)], sems.at[half, k]).start(priority=k)
            return start_row

        def wait_rows(half):
            for k in range(2):
                pltpu.make_async_copy(y_hbm.at[pl.ds(0, TM)], gbuf.at[half, k], sems.at[half, k]).wait()

        @pl.when(i == 0)
        def _():
            first = row_starter(0, 0)

            def body(r, carry):
                first(r)
                return carry

            lax.fori_loop(0, TM, body, 0, unroll=16)

        wait_rows(b)
        routep = routep_ref[...]
        moe = routep[:, 2:3] * gbuf[b, 0] + routep[:, 3:4] * gbuf[b, 1]
        x = _layer_norm(ALPHA * x1p_ref[...] + moe, g2_ref[...], b2_ref[...])
        next_rows = row_starter(jnp.minimum(i + 1, last), 1 - b)
        new_gather = lambda first_row: _RowGather(next_rows, first_row)
    else:
        x = jnp.where(i < N_PROMPT_TILES, xp_ref[...], xsamp_ref[...])
        new_gather = lambda first_row: _NoGather()

    xb = x.astype(_bf16)
    row = lax.broadcasted_iota(jnp.int32, (TM, 1), 0)

    @pl.when(i < N_PROMPT_TILES)
    def _prompt():
        t0 = (i % TILES_PER_SEQ) * TM

        @pl.when(t0 == 0)
        def _():
            zext_p[:, 0:CONV_HALO, :] = jnp.zeros((1, CONV_HALO, CONV_WIDTH), _f32)
            pext_p[:, 0:POOL_HALO, :] = jnp.zeros((1, POOL_HALO, POOL_WIDTH), _f32)

        z, p, _ = _mixer_body(xb, w_in_ref, conv_w_ref, sg_ref, sb_ref, ws_p_ref, bias_p_ref,
                              pw_ref, ps_ref, zext_p, pext_p, ybuf_ref, TM, t0 + row, CHUNK, new_gather(0))
        z_last = z[TM - CONV_HALO:, :].reshape(1, CONV_HALO, CONV_WIDTH)
        p_last = p[TM - POOL_HALO:, :].reshape(1, POOL_HALO, POOL_WIDTH)
        zext_p[:, 0:CONV_HALO, :] = z_last
        pext_p[:, 0:POOL_HALO, :] = p_last
        ztail_ref[...] = z_last
        ptail_ref[...] = p_last

    @pl.when(i >= N_PROMPT_TILES)
    def _sample():
        zext_s[:, 0:CONV_HALO, :] = zst_ref[...]
        pext_s[:, 0:POOL_HALO, :] = pst_ref[...]
        z, p, vn = _mixer_body(xb, w_in_ref, conv_w_ref, sg_ref, sb_ref, ws_s_ref, bias_s_ref,
                               pw_ref, ps_ref, zext_s, pext_s, ybuf_ref, DEC_SEQ,
                               PAST_LEN + (row & (DEC_SEQ - 1)), DEC_SEQ, new_gather(0))
        zs_ref[...] = z.reshape(SEQS_PER_TILE, DEC_SEQ, CONV_WIDTH)
        psamp_ref[...] = p.reshape(SEQS_PER_TILE, DEC_SEQ, POOL_WIDTH)
        vns_ref[...] = vn.reshape(SEQS_PER_TILE, DEC_SEQ, CHUNK_WIDTH)
        ztail_ref[...] = jnp.zeros(ztail_ref.shape, _f32)
        ptail_ref[...] = jnp.zeros(ptail_ref.shape, _f32)

    tail_gather = new_gather(BRANCH_GATHER_ROWS)
    mix = _dot(ybuf_ref[...], w_out_ref[...])
    tail_gather.upto(BRANCH_GATHER_ROWS + (TM - BRANCH_GATHER_ROWS) // 2)
    x1 = _layer_norm(ALPHA * x + mix, g1_ref[...], b1_ref[...])
    x1_ref[...] = x1
    tail_gather.upto(TM)
    route_ref[...] = _route(x1, wr_ref, br_ref)

    if fused:
        @pl.when(i == last)
        def _():
            wait_rows(1 - b)


def _layer_spec(layer, shape):
    nd = len(shape)
    return pl.BlockSpec((None,) + shape, lambda i, *_: (layer,) + (0,) * nd, pipeline_mode=pl.Buffered(1))


def _mixer_call(layer, x_src, params):
    fused = len(x_src) == 6
    samp = lambda i, *_: (jnp.maximum(i - N_PROMPT_TILES, 0), 0, 0)
    samp_l = lambda i, *_: (layer, jnp.maximum(i - N_PROMPT_TILES, 0), 0, 0)
    tail = lambda i, *_: (i // TILES_PER_SEQ, 0, 0)
    tok = lambda i, *_: (i, 0)
    if fused:
        x_specs = [
            pl.BlockSpec((TM, D_MODEL), tok),
            pl.BlockSpec((TM, LANES), tok),
            pl.BlockSpec(memory_space=pl.ANY),
            _layer_spec(layer - 1, (1, D_MODEL)),
            _layer_spec(layer - 1, (1, D_MODEL)),
        ]
    else:
        x_specs = [
            pl.BlockSpec((TM, D_MODEL), lambda i, *_: (jnp.minimum(i, N_PROMPT_TILES - 1), 0)),
            pl.BlockSpec((TM, D_MODEL), lambda i, *_: (jnp.maximum(i - N_PROMPT_TILES, 0), 0)),
        ]
    in_specs = x_specs + [
        _layer_spec(layer, (D_MODEL, IN_WIDTH)),
        _layer_spec(layer, (D_MODEL, D_MODEL)),
        _layer_spec(layer, (3, CONV_WIDTH)),
        _layer_spec(layer, (1, CHUNK_WIDTH)),
        _layer_spec(layer, (1, CHUNK_WIDTH)),
        _layer_spec(layer, (N_CHUNK_HEADS, CHUNK, CHUNK)),
        _layer_spec(layer, (N_CHUNK_HEADS, CHUNK, CHUNK)),
        _layer_spec(layer, (CHUNK, N_CHUNK_HEADS)),
        _layer_spec(layer, (CHUNK, N_CHUNK_HEADS)),
        _layer_spec(layer, (4, POOL_GROUP_DIM, POOL_GROUP_DIM)),
        _layer_spec(layer, (1, POOL_WIDTH)),
        _layer_spec(layer, (1, D_MODEL)),
        _layer_spec(layer, (1, D_MODEL)),
        _layer_spec(layer, (D_MODEL, 2 * LANES)),
        _layer_spec(layer, (1, LANES)),
        pl.BlockSpec((None, SEQS_PER_TILE, CONV_HALO, CONV_WIDTH), samp_l),
        pl.BlockSpec((None, SEQS_PER_TILE, POOL_HALO, POOL_WIDTH), samp_l),
    ]
    out_specs = [
        pl.BlockSpec((TM, D_MODEL), tok),
        pl.BlockSpec((TM, LANES), tok),
        pl.BlockSpec((1, CONV_HALO, CONV_WIDTH), tail),
        pl.BlockSpec((1, POOL_HALO, POOL_WIDTH), tail),
        pl.BlockSpec((SEQS_PER_TILE, DEC_SEQ, CONV_WIDTH), samp),
        pl.BlockSpec((SEQS_PER_TILE, DEC_SEQ, POOL_WIDTH), samp),
        pl.BlockSpec((SEQS_PER_TILE, DEC_SEQ, CHUNK_WIDTH), samp),
    ]
    out_shape = [
        jax.ShapeDtypeStruct((N_TOK, D_MODEL), _f32),
        jax.ShapeDtypeStruct((N_TOK, LANES), _f32),
        jax.ShapeDtypeStruct((BATCH + 1, CONV_HALO, CONV_WIDTH), _f32),
        jax.ShapeDtypeStruct((BATCH + 1, POOL_HALO, POOL_WIDTH), _f32),
        jax.ShapeDtypeStruct((DEC_BATCH, DEC_SEQ, CONV_WIDTH), _f32),
        jax.ShapeDtypeStruct((DEC_BATCH, DEC_SEQ, POOL_WIDTH), _f32),
        jax.ShapeDtypeStruct((DEC_BATCH, DEC_SEQ, CHUNK_WIDTH), _f32),
    ]
    scratch = [
        pltpu.VMEM((1, CONV_HALO + TM, CONV_WIDTH), _f32),
        pltpu.VMEM((1, POOL_HALO + TM, POOL_WIDTH), _f32),
        pltpu.VMEM((SEQS_PER_TILE, CONV_HALO + DEC_SEQ, CONV_WIDTH), _f32),
        pltpu.VMEM((SEQS_PER_TILE, POOL_HALO + DEC_SEQ, POOL_WIDTH), _f32),
        pltpu.VMEM((TM, D_MODEL), _bf16),
    ]
    assert len(in_specs) - len(x_specs) + len(out_specs) + len(scratch) == N_MIXER_COMMON_REFS
    if fused:
        scratch += [pltpu.VMEM((2, 2, TM, D_MODEL), _f32), pltpu.SemaphoreType.DMA((2, 2))]
    return pl.pallas_call(
        functools.partial(_mixer_kernel, fused),
        grid_spec=pltpu.PrefetchScalarGridSpec(
            num_scalar_prefetch=1 if fused else 0,
            grid=(N_TILES,),
            in_specs=in_specs,
            out_specs=out_specs,
            scratch_shapes=scratch,
        ),
        out_shape=out_shape,
        compiler_params=pltpu.CompilerParams(
            dimension_semantics=("arbitrary",), vmem_limit_bytes=VMEM_LIMIT),
        name="mixer",
    )(*x_src, *params)


def _slots_kernel(ids_ref, slot_ref, tile_expert_ref, n_used_ref, count_ref, offset_ref):
    ids = ids_ref[...]
    lane = lax.broadcasted_iota(jnp.int32, (1, LANES), 1)
    counts = jnp.zeros((1, LANES), _f32)
    offsets = jnp.zeros((1, LANES), _f32)
    r_i = lax.broadcasted_iota(jnp.int32, (LANES, LANES), 0)
    c_i = lax.broadcasted_iota(jnp.int32, (LANES, LANES), 1)
    upper = (r_i < c_i).astype(_bf16)
    rr = lax.broadcasted_iota(jnp.int32, (ASSIGN_ROWS, ASSIGN_ROWS), 0)
    rc = lax.broadcasted_iota(jnp.int32, (ASSIGN_ROWS, ASSIGN_ROWS), 1)
    lower = (rc < rr).astype(_bf16)
    tile_row0 = (lax.broadcasted_iota(jnp.int32, (1, LANES), 1) * TE).astype(_f32)

    slot = jnp.zeros((ASSIGN_ROWS, LANES), _f32)
    tile_expert = jnp.zeros((1, LANES), _f32)
    off = jnp.zeros((1, 1), _f32)
    for e in range(N_EXPERTS):
        m = (ids == e).astype(_f32)
        within = _dot(m.astype(_bf16), upper)
        rowsum = jnp.sum(m, axis=1, keepdims=True)
        rowpre = _dot(lower, jnp.broadcast_to(rowsum, (ASSIGN_ROWS, LANES)).astype(_bf16))
        cnt = jnp.sum(rowsum, axis=0, keepdims=True)
        padded = jnp.floor((cnt + (TE - 1)) * (1.0 / TE)) * TE
        slot = slot + m * (off + rowpre + within)
        in_seg = (tile_row0 >= off) & (tile_row0 < off + padded)
        tile_expert = tile_expert + jnp.where(in_seg, float(e), 0.0)
        counts = counts + jnp.where(lane == e, cnt, 0.0)
        offsets = offsets + jnp.where(lane == e, off, 0.0)
        off = off + padded
    tile_expert = jnp.where(tile_row0 >= off, float(N_EXPERTS - 1), tile_expert)
    slot_ref[...] = slot.astype(jnp.int32)
    tile_expert_ref[...] = tile_expert.astype(jnp.int32)
    n_used_ref[...] = jnp.broadcast_to(off * (1.0 / TE), (1, LANES)).astype(jnp.int32)
    count_ref[...] = counts.astype(jnp.int32)
    offset_ref[...] = offsets.astype(jnp.int32)


def _slots_call(ids):
    return pl.pallas_call(
        _slots_kernel,
        out_shape=[jax.ShapeDtypeStruct((ASSIGN_ROWS, LANES), jnp.int32)]
        + [jax.ShapeDtypeStruct((1, LANES), jnp.int32)] * 4,
        name="slots",
    )(ids)


X_BUFS = 3
Y_BUFS = 2


def _experts_kernel(layer, te_ref, nu_ref, count_ref, offset_ref, slot_ref,
                    x1_hbm, wg_hbm, wu_hbm, wd_hbm, y_in_hbm, y_hbm,
                    tok_of_slot, xbuf, ybuf, wg_st, wu_st, wd_st, wg_b, wu_b, wd_b, sem_x, sem_y, sem_w):
    del y_in_hbm
    n_used = nu_ref[0]
    tile_shift = TE.bit_length() - 1

    for k in range(2):
        def fill(t, carry, k=k):
            tok_of_slot[slot_ref[k * N_TOK + t]] = t
            return carry
        lax.fori_loop(0, N_TOK, fill, 0, unroll=16)
    for e in range(N_EXPERTS):
        first_pad = offset_ref[e] + count_ref[e]
        seg_end = offset_ref[e] + (((count_ref[e] + (TE - 1)) >> tile_shift) << tile_shift)

        def pad(s, carry):
            tok_of_slot[s] = 0
            return carry
        lax.fori_loop(first_pad, seg_end, pad, 0)

    def gather(tile, buf):
        base = tile * TE
        for r in range(TE):
            pltpu.make_async_copy(x1_hbm.at[pl.ds(tok_of_slot[base + r], 1)],
                                  xbuf.at[buf, pl.ds(r, 1)], sem_x.at[buf]).start(priority=r % 2)

    def weight_copies(expert, ws):
        return [pltpu.make_async_copy(src.at[layer, expert], dst.at[ws], sem_w.at[ws])
                for src, dst in ((wg_hbm, wg_st), (wu_hbm, wu_st), (wd_hbm, wd_st))]

    def y_copy(tile, buf):
        return pltpu.make_async_copy(ybuf.at[buf], y_hbm.at[pl.ds(tile * TE, TE)], sem_y.at[buf])

    for c in weight_copies(te_ref[0], 0):
        c.start()
    gather(0, 0)

    @pl.when(n_used > 1)
    def _():
        gather(1, 1)

    def tile_step(j, seg):
        expert = te_ref[j]
        first = (j == 0) | (te_ref[jnp.maximum(j - 1, 0)] != expert)

        @pl.when(j + (X_BUFS - 1) < n_used)
        def _():
            gather(j + (X_BUFS - 1), (j + (X_BUFS - 1)) % X_BUFS)

        @pl.when(first)
        def _():
            ws = seg % 2
            for c in weight_copies(expert, ws):
                c.wait()
            wg_b[...] = wg_st[ws].astype(_bf16)
            wu_b[...] = wu_st[ws].astype(_bf16)
            wd_b[...] = wd_st[ws].astype(_bf16)
            next_first = j + ((count_ref[expert] + (TE - 1)) >> tile_shift)

            @pl.when(next_first < n_used)
            def _():
                for c in weight_copies(te_ref[next_first], 1 - ws):
                    c.start()

        xslot = j % X_BUFS
        pltpu.make_async_copy(x1_hbm.at[pl.ds(0, TE)], xbuf.at[xslot], sem_x.at[xslot]).wait()
        xb = xbuf[xslot].astype(_bf16)
        hg = _dot(xb, wg_b[...])
        hu = _dot(xb, wu_b[...])
        a = hg / (1.0 + jnp.exp(-hg)) * hu
        y = _dot(a.astype(_bf16), wd_b[...])

        yslot = j % Y_BUFS

        @pl.when(j >= Y_BUFS)
        def _():
            y_copy(j - Y_BUFS, yslot).wait()

        ybuf[yslot] = y
        y_copy(j, yslot).start()
        return seg + first.astype(jnp.int32)

    lax.fori_loop(0, n_used, tile_step, jnp.int32(0))

    for back in range(1, Y_BUFS + 1):
        @pl.when(n_used >= back)
        def _(back=back):
            y_copy(n_used - back, (n_used - back) % Y_BUFS).wait()


def _experts_call(layer, tile_expert, n_used, counts, offsets, slot_flat, x1, wg, wu, wd, y_buf):
    any_spec = pl.BlockSpec(memory_space=pl.ANY)
    n_prefetch = 5
    return pl.pallas_call(
        functools.partial(_experts_kernel, layer),
        grid_spec=pltpu.PrefetchScalarGridSpec(
            num_scalar_prefetch=n_prefetch,
            grid=(1,),
            in_specs=[any_spec] * 5,
            out_specs=any_spec,
            scratch_shapes=[
                pltpu.SMEM((N_SLOTS,), jnp.int32),
                pltpu.VMEM((X_BUFS, TE, D_MODEL), _f32),
                pltpu.VMEM((Y_BUFS, TE, D_MODEL), _f32),
                pltpu.VMEM((2, D_MODEL, D_EXPERT), _f32),
                pltpu.VMEM((2, D_MODEL, D_EXPERT), _f32),
                pltpu.VMEM((2, D_EXPERT, D_MODEL), _f32),
                pltpu.VMEM((D_MODEL, D_EXPERT), _bf16),
                pltpu.VMEM((D_MODEL, D_EXPERT), _bf16),
                pltpu.VMEM((D_EXPERT, D_MODEL), _bf16),
                pltpu.SemaphoreType.DMA((X_BUFS,)),
                pltpu.SemaphoreType.DMA((Y_BUFS,)),
                pltpu.SemaphoreType.DMA((2,)),
            ],
        ),
        out_shape=jax.ShapeDtypeStruct((N_SLOTS, D_MODEL), _f32),
        input_output_aliases={n_prefetch + 4: 0},
        compiler_params=pltpu.CompilerParams(
            dimension_semantics=("arbitrary",), vmem_limit_bytes=VMEM_LIMIT),
        name="experts",
    )(tile_expert, n_used, counts, offsets, slot_flat, x1, wg, wu, wd, y_buf)


def _combine_kernel(slot_ref, x1_ref, route_ref, g2_ref, b2_ref, y_hbm, out_p_ref, out_s_ref,
                    ybuf, sems):
    i = pl.program_id(0)
    b = i % 2

    def gather(tile, slot):
        tok0 = tile * TM

        def start(r, carry):
            for k in range(2):
                pltpu.make_async_copy(y_hbm.at[pl.ds(slot_ref[k * N_TOK + tok0 + r], 1)],
                                      ybuf.at[slot, k, pl.ds(r, 1)], sems.at[slot, k]).start(priority=k)
            return carry

        lax.fori_loop(0, TM, start, 0, unroll=16)

    @pl.when(i == 0)
    def _():
        gather(0, 0)

    @pl.when(i + 1 < pl.num_programs(0))
    def _():
        gather(i + 1, 1 - b)

    for k in range(2):
        pltpu.make_async_copy(y_hbm.at[pl.ds(0, TM)], ybuf.at[b, k], sems.at[b, k]).wait()

    route = route_ref[...]
    moe = route[:, 2:3] * ybuf[b, 0] + route[:, 3:4] * ybuf[b, 1]
    out = _layer_norm(ALPHA * x1_ref[...] + moe, g2_ref[...], b2_ref[...])

    @pl.when(i < N_PROMPT_TILES)
    def _():
        out_p_ref[...] = out

    @pl.when(i >= N_PROMPT_TILES)
    def _():
        out_s_ref[...] = out


def _combine_call(slot_flat, x1, route, g2, b2, y):
    tok = lambda i, s: (i, 0)
    const = lambda i, s: (0, 0)
    return pl.pallas_call(
        _combine_kernel,
        grid_spec=pltpu.PrefetchScalarGridSpec(
            num_scalar_prefetch=1,
            grid=(N_TILES,),
            in_specs=[
                pl.BlockSpec((TM, D_MODEL), tok),
                pl.BlockSpec((TM, LANES), tok),
                pl.BlockSpec((1, D_MODEL), const),
                pl.BlockSpec((1, D_MODEL), const),
                pl.BlockSpec(memory_space=pl.ANY),
            ],
            out_specs=[
                pl.BlockSpec((TM, D_MODEL), lambda i, s: (jnp.minimum(i, N_PROMPT_TILES - 1), 0)),
                pl.BlockSpec((TM, D_MODEL), lambda i, s: (jnp.maximum(i - N_PROMPT_TILES, 0), 0)),
            ],
            scratch_shapes=[
                pltpu.VMEM((2, 2, TM, D_MODEL), _f32),
                pltpu.SemaphoreType.DMA((2, 2)),
            ],
        ),
        out_shape=[jax.ShapeDtypeStruct((N_PROMPT, D_MODEL), _f32),
                   jax.ShapeDtypeStruct((N_SAMPLE, D_MODEL), _f32)],
        compiler_params=pltpu.CompilerParams(dimension_semantics=("arbitrary",)),
        name="combine",
    )(slot_flat, x1, route, g2, b2, y)


def kernel(x_prompt, x_sample, state_conv, state_pool, w_in, conv_w, sgu_ln_g, sgu_ln_b, sgu_w, sgu_b, pool_w, pool_scale, w_out, ln1_g, ln1_b, router_group_w, router_group_b, router_expert_w, router_expert_b, expert_w_gate, expert_w_up, expert_w_down, ln2_g, ln2_b):
    y_buf = jnp.zeros((N_SLOTS, D_MODEL), _f32)
    reps = CHUNK // DEC_SEQ
    row = lambda a: a.reshape(DEPTH, 1, -1)

    w_r = jnp.concatenate([router_group_w, router_expert_w], axis=2)
    w_r = jnp.pad(w_r, ((0, 0), (0, 0), (0, LANES - w_r.shape[2])))
    wr_hi = w_r.astype(_bf16)
    wr = jnp.concatenate([wr_hi, (w_r - wr_hi.astype(_f32)).astype(_bf16)], axis=2)
    b_r = jnp.concatenate([router_group_b, router_expert_b], axis=1)
    b_r = row(jnp.pad(b_r, ((0, 0), (0, LANES - b_r.shape[1]))))
    ws_s = jnp.tile(sgu_w[:, :, :DEC_SEQ, :DEC_SEQ], (1, 1, reps, reps))
    bias_p = jnp.swapaxes(sgu_b, 1, 2)
    bias_s = jnp.tile(bias_p[:, :DEC_SEQ], (1, reps, 1))
    zst = jnp.pad(state_conv, ((0, 0), (0, 0), (CONV_HALO - 2, 0), (0, 0)))
    pst = jnp.pad(state_pool, ((0, 0), (0, 0), (POOL_HALO - POOL_BUF, 0), (0, 0)))
    mixer_params = (w_in.astype(_bf16), w_out.astype(_bf16), conv_w, row(sgu_ln_g), row(sgu_ln_b), sgu_w,
                    ws_s, bias_p, bias_s, pool_w, row(pool_scale), row(ln1_g), row(ln1_b), wr, b_r, zst, pst)
    g2, b2 = row(ln2_g), row(ln2_b)

    ztails, ptails, zss, psamps, v_s = [], [], [], [], []
    x_src = (x_prompt.reshape(N_PROMPT, D_MODEL), x_sample.reshape(N_SAMPLE, D_MODEL))
    for l in range(DEPTH):
        x1, route, ztail, ptail, zs, psamp, vns = _mixer_call(l, x_src, mixer_params)
        ids = route[:, 0:2].astype(jnp.int32).T.reshape(ASSIGN_ROWS, LANES)
        slot, tile_expert, n_used, counts, offsets = _slots_call(ids)
        slot_flat = slot.reshape(N_ASSIGN)
        y = _experts_call(l, tile_expert.reshape(LANES), n_used.reshape(LANES)[:1], counts.reshape(LANES),
                          offsets.reshape(LANES), slot_flat, x1, expert_w_gate, expert_w_up, expert_w_down,
                          y_buf)
        x_src = (slot_flat, x1, route, y, g2, b2)
        y_buf = y
        ztails.append(ztail)
        ptails.append(ptail)
        zss.append(zs)
        psamps.append(psamp)
        v_s.append(vns)

    x_p, x_s = _combine_call(slot_flat, x1, route, g2[DEPTH - 1], b2[DEPTH - 1], y)
    y_prompt = x_p.reshape(BATCH, SEQ, D_MODEL)
    y_sample = x_s.reshape(DEC_BATCH, DEC_SEQ, D_MODEL)
    new_conv_prompt = jnp.stack(ztails)[:, :BATCH, CONV_HALO - 2:]
    new_pool_prompt = jnp.stack(ptails)[:, :BATCH, POOL_HALO - POOL_BUF:]
    new_conv_sample = jnp.stack(zss)[:, :, DEC_SEQ - 2:]
    new_pool_sample = jnp.concatenate([state_pool[:, :, DEC_SEQ:], jnp.stack(psamps)], axis=2)
    return (y_prompt, y_sample, new_conv_prompt, new_pool_prompt, new_conv_sample, new_pool_sample,
            jnp.stack(v_s))
```

```python
import functools

import jax
import jax.numpy as jnp
from jax import lax
from jax.experimental import pallas as pl
from jax.experimental.pallas import tpu as pltpu

D_MODEL = 2048
BATCH = 4
SEQ = 2048
DEPTH = 4
DEC_BATCH = 128
DEC_SEQ = 8
PAST_LEN = 16384
HEAD_DIM = 128
CONV_WIDTH = 768
POOL_WIDTH = 512
CHUNK_WIDTH = 768
CHUNK = 128
N_CHUNK_HEADS = 6
POOL_WINDOWS = (2, 4, 8, 16)
POOL_GROUP_DIM = 128
POOL_BUF = 15
IN_WIDTH = 4352
N_EXPERT_GROUPS = 4
EXPERTS_PER_GROUP = 4
N_EXPERTS = 16
D_EXPERT = 512
ALPHA = (2 * DEPTH) ** 0.25
LN_EPS = 1e-5

N_PROMPT = BATCH * SEQ
N_SAMPLE = DEC_BATCH * DEC_SEQ
N_TOK = N_PROMPT + N_SAMPLE
N_ASSIGN = 2 * N_TOK

LANES = 128
SUBLANES = 8
CONV_HALO = 8
POOL_HALO = 16

TM = 256
TILES_PER_SEQ = SEQ // TM
N_PROMPT_TILES = N_PROMPT // TM
N_SAMPLE_TILES = N_SAMPLE // TM
N_TILES = N_PROMPT_TILES + N_SAMPLE_TILES
SEQS_PER_TILE = TM // DEC_SEQ

TE = 256
N_SLOTS = N_ASSIGN + N_EXPERTS * TE
N_ETILES = N_SLOTS // TE
ASSIGN_ROWS = N_ASSIGN // LANES

CUT_B, CUT_C, CUT_H, CUT_U, CUT_V, CUT_P = 0, 768, 1536, 2304, 3072, 3840

VMEM_LIMIT = 62 * 1024 * 1024

_f32 = jnp.float32
_bf16 = jnp.bfloat16


def _dot(a, b):
    return jnp.dot(a, b, preferred_element_type=_f32)


def _layer_norm(r, g, b):
    mu = jnp.mean(r, axis=-1, keepdims=True)
    c = r - mu
    var = jnp.mean(c * c, axis=-1, keepdims=True)
    return c * lax.rsqrt(var + LN_EPS) * g + b


class _RowGather:
    def __init__(self, start_row, first_row):
        self._start_row = start_row
        self._done = first_row

    def upto(self, row_end):
        for r in range(self._done, row_end):
            self._start_row(r)
        self._done = max(self._done, row_end)


class _NoGather:
    def upto(self, row_end):
        del row_end


BRANCH_GATHER_ROWS = 192


def _mixer_body(xb, w_in_ref, conv_w_ref, sg_ref, sb_ref, ws_ref, bias_ref, pw_ref, ps_ref,
                zext_ref, pext_ref, ybuf_ref, seq_len, pos, chunk_len, gather):
    n_seq = TM // seq_len

    z = _dot(xb, w_in_ref[:, CUT_C:CUT_H]) * _dot(xb, w_in_ref[:, CUT_H:CUT_U])
    zext_ref[:, CONV_HALO:, :] = z.reshape(n_seq, seq_len, CONV_WIDTH)
    gather.upto(16)
    cw = conv_w_ref[...]
    conv = (cw[0:1, :] * zext_ref[:, CONV_HALO - 2:CONV_HALO - 2 + seq_len, :]
            + cw[1:2, :] * zext_ref[:, CONV_HALO - 1:CONV_HALO - 1 + seq_len, :]
            + cw[2:3, :] * zext_ref[:, CONV_HALO:, :])
    y_conv = _dot(xb, w_in_ref[:, CUT_B:CUT_C]) * conv.reshape(TM, CONV_WIDTH)
    ybuf_ref[:, 0:CONV_WIDTH] = y_conv.astype(_bf16)
    gather.upto(32)

    v = _dot(xb, w_in_ref[:, CUT_V:CUT_P])
    vn = _layer_norm(v, sg_ref[...], sb_ref[...])
    vnb = vn.astype(_bf16)
    gather.upto(48)
    r_i = lax.broadcasted_iota(jnp.int32, (CHUNK, CHUNK), 0)
    c_i = lax.broadcasted_iota(jnp.int32, (CHUNK, CHUNK), 1)
    shift = chunk_len.bit_length() - 1
    mask = (c_i <= r_i) & ((r_i >> shift) == (c_i >> shift))
    u = _dot(xb, w_in_ref[:, CUT_U:CUT_V])
    gather.upto(64)
    n_chunks = TM // CHUNK
    for h in range(N_CHUNK_HEADS):
        m_h = jnp.where(mask, ws_ref[h], 0.0).astype(_bf16)
        b_h = bias_ref[:, h:h + 1]
        cols = slice(h * HEAD_DIM, (h + 1) * HEAD_DIM)
        vn_h = jnp.concatenate([vnb[c * CHUNK:(c + 1) * CHUNK, cols] for c in range(n_chunks)], axis=1)
        mixed = _dot(m_h, vn_h)
        for c in range(n_chunks):
            rows = slice(c * CHUNK, (c + 1) * CHUNK)
            ybuf_ref[rows, CONV_WIDTH + h * HEAD_DIM:CONV_WIDTH + (h + 1) * HEAD_DIM] = (
                u[rows, cols] * (mixed[:, c * HEAD_DIM:(c + 1) * HEAD_DIM] + b_h)).astype(_bf16)
        gather.upto(64 + 8 * (h + 1))

    p = _dot(xb, w_in_ref[:, CUT_P:IN_WIDTH])
    pext_ref[:, POOL_HALO:, :] = p.reshape(n_seq, seq_len, POOL_WIDTH)
    gather.upto(128)
    posf = (pos + 1).astype(_f32)
    for g, w in enumerate(POOL_WINDOWS):
        cols = slice(g * POOL_GROUP_DIM, (g + 1) * POOL_GROUP_DIM)
        s = pext_ref[:, POOL_HALO:, cols]
        for j in range(1, w):
            s = s + pext_ref[:, POOL_HALO - j:POOL_HALO - j + seq_len, cols]
        cnt = jnp.minimum(float(w), posf)
        diff = s.reshape(TM, POOL_GROUP_DIM) / cnt - p[:, cols]
        y_g = _dot(diff.astype(_bf16), pw_ref[g].astype(_bf16)) * ps_ref[:, cols]
        ybuf_ref[:, CONV_WIDTH + CHUNK_WIDTH + g * POOL_GROUP_DIM:
                 CONV_WIDTH + CHUNK_WIDTH + (g + 1) * POOL_GROUP_DIM] = y_g.astype(_bf16)
        gather.upto(128 + 16 * (g + 1))
    assert 128 + 16 * len(POOL_WINDOWS) == BRANCH_GATHER_ROWS
    return z, p, vn


def _route(x1, wr_ref, br_ref):
    x_hi = x1.astype(_bf16)
    x_lo = (x1 - x_hi.astype(_f32)).astype(_bf16)
    hi_parts = _dot(x_hi, wr_ref[...])
    logits = hi_parts[:, :LANES] + hi_parts[:, LANES:] + _dot(x_lo, wr_ref[:, :LANES]) + br_ref[...]
    lane = lax.broadcasted_iota(jnp.int32, logits.shape, 1)
    neg = -jnp.inf
    big = jnp.int32(1 << 20)

    gmask = lane < N_EXPERT_GROUPS
    gl = jnp.where(gmask, logits, neg)
    g_max = jnp.max(gl, axis=1, keepdims=True)
    g_idx = jnp.min(jnp.where(gmask & (gl == g_max), lane, big), axis=1, keepdims=True)
    g_val = 1.0 / jnp.sum(jnp.exp(gl - g_max), axis=1, keepdims=True)

    e_lane = lane - N_EXPERT_GROUPS
    emask = (e_lane >= 0) & (e_lane < N_EXPERTS) & ((e_lane >> (EXPERTS_PER_GROUP.bit_length() - 1)) == g_idx)
    el = jnp.where(emask, logits, neg)
    e_max = jnp.max(el, axis=1, keepdims=True)
    i1 = jnp.min(jnp.where(emask & (el == e_max), e_lane, big), axis=1, keepdims=True)
    e_sum = jnp.sum(jnp.exp(el - e_max), axis=1, keepdims=True)
    mask2 = emask & (e_lane != i1)
    e_second = jnp.max(jnp.where(mask2, el, neg), axis=1, keepdims=True)
    i2 = jnp.min(jnp.where(mask2 & (el == e_second), e_lane, big), axis=1, keepdims=True)
    v1 = 1.0 / e_sum
    v2 = jnp.exp(e_second - e_max) / e_sum
    tot = v1 + v2
    gate1 = g_val * (v1 / tot)
    gate2 = g_val * (v2 / tot)
    return jnp.where(lane == 0, i1.astype(_f32),
                     jnp.where(lane == 1, i2.astype(_f32),
                               jnp.where(lane == 2, gate1, jnp.where(lane == 3, gate2, 0.0))))


N_MIXER_COMMON_REFS = 29


def _mixer_kernel(fused, *refs):
    if fused:
        slot_ref, x1p_ref, routep_ref, y_hbm, g2_ref, b2_ref = refs[:6]
        refs = refs[6:]
    else:
        xp_ref, xsamp_ref = refs[:2]
        refs = refs[2:]
    (w_in_ref, w_out_ref, conv_w_ref, sg_ref, sb_ref, ws_p_ref, ws_s_ref, bias_p_ref, bias_s_ref,
     pw_ref, ps_ref, g1_ref, b1_ref, wr_ref, br_ref, zst_ref, pst_ref,
     x1_ref, route_ref, ztail_ref, ptail_ref, zs_ref, psamp_ref, vns_ref,
     zext_p, pext_p, zext_s, pext_s, ybuf_ref) = refs[:N_MIXER_COMMON_REFS]
    i = pl.program_id(0)
    last = pl.num_programs(0) - 1

    if fused:
        gbuf, sems = refs[N_MIXER_COMMON_REFS:]
        b = i % 2

        def row_starter(tile, half):
            tok0 = tile * TM

            def start_row(r):
                for k in range(2):
                    pltpu.make_async_copy(y_hbm.at[pl.ds(slot_ref[k * N_TOK + tok0 + r], 1)],
                                          gbuf.at[half, k, pl.ds(r, 1)], sems.at[half, k]).start(priority=k)
            return start_row

        def wait_rows(half):
            for k in range(2):
                pltpu.make_async_copy(y_hbm.at[pl.ds(0, TM)], gbuf.at[half, k], sems.at[half, k]).wait()

        @pl.when(i == 0)
        def _():
            first = row_starter(0, 0)

            def body(r, carry):
                first(r)
                return carry

            lax.fori_loop(0, TM, body, 0, unroll=16)

        wait_rows(b)
        routep = routep_ref[...]
        moe = routep[:, 2:3] * gbuf[b, 0] + routep[:, 3:4] * gbuf[b, 1]
        x = _layer_norm(ALPHA * x1p_ref[...] + moe, g2_ref[...], b2_ref[...])
        next_rows = row_starter(jnp.minimum(i + 1, last), 1 - b)
        new_gather = lambda first_row: _RowGather(next_rows, first_row)
    else:
        x = jnp.where(i < N_PROMPT_TILES, xp_ref[...], xsamp_ref[...])
        new_gather = lambda first_row: _NoGather()

    xb = x.astype(_bf16)
    row = lax.broadcasted_iota(jnp.int32, (TM, 1), 0)

    @pl.when(i < N_PROMPT_TILES)
    def _prompt():
        t0 = (i % TILES_PER_SEQ) * TM

        @pl.when(t0 == 0)
        def _():
            zext_p[:, 0:CONV_HALO, :] = jnp.zeros((1, CONV_HALO, CONV_WIDTH), _f32)
            pext_p[:, 0:POOL_HALO, :] = jnp.zeros((1, POOL_HALO, POOL_WIDTH), _f32)

        z, p, _ = _mixer_body(xb, w_in_ref, conv_w_ref, sg_ref, sb_ref, ws_p_ref, bias_p_ref,
                              pw_ref, ps_ref, zext_p, pext_p, ybuf_ref, TM, t0 + row, CHUNK, new_gather(0))
        z_last = z[TM - CONV_HALO:, :].reshape(1, CONV_HALO, CONV_WIDTH)
        p_last = p[TM - POOL_HALO:, :].reshape(1, POOL_HALO, POOL_WIDTH)
        zext_p[:, 0:CONV_HALO, :] = z_last
        pext_p[:, 0:POOL_HALO, :] = p_last
        ztail_ref[...] = z_last
        ptail_ref[...] = p_last

    @pl.when(i >= N_PROMPT_TILES)
    def _sample():
        zext_s[:, 0:CONV_HALO, :] = zst_ref[...]
        pext_s[:, 0:POOL_HALO, :] = pst_ref[...]
        z, p, vn = _mixer_body(xb, w_in_ref, conv_w_ref, sg_ref, sb_ref, ws_s_ref, bias_s_ref,
                               pw_ref, ps_ref, zext_s, pext_s, ybuf_ref, DEC_SEQ,
                               PAST_LEN + (row & (DEC_SEQ - 1)), DEC_SEQ, new_gather(0))
        zs_ref[...] = z.reshape(SEQS_PER_TILE, DEC_SEQ, CONV_WIDTH)
        psamp_ref[...] = p.reshape(SEQS_PER_TILE, DEC_SEQ, POOL_WIDTH)
        vns_ref[...] = vn.reshape(SEQS_PER_TILE, DEC_SEQ, CHUNK_WIDTH)
        ztail_ref[...] = jnp.zeros(ztail_ref.shape, _f32)
        ptail_ref[...] = jnp.zeros(ptail_ref.shape, _f32)

    tail_gather = new_gather(BRANCH_GATHER_ROWS)
    mix = _dot(ybuf_ref[...], w_out_ref[...])
    tail_gather.upto(BRANCH_GATHER_ROWS + (TM - BRANCH_GATHER_ROWS) // 2)
    x1 = _layer_norm(ALPHA * x + mix, g1_ref[...], b1_ref[...])
    x1_ref[...] = x1
    tail_gather.upto(TM)
    route_ref[...] = _route(x1, wr_ref, br_ref)

    if fused:
        @pl.when(i == last)
        def _():
            wait_rows(1 - b)


def _layer_spec(layer, shape):
    nd = len(shape)
    return pl.BlockSpec((None,) + shape, lambda i, *_: (layer,) + (0,) * nd, pipeline_mode=pl.Buffered(1))


def _mixer_call(layer, x_src, params):
    fused = len(x_src) == 6
    samp = lambda i, *_: (jnp.maximum(i - N_PROMPT_TILES, 0), 0, 0)
    samp_l = lambda i, *_: (layer, jnp.maximum(i - N_PROMPT_TILES, 0), 0, 0)
    tail = lambda i, *_: (i // TILES_PER_SEQ, 0, 0)
    tok = lambda i, *_: (i, 0)
    if fused:
        x_specs = [
            pl.BlockSpec((TM, D_MODEL), tok),
            pl.BlockSpec((TM, LANES), tok),
            pl.BlockSpec(memory_space=pl.ANY),
            _layer_spec(layer - 1, (1, D_MODEL)),
            _layer_spec(layer - 1, (1, D_MODEL)),
        ]
    else:
        x_specs = [
            pl.BlockSpec((TM, D_MODEL), lambda i, *_: (jnp.minimum(i, N_PROMPT_TILES - 1), 0)),
            pl.BlockSpec((TM, D_MODEL), lambda i, *_: (jnp.maximum(i - N_PROMPT_TILES, 0), 0)),
        ]
    in_specs = x_specs + [
        _layer_spec(layer, (D_MODEL, IN_WIDTH)),
        _layer_spec(layer, (D_MODEL, D_MODEL)),
        _layer_spec(layer, (3, CONV_WIDTH)),
        _layer_spec(layer, (1, CHUNK_WIDTH)),
        _layer_spec(layer, (1, CHUNK_WIDTH)),
        _layer_spec(layer, (N_CHUNK_HEADS, CHUNK, CHUNK)),
        _layer_spec(layer, (N_CHUNK_HEADS, CHUNK, CHUNK)),
        _layer_spec(layer, (CHUNK, N_CHUNK_HEADS)),
        _layer_spec(layer, (CHUNK, N_CHUNK_HEADS)),
        _layer_spec(layer, (4, POOL_GROUP_DIM, POOL_GROUP_DIM)),
        _layer_spec(layer, (1, POOL_WIDTH)),
        _layer_spec(layer, (1, D_MODEL)),
        _layer_spec(layer, (1, D_MODEL)),
        _layer_spec(layer, (D_MODEL, 2 * LANES)),
        _layer_spec(layer, (1, LANES)),
        pl.BlockSpec((None, SEQS_PER_TILE, CONV_HALO, CONV_WIDTH), samp_l),
        pl.BlockSpec((None, SEQS_PER_TILE, POOL_HALO, POOL_WIDTH), samp_l),
    ]
    out_specs = [
        pl.BlockSpec((TM, D_MODEL), tok),
        pl.BlockSpec((TM, LANES), tok),
        pl.BlockSpec((1, CONV_HALO, CONV_WIDTH), tail),
        pl.BlockSpec((1, POOL_HALO, POOL_WIDTH), tail),
        pl.BlockSpec((SEQS_PER_TILE, DEC_SEQ, CONV_WIDTH), samp),
        pl.BlockSpec((SEQS_PER_TILE, DEC_SEQ, POOL_WIDTH), samp),
        pl.BlockSpec((SEQS_PER_TILE, DEC_SEQ, CHUNK_WIDTH), samp),
    ]
    out_shape = [
        jax.ShapeDtypeStruct((N_TOK, D_MODEL), _f32),
        jax.ShapeDtypeStruct((N_TOK, LANES), _f32),
        jax.ShapeDtypeStruct((BATCH + 1, CONV_HALO, CONV_WIDTH), _f32),
        jax.ShapeDtypeStruct((BATCH + 1, POOL_HALO, POOL_WIDTH), _f32),
        jax.ShapeDtypeStruct((DEC_BATCH, DEC_SEQ, CONV_WIDTH), _f32),
        jax.ShapeDtypeStruct((DEC_BATCH, DEC_SEQ, POOL_WIDTH), _f32),
        jax.ShapeDtypeStruct((DEC_BATCH, DEC_SEQ, CHUNK_WIDTH), _f32),
    ]
    scratch = [
        pltpu.VMEM((1, CONV_HALO + TM, CONV_WIDTH), _f32),
        pltpu.VMEM((1, POOL_HALO + TM, POOL_WIDTH), _f32),
        pltpu.VMEM((SEQS_PER_TILE, CONV_HALO + DEC_SEQ, CONV_WIDTH), _f32),
        pltpu.VMEM((SEQS_PER_TILE, POOL_HALO + DEC_SEQ, POOL_WIDTH), _f32),
        pltpu.VMEM((TM, D_MODEL), _bf16),
    ]
    assert len(in_specs) - len(x_specs) + len(out_specs) + len(scratch) == N_MIXER_COMMON_REFS
    if fused:
        scratch += [pltpu.VMEM((2, 2, TM, D_MODEL), _f32), pltpu.SemaphoreType.DMA((2, 2))]
    return pl.pallas_call(
        functools.partial(_mixer_kernel, fused),
        grid_spec=pltpu.PrefetchScalarGridSpec(
            num_scalar_prefetch=1 if fused else 0,
            grid=(N_TILES,),
            in_specs=in_specs,
            out_specs=out_specs,
            scratch_shapes=scratch,
        ),
        out_shape=out_shape,
        compiler_params=pltpu.CompilerParams(
            dimension_semantics=("arbitrary",), vmem_limit_bytes=VMEM_LIMIT),
        name="mixer",
    )(*x_src, *params)


def _slots_kernel(ids_ref, slot_ref, tile_expert_ref, n_used_ref, count_ref, offset_ref):
    ids = ids_ref[...]
    lane = lax.broadcasted_iota(jnp.int32, (1, LANES), 1)
    counts = jnp.zeros((1, LANES), _f32)
    offsets = jnp.zeros((1, LANES), _f32)
    r_i = lax.broadcasted_iota(jnp.int32, (LANES, LANES), 0)
    c_i = lax.broadcasted_iota(jnp.int32, (LANES, LANES), 1)
    upper = (r_i < c_i).astype(_bf16)
    rr = lax.broadcasted_iota(jnp.int32, (ASSIGN_ROWS, ASSIGN_ROWS), 0)
    rc = lax.broadcasted_iota(jnp.int32, (ASSIGN_ROWS, ASSIGN_ROWS), 1)
    lower = (rc < rr).astype(_bf16)
    tile_row0 = (lax.broadcasted_iota(jnp.int32, (1, LANES), 1) * TE).astype(_f32)

    slot = jnp.zeros((ASSIGN_ROWS, LANES), _f32)
    tile_expert = jnp.zeros((1, LANES), _f32)
    off = jnp.zeros((1, 1), _f32)
    for e in range(N_EXPERTS):
        m = (ids == e).astype(_f32)
        within = _dot(m.astype(_bf16), upper)
        rowsum = jnp.sum(m, axis=1, keepdims=True)
        rowpre = _dot(lower, jnp.broadcast_to(rowsum, (ASSIGN_ROWS, LANES)).astype(_bf16))
        cnt = jnp.sum(rowsum, axis=0, keepdims=True)
        padded = jnp.floor((cnt + (TE - 1)) * (1.0 / TE)) * TE
        slot = slot + m * (off + rowpre + within)
        in_seg = (tile_row0 >= off) & (tile_row0 < off + padded)
        tile_expert = tile_expert + jnp.where(in_seg, float(e), 0.0)
        counts = counts + jnp.where(lane == e, cnt, 0.0)
        offsets = offsets + jnp.where(lane == e, off, 0.0)
        off = off + padded
    tile_expert = jnp.where(tile_row0 >= off, float(N_EXPERTS - 1), tile_expert)
    slot_ref[...] = slot.astype(jnp.int32)
    tile_expert_ref[...] = tile_expert.astype(jnp.int32)
    n_used_ref[...] = jnp.broadcast_to(off * (1.0 / TE), (1, LANES)).astype(jnp.int32)
    count_ref[...] = counts.astype(jnp.int32)
    offset_ref[...] = offsets.astype(jnp.int32)


def _slots_call(ids):
    return pl.pallas_call(
        _slots_kernel,
        out_shape=[jax.ShapeDtypeStruct((ASSIGN_ROWS, LANES), jnp.int32)]
        + [jax.ShapeDtypeStruct((1, LANES), jnp.int32)] * 4,
        name="slots",
    )(ids)


X_BUFS = 4
Y_BUFS = 3


def _experts_kernel(layer, te_ref, nu_ref, count_ref, offset_ref, slot_ref,
                    x1_hbm, wg_hbm, wu_hbm, wd_hbm, y_in_hbm, y_hbm,
                    tok_of_slot, xbuf, ybuf, wg_st, wu_st, wd_st, wg_b, wu_b, wd_b, sem_x, sem_y, sem_w):
    del y_in_hbm
    n_used = nu_ref[0]
    tile_shift = TE.bit_length() - 1

    for k in range(2):
        def fill(t, carry, k=k):
            tok_of_slot[slot_ref[k * N_TOK + t]] = t
            return carry
        lax.fori_loop(0, N_TOK, fill, 0, unroll=16)
    for e in range(N_EXPERTS):
        first_pad = offset_ref[e] + count_ref[e]
        seg_end = offset_ref[e] + (((count_ref[e] + (TE - 1)) >> tile_shift) << tile_shift)

        def pad(s, carry):
            tok_of_slot[s] = 0
            return carry
        lax.fori_loop(first_pad, seg_end, pad, 0)

    def gather(tile, buf):
        base = tile * TE
        for r in range(TE):
            pltpu.make_async_copy(x1_hbm.at[pl.ds(tok_of_slot[base + r], 1)],
                                  xbuf.at[buf, pl.ds(r, 1)], sem_x.at[buf]).start(priority=r % 2)

    def weight_copies(expert, ws):
        return [pltpu.make_async_copy(src.at[layer, expert], dst.at[ws], sem_w.at[ws])
                for src, dst in ((wg_hbm, wg_st), (wu_hbm, wu_st), (wd_hbm, wd_st))]

    def y_copy(tile, buf):
        return pltpu.make_async_copy(ybuf.at[buf], y_hbm.at[pl.ds(tile * TE, TE)], sem_y.at[buf])

    for c in weight_copies(te_ref[0], 0):
        c.start()
    gather(0, 0)
    for ahead in range(1, X_BUFS - 1):
        @pl.when(n_used > ahead)
        def _(ahead=ahead):
            gather(ahead, ahead)

    def tile_step(j, seg):
        expert = te_ref[j]
        first = (j == 0) | (te_ref[jnp.maximum(j - 1, 0)] != expert)

        @pl.when(j + (X_BUFS - 1) < n_used)
        def _():
            gather(j + (X_BUFS - 1), (j + (X_BUFS - 1)) % X_BUFS)

        @pl.when(first)
        def _():
            ws = seg % 2
            for c in weight_copies(expert, ws):
                c.wait()
            wg_b[...] = wg_st[ws].astype(_bf16)
            wu_b[...] = wu_st[ws].astype(_bf16)
            wd_b[...] = wd_st[ws].astype(_bf16)
            next_first = j + ((count_ref[expert] + (TE - 1)) >> tile_shift)

            @pl.when(next_first < n_used)
            def _():
                for c in weight_copies(te_ref[next_first], 1 - ws):
                    c.start()

        xslot = j % X_BUFS
        pltpu.make_async_copy(x1_hbm.at[pl.ds(0, TE)], xbuf.at[xslot], sem_x.at[xslot]).wait()
        xb = xbuf[xslot].astype(_bf16)
        hg = _dot(xb, wg_b[...])
        hu = _dot(xb, wu_b[...])
        a = hg / (1.0 + jnp.exp(-hg)) * hu
        y = _dot(a.astype(_bf16), wd_b[...])

        yslot = j % Y_BUFS

        @pl.when(j >= Y_BUFS)
        def _():
            y_copy(j - Y_BUFS, yslot).wait()

        ybuf[yslot] = y
        y_copy(j, yslot).start()
        return seg + first.astype(jnp.int32)

    lax.fori_loop(0, n_used, tile_step, jnp.int32(0))

    for back in range(1, Y_BUFS + 1):
        @pl.when(n_used >= back)
        def _(back=back):
            y_copy(n_used - back, (n_used - back) % Y_BUFS).wait()


def _experts_call(layer, tile_expert, n_used, counts, offsets, slot_flat, x1, wg, wu, wd, y_buf):
    any_spec = pl.BlockSpec(memory_space=pl.ANY)
    n_prefetch = 5
    return pl.pallas_call(
        functools.partial(_experts_kernel, layer),
        grid_spec=pltpu.PrefetchScalarGridSpec(
            num_scalar_prefetch=n_prefetch,
            grid=(1,),
            in_specs=[any_spec] * 5,
            out_specs=any_spec,
            scratch_shapes=[
                pltpu.SMEM((N_SLOTS,), jnp.int32),
                pltpu.VMEM((X_BUFS, TE, D_MODEL), _f32),
                pltpu.VMEM((Y_BUFS, TE, D_MODEL), _f32),
                pltpu.VMEM((2, D_MODEL, D_EXPERT), _f32),
                pltpu.VMEM((2, D_MODEL, D_EXPERT), _f32),
                pltpu.VMEM((2, D_EXPERT, D_MODEL), _f32),
                pltpu.VMEM((D_MODEL, D_EXPERT), _bf16),
                pltpu.VMEM((D_MODEL, D_EXPERT), _bf16),
                pltpu.VMEM((D_EXPERT, D_MODEL), _bf16),
                pltpu.SemaphoreType.DMA((X_BUFS,)),
                pltpu.SemaphoreType.DMA((Y_BUFS,)),
                pltpu.SemaphoreType.DMA((2,)),
            ],
        ),
        out_shape=jax.ShapeDtypeStruct((N_SLOTS, D_MODEL), _f32),
        input_output_aliases={n_prefetch + 4: 0},
        compiler_params=pltpu.CompilerParams(
            dimension_semantics=("arbitrary",), vmem_limit_bytes=VMEM_LIMIT),
        name="experts",
    )(tile_expert, n_used, counts, offsets, slot_flat, x1, wg, wu, wd, y_buf)


def _combine_kernel(slot_ref, x1_ref, route_ref, g2_ref, b2_ref, y_hbm, out_p_ref, out_s_ref,
                    ybuf, sems):
    i = pl.program_id(0)
    b = i % 2

    def gather(tile, slot):
        tok0 = tile * TM

        def start(r, carry):
            for k in range(2):
                pltpu.make_async_copy(y_hbm.at[pl.ds(slot_ref[k * N_TOK + tok0 + r], 1)],
                                      ybuf.at[slot, k, pl.ds(r, 1)], sems.at[slot, k]).start(priority=k)
            return carry

        lax.fori_loop(0, TM, start, 0, unroll=16)

    @pl.when(i == 0)
    def _():
        gather(0, 0)

    @pl.when(i + 1 < pl.num_programs(0))
    def _():
        gather(i + 1, 1 - b)

    for k in range(2):
        pltpu.make_async_copy(y_hbm.at[pl.ds(0, TM)], ybuf.at[b, k], sems.at[b, k]).wait()

    route = route_ref[...]
    moe = route[:, 2:3] * ybuf[b, 0] + route[:, 3:4] * ybuf[b, 1]
    out = _layer_norm(ALPHA * x1_ref[...] + moe, g2_ref[...], b2_ref[...])

    @pl.when(i < N_PROMPT_TILES)
    def _():
        out_p_ref[...] = out

    @pl.when(i >= N_PROMPT_TILES)
    def _():
        out_s_ref[...] = out


def _combine_call(slot_flat, x1, route, g2, b2, y):
    tok = lambda i, s: (i, 0)
    const = lambda i, s: (0, 0)
    return pl.pallas_call(
        _combine_kernel,
        grid_spec=pltpu.PrefetchScalarGridSpec(
            num_scalar_prefetch=1,
            grid=(N_TILES,),
            in_specs=[
                pl.BlockSpec((TM, D_MODEL), tok),
                pl.BlockSpec((TM, LANES), tok),
                pl.BlockSpec((1, D_MODEL), const),
                pl.BlockSpec((1, D_MODEL), const),
                pl.BlockSpec(memory_space=pl.ANY),
            ],
            out_specs=[
                pl.BlockSpec((TM, D_MODEL), lambda i, s: (jnp.minimum(i, N_PROMPT_TILES - 1), 0)),
                pl.BlockSpec((TM, D_MODEL), lambda i, s: (jnp.maximum(i - N_PROMPT_TILES, 0), 0)),
            ],
            scratch_shapes=[
                pltpu.VMEM((2, 2, TM, D_MODEL), _f32),
                pltpu.SemaphoreType.DMA((2, 2)),
            ],
        ),
        out_shape=[jax.ShapeDtypeStruct((N_PROMPT, D_MODEL), _f32),
                   jax.ShapeDtypeStruct((N_SAMPLE, D_MODEL), _f32)],
        compiler_params=pltpu.CompilerParams(dimension_semantics=("arbitrary",)),
        name="combine",
    )(slot_flat, x1, route, g2, b2, y)


def kernel(x_prompt, x_sample, state_conv, state_pool, w_in, conv_w, sgu_ln_g, sgu_ln_b, sgu_w, sgu_b, pool_w, pool_scale, w_out, ln1_g, ln1_b, router_group_w, router_group_b, router_expert_w, router_expert_b, expert_w_gate, expert_w_up, expert_w_down, ln2_g, ln2_b):
    y_buf = jnp.zeros((N_SLOTS, D_MODEL), _f32)
    reps = CHUNK // DEC_SEQ
    row = lambda a: a.reshape(DEPTH, 1, -1)

    w_r = jnp.concatenate([router_group_w, router_expert_w], axis=2)
    w_r = jnp.pad(w_r, ((0, 0), (0, 0), (0, LANES - w_r.shape[2])))
    wr_hi = w_r.astype(_bf16)
    wr = jnp.concatenate([wr_hi, (w_r - wr_hi.astype(_f32)).astype(_bf16)], axis=2)
    b_r = jnp.concatenate([router_group_b, router_expert_b], axis=1)
    b_r = row(jnp.pad(b_r, ((0, 0), (0, LANES - b_r.shape[1]))))
    ws_s = jnp.tile(sgu_w[:, :, :DEC_SEQ, :DEC_SEQ], (1, 1, reps, reps))
    bias_p = jnp.swapaxes(sgu_b, 1, 2)
    bias_s = jnp.tile(bias_p[:, :DEC_SEQ], (1, reps, 1))
    zst = jnp.pad(state_conv, ((0, 0), (0, 0), (CONV_HALO - 2, 0), (0, 0)))
    pst = jnp.pad(state_pool, ((0, 0), (0, 0), (POOL_HALO - POOL_BUF, 0), (0, 0)))
    mixer_params = (w_in.astype(_bf16), w_out.astype(_bf16), conv_w, row(sgu_ln_g), row(sgu_ln_b), sgu_w,
                    ws_s, bias_p, bias_s, pool_w, row(pool_scale), row(ln1_g), row(ln1_b), wr, b_r, zst, pst)
    g2, b2 = row(ln2_g), row(ln2_b)

    ztails, ptails, zss, psamps, v_s = [], [], [], [], []
    x_src = (x_prompt.reshape(N_PROMPT, D_MODEL), x_sample.reshape(N_SAMPLE, D_MODEL))
    for l in range(DEPTH):
        x1, route, ztail, ptail, zs, psamp, vns = _mixer_call(l, x_src, mixer_params)
        ids = route[:, 0:2].astype(jnp.int32).T.reshape(ASSIGN_ROWS, LANES)
        slot, tile_expert, n_used, counts, offsets = _slots_call(ids)
        slot_flat = slot.reshape(N_ASSIGN)
        y = _experts_call(l, tile_expert.reshape(LANES), n_used.reshape(LANES)[:1], counts.reshape(LANES),
                          offsets.reshape(LANES), slot_flat, x1, expert_w_gate, expert_w_up, expert_w_down,
                          y_buf)
        x_src = (slot_flat, x1, route, y, g2, b2)
        y_buf = y
        ztails.append(ztail)
        ptails.append(ptail)
        zss.append(zs)
        psamps.append(psamp)
        v_s.append(vns)

    x_p, x_s = _combine_call(slot_flat, x1, route, g2[DEPTH - 1], b2[DEPTH - 1], y)
    y_prompt = x_p.reshape(BATCH, SEQ, D_MODEL)
    y_sample = x_s.reshape(DEC_BATCH, DEC_SEQ, D_MODEL)
    new_conv_prompt = jnp.stack(ztails)[:, :BATCH, CONV_HALO - 2:]
    new_pool_prompt = jnp.stack(ptails)[:, :BATCH, POOL_HALO - POOL_BUF:]
    new_conv_sample = jnp.stack(zss)[:, :, DEC_SEQ - 2:]
    new_pool_sample = jnp.concatenate([state_pool[:, :, DEC_SEQ:], jnp.stack(psamps)], axis=2)
    return (y_prompt, y_sample, new_conv_prompt, new_pool_prompt, new_conv_sample, new_pool_sample,
            jnp.stack(v_s))
```

```python
import functools

import jax
import jax.numpy as jnp
from jax import lax
from jax.experimental import pallas as pl
from jax.experimental.pallas import tpu as pltpu

D_MODEL = 2048
BATCH = 4
SEQ = 2048
DEPTH = 4
DEC_BATCH = 128
DEC_SEQ = 8
PAST_LEN = 16384
HEAD_DIM = 128
CONV_WIDTH = 768
POOL_WIDTH = 512
CHUNK_WIDTH = 768
CHUNK = 128
N_CHUNK_HEADS = 6
POOL_WINDOWS = (2, 4, 8, 16)
POOL_GROUP_DIM = 128
POOL_BUF = 15
IN_WIDTH = 4352
N_EXPERT_GROUPS = 4
EXPERTS_PER_GROUP = 4
N_EXPERTS = 16
D_EXPERT = 512
ALPHA = (2 * DEPTH) ** 0.25
LN_EPS = 1e-5

N_PROMPT = BATCH * SEQ
N_SAMPLE = DEC_BATCH * DEC_SEQ
N_TOK = N_PROMPT + N_SAMPLE
N_ASSIGN = 2 * N_TOK

LANES = 128
SUBLANES = 8
CONV_HALO = 8
POOL_HALO = 16

TM = 256
TILES_PER_SEQ = SEQ // TM
N_PROMPT_TILES = N_PROMPT // TM
N_SAMPLE_TILES = N_SAMPLE // TM
N_TILES = N_PROMPT_TILES + N_SAMPLE_TILES
SEQS_PER_TILE = TM // DEC_SEQ

TE = 256
N_SLOTS = N_ASSIGN + N_EXPERTS * TE
N_ETILES = N_SLOTS // TE
ASSIGN_ROWS = N_ASSIGN // LANES

CUT_B, CUT_C, CUT_H, CUT_U, CUT_V, CUT_P = 0, 768, 1536, 2304, 3072, 3840

VMEM_LIMIT = 62 * 1024 * 1024

_f32 = jnp.float32
_bf16 = jnp.bfloat16


def _dot(a, b):
    return jnp.dot(a, b, preferred_element_type=_f32)


def _layer_norm(r, g, b):
    mu = jnp.mean(r, axis=-1, keepdims=True)
    c = r - mu
    var = jnp.mean(c * c, axis=-1, keepdims=True)
    return c * lax.rsqrt(var + LN_EPS) * g + b


class _RowGather:
    def __init__(self, start_row, first_row):
        self._start_row = start_row
        self._done = first_row

    def upto(self, row_end):
        for r in range(self._done, row_end):
            self._start_row(r)
        self._done = max(self._done, row_end)


class _NoGather:
    def upto(self, row_end):
        del row_end


BRANCH_GATHER_ROWS = 192


def _mixer_body(xb, w_in_ref, conv_w_ref, sg_ref, sb_ref, ws_ref, bias_ref, pw_ref, ps_ref,
                zext_ref, pext_ref, ybuf_ref, seq_len, pos, chunk_len, gather):
    n_seq = TM // seq_len

    z = _dot(xb, w_in_ref[:, CUT_C:CUT_H]) * _dot(xb, w_in_ref[:, CUT_H:CUT_U])
    zext_ref[:, CONV_HALO:, :] = z.reshape(n_seq, seq_len, CONV_WIDTH)
    gather.upto(16)
    cw = conv_w_ref[...]
    conv = (cw[0:1, :] * zext_ref[:, CONV_HALO - 2:CONV_HALO - 2 + seq_len, :]
            + cw[1:2, :] * zext_ref[:, CONV_HALO - 1:CONV_HALO - 1 + seq_len, :]
            + cw[2:3, :] * zext_ref[:, CONV_HALO:, :])
    y_conv = _dot(xb, w_in_ref[:, CUT_B:CUT_C]) * conv.reshape(TM, CONV_WIDTH)
    ybuf_ref[:, 0:CONV_WIDTH] = y_conv.astype(_bf16)
    gather.upto(32)

    v = _dot(xb, w_in_ref[:, CUT_V:CUT_P])
    vn = _layer_norm(v, sg_ref[...], sb_ref[...])
    vnb = vn.astype(_bf16)
    gather.upto(48)
    r_i = lax.broadcasted_iota(jnp.int32, (CHUNK, CHUNK), 0)
    c_i = lax.broadcasted_iota(jnp.int32, (CHUNK, CHUNK), 1)
    shift = chunk_len.bit_length() - 1
    mask = (c_i <= r_i) & ((r_i >> shift) == (c_i >> shift))
    u = _dot(xb, w_in_ref[:, CUT_U:CUT_V])
    gather.upto(64)
    n_chunks = TM // CHUNK
    for h in range(N_CHUNK_HEADS):
        m_h = jnp.where(mask, ws_ref[h], 0.0).astype(_bf16)
        b_h = bias_ref[:, h:h + 1]
        cols = slice(h * HEAD_DIM, (h + 1) * HEAD_DIM)
        vn_h = jnp.concatenate([vnb[c * CHUNK:(c + 1) * CHUNK, cols] for c in range(n_chunks)], axis=1)
        mixed = _dot(m_h, vn_h)
        for c in range(n_chunks):
            rows = slice(c * CHUNK, (c + 1) * CHUNK)
            ybuf_ref[rows, CONV_WIDTH + h * HEAD_DIM:CONV_WIDTH + (h + 1) * HEAD_DIM] = (
                u[rows, cols] * (mixed[:, c * HEAD_DIM:(c + 1) * HEAD_DIM] + b_h)).astype(_bf16)
        gather.upto(64 + 8 * (h + 1))

    p = _dot(xb, w_in_ref[:, CUT_P:IN_WIDTH])
    pext_ref[:, POOL_HALO:, :] = p.reshape(n_seq, seq_len, POOL_WIDTH)
    gather.upto(128)
    posf = (pos + 1).astype(_f32)
    for g, w in enumerate(POOL_WINDOWS):
        cols = slice(g * POOL_GROUP_DIM, (g + 1) * POOL_GROUP_DIM)
        s = pext_ref[:, POOL_HALO:, cols]
        for j in range(1, w):
            s = s + pext_ref[:, POOL_HALO - j:POOL_HALO - j + seq_len, cols]
        cnt = jnp.minimum(float(w), posf)
        diff = s.reshape(TM, POOL_GROUP_DIM) / cnt - p[:, cols]
        y_g = _dot(diff.astype(_bf16), pw_ref[g].astype(_bf16)) * ps_ref[:, cols]
        ybuf_ref[:, CONV_WIDTH + CHUNK_WIDTH + g * POOL_GROUP_DIM:
                 CONV_WIDTH + CHUNK_WIDTH + (g + 1) * POOL_GROUP_DIM] = y_g.astype(_bf16)
        gather.upto(128 + 16 * (g + 1))
    assert 128 + 16 * len(POOL_WINDOWS) == BRANCH_GATHER_ROWS
    return z, p, vn


def _route(x1, wr_ref, br_ref):
    x_hi = x1.astype(_bf16)
    x_lo = (x1 - x_hi.astype(_f32)).astype(_bf16)
    hi_parts = _dot(x_hi, wr_ref[...])
    logits = hi_parts[:, :LANES] + hi_parts[:, LANES:] + _dot(x_lo, wr_ref[:, :LANES]) + br_ref[...]
    lane = lax.broadcasted_iota(jnp.int32, logits.shape, 1)
    neg = -jnp.inf
    big = jnp.int32(1 << 20)

    gmask = lane < N_EXPERT_GROUPS
    gl = jnp.where(gmask, logits, neg)
    g_max = jnp.max(gl, axis=1, keepdims=True)
    g_idx = jnp.min(jnp.where(gmask & (gl == g_max), lane, big), axis=1, keepdims=True)
    g_val = 1.0 / jnp.sum(jnp.exp(gl - g_max), axis=1, keepdims=True)

    e_lane = lane - N_EXPERT_GROUPS
    emask = (e_lane >= 0) & (e_lane < N_EXPERTS) & ((e_lane >> (EXPERTS_PER_GROUP.bit_length() - 1)) == g_idx)
    el = jnp.where(emask, logits, neg)
    e_max = jnp.max(el, axis=1, keepdims=True)
    i1 = jnp.min(jnp.where(emask & (el == e_max), e_lane, big), axis=1, keepdims=True)
    e_sum = jnp.sum(jnp.exp(el - e_max), axis=1, keepdims=True)
    mask2 = emask & (e_lane != i1)
    e_second = jnp.max(jnp.where(mask2, el, neg), axis=1, keepdims=True)
    i2 = jnp.min(jnp.where(mask2 & (el == e_second), e_lane, big), axis=1, keepdims=True)
    v1 = 1.0 / e_sum
    v2 = jnp.exp(e_second - e_max) / e_sum
    tot = v1 + v2
    gate1 = g_val * (v1 / tot)
    gate2 = g_val * (v2 / tot)
    return jnp.where(lane == 0, i1.astype(_f32),
                     jnp.where(lane == 1, i2.astype(_f32),
                               jnp.where(lane == 2, gate1, jnp.where(lane == 3, gate2, 0.0))))


N_MIXER_COMMON_REFS = 29


def _mixer_kernel(fused, *refs):
    if fused:
        slot_ref, x1p_ref, routep_ref, y_hbm, g2_ref, b2_ref = refs[:6]
        refs = refs[6:]
    else:
        xp_ref, xsamp_ref = refs[:2]
        refs = refs[2:]
    (w_in_ref, w_out_ref, conv_w_ref, sg_ref, sb_ref, ws_p_ref, ws_s_ref, bias_p_ref, bias_s_ref,
     pw_ref, ps_ref, g1_ref, b1_ref, wr_ref, br_ref, zst_ref, pst_ref,
     x1_ref, route_ref, ztail_ref, ptail_ref, zs_ref, psamp_ref, vns_ref,
     zext_p, pext_p, zext_s, pext_s, ybuf_ref) = refs[:N_MIXER_COMMON_REFS]
    i = pl.program_id(0)
    last = pl.num_programs(0) - 1

    if fused:
        gbuf, sems = refs[N_MIXER_COMMON_REFS:]
        b = i % 2

        def row_starter(tile, half):
            tok0 = tile * TM

            def start_row(r):
                for k in range(2):
                    pltpu.make_async_copy(y_hbm.at[pl.ds(slot_ref[k * N_TOK + tok0 + r], 1)],
                                          gbuf.at[half, k, pl.ds(r, 1)], sems.at[half, k]).start(priority=k)
            return start_row

        def wait_rows(half):
            for k in range(2):
                pltpu.make_async_copy(y_hbm.at[pl.ds(0, TM)], gbuf.at[half, k], sems.at[half, k]).wait()

        @pl.when(i == 0)
        def _():
            first = row_starter(0, 0)

            def body(r, carry):
                first(r)
                return carry

            lax.fori_loop(0, TM, body, 0, unroll=16)

        wait_rows(b)
        routep = routep_ref[...]
        moe = routep[:, 2:3] * gbuf[b, 0] + routep[:, 3:4] * gbuf[b, 1]
        x = _layer_norm(ALPHA * x1p_ref[...] + moe, g2_ref[...], b2_ref[...])
        next_rows = row_starter(jnp.minimum(i + 1, last), 1 - b)
        new_gather = lambda first_row: _RowGather(next_rows, first_row)
    else:
        x = jnp.where(i < N_PROMPT_TILES, xp_ref[...], xsamp_ref[...])
        new_gather = lambda first_row: _NoGather()

    xb = x.astype(_bf16)
    row = lax.broadcasted_iota(jnp.int32, (TM, 1), 0)

    @pl.when(i < N_PROMPT_TILES)
    def _prompt():
        t0 = (i % TILES_PER_SEQ) * TM

        @pl.when(t0 == 0)
        def _():
            zext_p[:, 0:CONV_HALO, :] = jnp.zeros((1, CONV_HALO, CONV_WIDTH), _f32)
            pext_p[:, 0:POOL_HALO, :] = jnp.zeros((1, POOL_HALO, POOL_WIDTH), _f32)

        z, p, _ = _mixer_body(xb, w_in_ref, conv_w_ref, sg_ref, sb_ref, ws_p_ref, bias_p_ref,
                              pw_ref, ps_ref, zext_p, pext_p, ybuf_ref, TM, t0 + row, CHUNK, new_gather(0))
        z_last = z[TM - CONV_HALO:, :].reshape(1, CONV_HALO, CONV_WIDTH)
        p_last = p[TM - POOL_HALO:, :].reshape(1, POOL_HALO, POOL_WIDTH)
        zext_p[:, 0:CONV_HALO, :] = z_last
        pext_p[:, 0:POOL_HALO, :] = p_last
        ztail_ref[...] = z_last
        ptail_ref[...] = p_last

    @pl.when(i >= N_PROMPT_TILES)
    def _sample():
        zext_s[:, 0:CONV_HALO, :] = zst_ref[...]
        pext_s[:, 0:POOL_HALO, :] = pst_ref[...]
        z, p, vn = _mixer_body(xb, w_in_ref, conv_w_ref, sg_ref, sb_ref, ws_s_ref, bias_s_ref,
                               pw_ref, ps_ref, zext_s, pext_s, ybuf_ref, DEC_SEQ,
                               PAST_LEN + (row & (DEC_SEQ - 1)), DEC_SEQ, new_gather(0))
        zs_ref[...] = z.reshape(SEQS_PER_TILE, DEC_SEQ, CONV_WIDTH)
        psamp_ref[...] = p.reshape(SEQS_PER_TILE, DEC_SEQ, POOL_WIDTH)
        vns_ref[...] = vn.reshape(SEQS_PER_TILE, DEC_SEQ, CHUNK_WIDTH)
        ztail_ref[...] = jnp.zeros(ztail_ref.shape, _f32)
        ptail_ref[...] = jnp.zeros(ptail_ref.shape, _f32)

    tail_gather = new_gather(BRANCH_GATHER_ROWS)
    mix = _dot(ybuf_ref[...], w_out_ref[...])
    tail_gather.upto(BRANCH_GATHER_ROWS + (TM - BRANCH_GATHER_ROWS) // 2)
    x1 = _layer_norm(ALPHA * x + mix, g1_ref[...], b1_ref[...])
    x1_ref[...] = x1
    tail_gather.upto(TM)
    route_ref[...] = _route(x1, wr_ref, br_ref)

    if fused:
        @pl.when(i == last)
        def _():
            wait_rows(1 - b)


def _layer_spec(layer, shape):
    nd = len(shape)
    return pl.BlockSpec((None,) + shape, lambda i, *_: (layer,) + (0,) * nd, pipeline_mode=pl.Buffered(1))


def _mixer_call(layer, x_src, params):
    fused = len(x_src) == 6
    samp = lambda i, *_: (jnp.maximum(i - N_PROMPT_TILES, 0), 0, 0)
    samp_l = lambda i, *_: (layer, jnp.maximum(i - N_PROMPT_TILES, 0), 0, 0)
    tail = lambda i, *_: (i // TILES_PER_SEQ, 0, 0)
    tok = lambda i, *_: (i, 0)
    if fused:
        x_specs = [
            pl.BlockSpec((TM, D_MODEL), tok),
            pl.BlockSpec((TM, LANES), tok),
            pl.BlockSpec(memory_space=pl.ANY),
            _layer_spec(layer - 1, (1, D_MODEL)),
            _layer_spec(layer - 1, (1, D_MODEL)),
        ]
    else:
        x_specs = [
            pl.BlockSpec((TM, D_MODEL), lambda i, *_: (jnp.minimum(i, N_PROMPT_TILES - 1), 0)),
            pl.BlockSpec((TM, D_MODEL), lambda i, *_: (jnp.maximum(i - N_PROMPT_TILES, 0), 0)),
        ]
    in_specs = x_specs + [
        _layer_spec(layer, (D_MODEL, IN_WIDTH)),
        _layer_spec(layer, (D_MODEL, D_MODEL)),
        _layer_spec(layer, (3, CONV_WIDTH)),
        _layer_spec(layer, (1, CHUNK_WIDTH)),
        _layer_spec(layer, (1, CHUNK_WIDTH)),
        _layer_spec(layer, (N_CHUNK_HEADS, CHUNK, CHUNK)),
        _layer_spec(layer, (N_CHUNK_HEADS, CHUNK, CHUNK)),
        _layer_spec(layer, (CHUNK, N_CHUNK_HEADS)),
        _layer_spec(layer, (CHUNK, N_CHUNK_HEADS)),
        _layer_spec(layer, (4, POOL_GROUP_DIM, POOL_GROUP_DIM)),
        _layer_spec(layer, (1, POOL_WIDTH)),
        _layer_spec(layer, (1, D_MODEL)),
        _layer_spec(layer, (1, D_MODEL)),
        _layer_spec(layer, (D_MODEL, 2 * LANES)),
        _layer_spec(layer, (1, LANES)),
        pl.BlockSpec((None, SEQS_PER_TILE, CONV_HALO, CONV_WIDTH), samp_l),
        pl.BlockSpec((None, SEQS_PER_TILE, POOL_HALO, POOL_WIDTH), samp_l),
    ]
    out_specs = [
        pl.BlockSpec((TM, D_MODEL), tok),
        pl.BlockSpec((TM, LANES), tok),
        pl.BlockSpec((1, CONV_HALO, CONV_WIDTH), tail),
        pl.BlockSpec((1, POOL_HALO, POOL_WIDTH), tail),
        pl.BlockSpec((SEQS_PER_TILE, DEC_SEQ, CONV_WIDTH), samp),
        pl.BlockSpec((SEQS_PER_TILE, DEC_SEQ, POOL_WIDTH), samp),
        pl.BlockSpec((SEQS_PER_TILE, DEC_SEQ, CHUNK_WIDTH), samp),
    ]
    out_shape = [
        jax.ShapeDtypeStruct((N_TOK, D_MODEL), _f32),
        jax.ShapeDtypeStruct((N_TOK, LANES), _f32),
        jax.ShapeDtypeStruct((BATCH + 1, CONV_HALO, CONV_WIDTH), _f32),
        jax.ShapeDtypeStruct((BATCH + 1, POOL_HALO, POOL_WIDTH), _f32),
        jax.ShapeDtypeStruct((DEC_BATCH, DEC_SEQ, CONV_WIDTH), _f32),
        jax.ShapeDtypeStruct((DEC_BATCH, DEC_SEQ, POOL_WIDTH), _f32),
        jax.ShapeDtypeStruct((DEC_BATCH, DEC_SEQ, CHUNK_WIDTH), _f32),
    ]
    scratch = [
        pltpu.VMEM((1, CONV_HALO + TM, CONV_WIDTH), _f32),
        pltpu.VMEM((1, POOL_HALO + TM, POOL_WIDTH), _f32),
        pltpu.VMEM((SEQS_PER_TILE, CONV_HALO + DEC_SEQ, CONV_WIDTH), _f32),
        pltpu.VMEM((SEQS_PER_TILE, POOL_HALO + DEC_SEQ, POOL_WIDTH), _f32),
        pltpu.VMEM((TM, D_MODEL), _bf16),
    ]
    assert len(in_specs) - len(x_specs) + len(out_specs) + len(scratch) == N_MIXER_COMMON_REFS
    if fused:
        scratch += [pltpu.VMEM((2, 2, TM, D_MODEL), _f32), pltpu.SemaphoreType.DMA((2, 2))]
    return pl.pallas_call(
        functools.partial(_mixer_kernel, fused),
        grid_spec=pltpu.PrefetchScalarGridSpec(
            num_scalar_prefetch=1 if fused else 0,
            grid=(N_TILES,),
            in_specs=in_specs,
            out_specs=out_specs,
            scratch_shapes=scratch,
        ),
        out_shape=out_shape,
        compiler_params=pltpu.CompilerParams(
            dimension_semantics=("arbitrary",), vmem_limit_bytes=VMEM_LIMIT),
        name="mixer",
    )(*x_src, *params)


def _slots_kernel(ids_ref, slot_ref, tile_expert_ref, n_used_ref, count_ref, offset_ref):
    ids = ids_ref[...]
    lane = lax.broadcasted_iota(jnp.int32, (1, LANES), 1)
    counts = jnp.zeros((1, LANES), _f32)
    offsets = jnp.zeros((1, LANES), _f32)
    r_i = lax.broadcasted_iota(jnp.int32, (LANES, LANES), 0)
    c_i = lax.broadcasted_iota(jnp.int32, (LANES, LANES), 1)
    upper = (r_i < c_i).astype(_bf16)
    rr = lax.broadcasted_iota(jnp.int32, (ASSIGN_ROWS, ASSIGN_ROWS), 0)
    rc = lax.broadcasted_iota(jnp.int32, (ASSIGN_ROWS, ASSIGN_ROWS), 1)
    lower = (rc < rr).astype(_bf16)
    tile_row0 = (lax.broadcasted_iota(jnp.int32, (1, LANES), 1) * TE).astype(_f32)

    slot = jnp.zeros((ASSIGN_ROWS, LANES), _f32)
    tile_expert = jnp.zeros((1, LANES), _f32)
    off = jnp.zeros((1, 1), _f32)
    for e in range(N_EXPERTS):
        m = (ids == e).astype(_f32)
        within = _dot(m.astype(_bf16), upper)
        rowsum = jnp.sum(m, axis=1, keepdims=True)
        rowpre = _dot(lower, jnp.broadcast_to(rowsum, (ASSIGN_ROWS, LANES)).astype(_bf16))
        cnt = jnp.sum(rowsum, axis=0, keepdims=True)
        padded = jnp.floor((cnt + (TE - 1)) * (1.0 / TE)) * TE
        slot = slot + m * (off + rowpre + within)
        in_seg = (tile_row0 >= off) & (tile_row0 < off + padded)
        tile_expert = tile_expert + jnp.where(in_seg, float(e), 0.0)
        counts = counts + jnp.where(lane == e, cnt, 0.0)
        offsets = offsets + jnp.where(lane == e, off, 0.0)
        off = off + padded
    tile_expert = jnp.where(tile_row0 >= off, float(N_EXPERTS - 1), tile_expert)
    slot_ref[...] = slot.astype(jnp.int32)
    tile_expert_ref[...] = tile_expert.astype(jnp.int32)
    n_used_ref[...] = jnp.broadcast_to(off * (1.0 / TE), (1, LANES)).astype(jnp.int32)
    count_ref[...] = counts.astype(jnp.int32)
    offset_ref[...] = offsets.astype(jnp.int32)


def _slots_call(ids):
    return pl.pallas_call(
        _slots_kernel,
        out_shape=[jax.ShapeDtypeStruct((ASSIGN_ROWS, LANES), jnp.int32)]
        + [jax.ShapeDtypeStruct((1, LANES), jnp.int32)] * 4,
        name="slots",
    )(ids)


X_BUFS = 6
Y_BUFS = 4


def _experts_kernel(layer, te_ref, nu_ref, count_ref, offset_ref, slot_ref,
                    x1_hbm, wg_hbm, wu_hbm, wd_hbm, y_in_hbm, y_hbm,
                    tok_of_slot, xbuf, ybuf, wg_st, wu_st, wd_st, wg_b, wu_b, wd_b, sem_x, sem_y, sem_w):
    del y_in_hbm
    n_used = nu_ref[0]
    tile_shift = TE.bit_length() - 1

    for k in range(2):
        def fill(t, carry, k=k):
            tok_of_slot[slot_ref[k * N_TOK + t]] = t
            return carry
        lax.fori_loop(0, N_TOK, fill, 0, unroll=16)
    for e in range(N_EXPERTS):
        first_pad = offset_ref[e] + count_ref[e]
        seg_end = offset_ref[e] + (((count_ref[e] + (TE - 1)) >> tile_shift) << tile_shift)

        def pad(s, carry):
            tok_of_slot[s] = 0
            return carry
        lax.fori_loop(first_pad, seg_end, pad, 0)

    def gather(tile, buf):
        base = tile * TE
        for r in range(TE):
            pltpu.make_async_copy(x1_hbm.at[pl.ds(tok_of_slot[base + r], 1)],
                                  xbuf.at[buf, pl.ds(r, 1)], sem_x.at[buf]).start(priority=r % 2)

    def weight_copies(expert, ws):
        return [pltpu.make_async_copy(src.at[layer, expert], dst.at[ws], sem_w.at[ws])
                for src, dst in ((wg_hbm, wg_st), (wu_hbm, wu_st), (wd_hbm, wd_st))]

    def y_copy(tile, buf):
        return pltpu.make_async_copy(ybuf.at[buf], y_hbm.at[pl.ds(tile * TE, TE)], sem_y.at[buf])

    for c in weight_copies(te_ref[0], 0):
        c.start()
    gather(0, 0)
    for ahead in range(1, X_BUFS - 1):
        @pl.when(n_used > ahead)
        def _(ahead=ahead):
            gather(ahead, ahead)

    def tile_step(j, seg):
        expert = te_ref[j]
        first = (j == 0) | (te_ref[jnp.maximum(j - 1, 0)] != expert)

        @pl.when(j + (X_BUFS - 1) < n_used)
        def _():
            gather(j + (X_BUFS - 1), (j + (X_BUFS - 1)) % X_BUFS)

        @pl.when(first)
        def _():
            ws = seg % 2
            for c in weight_copies(expert, ws):
                c.wait()
            wg_b[...] = wg_st[ws].astype(_bf16)
            wu_b[...] = wu_st[ws].astype(_bf16)
            wd_b[...] = wd_st[ws].astype(_bf16)
            next_first = j + ((count_ref[expert] + (TE - 1)) >> tile_shift)

            @pl.when(next_first < n_used)
            def _():
                for c in weight_copies(te_ref[next_first], 1 - ws):
                    c.start()

        xslot = j % X_BUFS
        pltpu.make_async_copy(x1_hbm.at[pl.ds(0, TE)], xbuf.at[xslot], sem_x.at[xslot]).wait()
        xb = xbuf[xslot].astype(_bf16)
        hg = _dot(xb, wg_b[...])
        hu = _dot(xb, wu_b[...])
        a = hg / (1.0 + jnp.exp(-hg)) * hu
        y = _dot(a.astype(_bf16), wd_b[...])

        yslot = j % Y_BUFS

        @pl.when(j >= Y_BUFS)
        def _():
            y_copy(j - Y_BUFS, yslot).wait()

        ybuf[yslot] = y
        y_copy(j, yslot).start()
        return seg + first.astype(jnp.int32)

    lax.fori_loop(0, n_used, tile_step, jnp.int32(0))

    for back in range(1, Y_BUFS + 1):
        @pl.when(n_used >= back)
        def _(back=back):
            y_copy(n_used - back, (n_used - back) % Y_BUFS).wait()


def _experts_call(layer, tile_expert, n_used, counts, offsets, slot_flat, x1, wg, wu, wd, y_buf):
    any_spec = pl.BlockSpec(memory_space=pl.ANY)
    n_prefetch = 5
    return pl.pallas_call(
        functools.partial(_experts_kernel, layer),
        grid_spec=pltpu.PrefetchScalarGridSpec(
            num_scalar_prefetch=n_prefetch,
            grid=(1,),
            in_specs=[any_spec] * 5,
            out_specs=any_spec,
            scratch_shapes=[
                pltpu.SMEM((N_SLOTS,), jnp.int32),
                pltpu.VMEM((X_BUFS, TE, D_MODEL), _f32),
                pltpu.VMEM((Y_BUFS, TE, D_MODEL), _f32),
                pltpu.VMEM((2, D_MODEL, D_EXPERT), _f32),
                pltpu.VMEM((2, D_MODEL, D_EXPERT), _f32),
                pltpu.VMEM((2, D_EXPERT, D_MODEL), _f32),
                pltpu.VMEM((D_MODEL, D_EXPERT), _bf16),
                pltpu.VMEM((D_MODEL, D_EXPERT), _bf16),
                pltpu.VMEM((D_EXPERT, D_MODEL), _bf16),
                pltpu.SemaphoreType.DMA((X_BUFS,)),
                pltpu.SemaphoreType.DMA((Y_BUFS,)),
                pltpu.SemaphoreType.DMA((2,)),
            ],
        ),
        out_shape=jax.ShapeDtypeStruct((N_SLOTS, D_MODEL), _f32),
        input_output_aliases={n_prefetch + 4: 0},
        compiler_params=pltpu.CompilerParams(
            dimension_semantics=("arbitrary",), vmem_limit_bytes=VMEM_LIMIT),
        name="experts",
    )(tile_expert, n_used, counts, offsets, slot_flat, x1, wg, wu, wd, y_buf)


COMBINE_BUFS = 4


def _combine_kernel(slot_ref, x1_ref, route_ref, g2_ref, b2_ref, y_hbm, out_p_ref, out_s_ref,
                    ybuf, sems):
    i = pl.program_id(0)
    b = i % COMBINE_BUFS

    def gather(tile, slot):
        tok0 = tile * TM

        def start(r, carry):
            for k in range(2):
                pltpu.make_async_copy(y_hbm.at[pl.ds(slot_ref[k * N_TOK + tok0 + r], 1)],
                                      ybuf.at[slot, k, pl.ds(r, 1)], sems.at[slot, k]).start(priority=k)
            return carry

        lax.fori_loop(0, TM, start, 0, unroll=16)

    @pl.when(i == 0)
    def _():
        for ahead in range(COMBINE_BUFS - 1):
            gather(ahead, ahead)

    @pl.when(i + (COMBINE_BUFS - 1) < pl.num_programs(0))
    def _():
        gather(i + (COMBINE_BUFS - 1), (i + (COMBINE_BUFS - 1)) % COMBINE_BUFS)

    for k in range(2):
        pltpu.make_async_copy(y_hbm.at[pl.ds(0, TM)], ybuf.at[b, k], sems.at[b, k]).wait()

    route = route_ref[...]
    moe = route[:, 2:3] * ybuf[b, 0] + route[:, 3:4] * ybuf[b, 1]
    out = _layer_norm(ALPHA * x1_ref[...] + moe, g2_ref[...], b2_ref[...])

    @pl.when(i < N_PROMPT_TILES)
    def _():
        out_p_ref[...] = out

    @pl.when(i >= N_PROMPT_TILES)
    def _():
        out_s_ref[...] = out


def _combine_call(slot_flat, x1, route, g2, b2, y):
    tok = lambda i, s: (i, 0)
    const = lambda i, s: (0, 0)
    return pl.pallas_call(
        _combine_kernel,
        grid_spec=pltpu.PrefetchScalarGridSpec(
            num_scalar_prefetch=1,
            grid=(N_TILES,),
            in_specs=[
                pl.BlockSpec((TM, D_MODEL), tok),
                pl.BlockSpec((TM, LANES), tok),
                pl.BlockSpec((1, D_MODEL), const),
                pl.BlockSpec((1, D_MODEL), const),
                pl.BlockSpec(memory_space=pl.ANY),
            ],
            out_specs=[
                pl.BlockSpec((TM, D_MODEL), lambda i, s: (jnp.minimum(i, N_PROMPT_TILES - 1), 0)),
                pl.BlockSpec((TM, D_MODEL), lambda i, s: (jnp.maximum(i - N_PROMPT_TILES, 0), 0)),
            ],
            scratch_shapes=[
                pltpu.VMEM((COMBINE_BUFS, 2, TM, D_MODEL), _f32),
                pltpu.SemaphoreType.DMA((COMBINE_BUFS, 2)),
            ],
        ),
        out_shape=[jax.ShapeDtypeStruct((N_PROMPT, D_MODEL), _f32),
                   jax.ShapeDtypeStruct((N_SAMPLE, D_MODEL), _f32)],
        compiler_params=pltpu.CompilerParams(
            dimension_semantics=("arbitrary",), vmem_limit_bytes=VMEM_LIMIT),
        name="combine",
    )(slot_flat, x1, route, g2, b2, y)


def kernel(x_prompt, x_sample, state_conv, state_pool, w_in, conv_w, sgu_ln_g, sgu_ln_b, sgu_w, sgu_b, pool_w, pool_scale, w_out, ln1_g, ln1_b, router_group_w, router_group_b, router_expert_w, router_expert_b, expert_w_gate, expert_w_up, expert_w_down, ln2_g, ln2_b):
    y_buf = jnp.zeros((N_SLOTS, D_MODEL), _f32)
    reps = CHUNK // DEC_SEQ
    row = lambda a: a.reshape(DEPTH, 1, -1)

    w_r = jnp.concatenate([router_group_w, router_expert_w], axis=2)
    w_r = jnp.pad(w_r, ((0, 0), (0, 0), (0, LANES - w_r.shape[2])))
    wr_hi = w_r.astype(_bf16)
    wr = jnp.concatenate([wr_hi, (w_r - wr_hi.astype(_f32)).astype(_bf16)], axis=2)
    b_r = jnp.concatenate([router_group_b, router_expert_b], axis=1)
    b_r = row(jnp.pad(b_r, ((0, 0), (0, LANES - b_r.shape[1]))))
    ws_s = jnp.tile(sgu_w[:, :, :DEC_SEQ, :DEC_SEQ], (1, 1, reps, reps))
    bias_p = jnp.swapaxes(sgu_b, 1, 2)
    bias_s = jnp.tile(bias_p[:, :DEC_SEQ], (1, reps, 1))
    zst = jnp.pad(state_conv, ((0, 0), (0, 0), (CONV_HALO - 2, 0), (0, 0)))
    pst = jnp.pad(state_pool, ((0, 0), (0, 0), (POOL_HALO - POOL_BUF, 0), (0, 0)))
    mixer_params = (w_in.astype(_bf16), w_out.astype(_bf16), conv_w, row(sgu_ln_g), row(sgu_ln_b), sgu_w,
                    ws_s, bias_p, bias_s, pool_w, row(pool_scale), row(ln1_g), row(ln1_b), wr, b_r, zst, pst)
    g2, b2 = row(ln2_g), row(ln2_b)

    ztails, ptails, zss, psamps, v_s = [], [], [], [], []
    x_src = (x_prompt.reshape(N_PROMPT, D_MODEL), x_sample.reshape(N_SAMPLE, D_MODEL))
    for l in range(DEPTH):
        x1, route, ztail, ptail, zs, psamp, vns = _mixer_call(l, x_src, mixer_params)
        ids = route[:, 0:2].astype(jnp.int32).T.reshape(ASSIGN_ROWS, LANES)
        slot, tile_expert, n_used, counts, offsets = _slots_call(ids)
        slot_flat = slot.reshape(N_ASSIGN)
        y = _experts_call(l, tile_expert.reshape(LANES), n_used.reshape(LANES)[:1], counts.reshape(LANES),
                          offsets.reshape(LANES), slot_flat, x1, expert_w_gate, expert_w_up, expert_w_down,
                          y_buf)
        x_src = (slot_flat, x1, route, y, g2, b2)
        y_buf = y
        ztails.append(ztail)
        ptails.append(ptail)
        zss.append(zs)
        psamps.append(psamp)
        v_s.append(vns)

    x_p, x_s = _combine_call(slot_flat, x1, route, g2[DEPTH - 1], b2[DEPTH - 1], y)
    y_prompt = x_p.reshape(BATCH, SEQ, D_MODEL)
    y_sample = x_s.reshape(DEC_BATCH, DEC_SEQ, D_MODEL)
    new_conv_prompt = jnp.stack(ztails)[:, :BATCH, CONV_HALO - 2:]
    new_pool_prompt = jnp.stack(ptails)[:, :BATCH, POOL_HALO - POOL_BUF:]
    new_conv_sample = jnp.stack(zss)[:, :, DEC_SEQ - 2:]
    new_pool_sample = jnp.concatenate([state_pool[:, :, DEC_SEQ:], jnp.stack(psamps)], axis=2)
    return (y_prompt, y_sample, new_conv_prompt, new_pool_prompt, new_conv_sample, new_pool_sample,
            jnp.stack(v_s))
```

```python
import functools

import jax
import jax.numpy as jnp
from jax import lax
from jax.experimental import pallas as pl
from jax.experimental.pallas import tpu as pltpu

D_MODEL = 2048
BATCH = 4
SEQ = 2048
DEPTH = 4
DEC_BATCH = 128
DEC_SEQ = 8
PAST_LEN = 16384
HEAD_DIM = 128
CONV_WIDTH = 768
POOL_WIDTH = 512
CHUNK_WIDTH = 768
CHUNK = 128
N_CHUNK_HEADS = 6
POOL_WINDOWS = (2, 4, 8, 16)
POOL_GROUP_DIM = 128
POOL_BUF = 15
IN_WIDTH = 4352
N_EXPERT_GROUPS = 4
EXPERTS_PER_GROUP = 4
N_EXPERTS = 16
D_EXPERT = 512
ALPHA = (2 * DEPTH) ** 0.25
LN_EPS = 1e-5

N_PROMPT = BATCH * SEQ
N_SAMPLE = DEC_BATCH * DEC_SEQ
N_TOK = N_PROMPT + N_SAMPLE
N_ASSIGN = 2 * N_TOK

LANES = 128
SUBLANES = 8
CONV_HALO = 8
POOL_HALO = 16

TM = 256
TILES_PER_SEQ = SEQ // TM
N_PROMPT_TILES = N_PROMPT // TM
N_SAMPLE_TILES = N_SAMPLE // TM
N_TILES = N_PROMPT_TILES + N_SAMPLE_TILES
SEQS_PER_TILE = TM // DEC_SEQ

TE = 256
N_SLOTS = N_ASSIGN + N_EXPERTS * TE
N_ETILES = N_SLOTS // TE
ASSIGN_ROWS = N_ASSIGN // LANES

CUT_B, CUT_C, CUT_H, CUT_U, CUT_V, CUT_P = 0, 768, 1536, 2304, 3072, 3840

VMEM_LIMIT = 62 * 1024 * 1024

_f32 = jnp.float32
_bf16 = jnp.bfloat16


def _dot(a, b):
    return jnp.dot(a, b, preferred_element_type=_f32)


def _layer_norm(r, g, b):
    mu = jnp.mean(r, axis=-1, keepdims=True)
    c = r - mu
    var = jnp.mean(c * c, axis=-1, keepdims=True)
    return c * lax.rsqrt(var + LN_EPS) * g + b


class _RowGather:
    def __init__(self, start_row, first_row):
        self._start_row = start_row
        self._done = first_row

    def upto(self, row_end):
        for r in range(self._done, row_end):
            self._start_row(r)
        self._done = max(self._done, row_end)


class _NoGather:
    def upto(self, row_end):
        del row_end


BRANCH_GATHER_ROWS = 192


def _mixer_body(xb, w_in_ref, conv_w_ref, sg_ref, sb_ref, ws_ref, bias_ref, pw_ref, ps_ref,
                zext_ref, pext_ref, ybuf_ref, seq_len, pos, chunk_len, gather):
    n_seq = TM // seq_len

    z = _dot(xb, w_in_ref[:, CUT_C:CUT_H]) * _dot(xb, w_in_ref[:, CUT_H:CUT_U])
    zext_ref[:, CONV_HALO:, :] = z.reshape(n_seq, seq_len, CONV_WIDTH)
    gather.upto(16)
    cw = conv_w_ref[...]
    conv = (cw[0:1, :] * zext_ref[:, CONV_HALO - 2:CONV_HALO - 2 + seq_len, :]
            + cw[1:2, :] * zext_ref[:, CONV_HALO - 1:CONV_HALO - 1 + seq_len, :]
            + cw[2:3, :] * zext_ref[:, CONV_HALO:, :])
    y_conv = _dot(xb, w_in_ref[:, CUT_B:CUT_C]) * conv.reshape(TM, CONV_WIDTH)
    ybuf_ref[:, 0:CONV_WIDTH] = y_conv.astype(_bf16)
    gather.upto(32)

    v = _dot(xb, w_in_ref[:, CUT_V:CUT_P])
    vn = _layer_norm(v, sg_ref[...], sb_ref[...])
    vnb = vn.astype(_bf16)
    gather.upto(48)
    r_i = lax.broadcasted_iota(jnp.int32, (CHUNK, CHUNK), 0)
    c_i = lax.broadcasted_iota(jnp.int32, (CHUNK, CHUNK), 1)
    shift = chunk_len.bit_length() - 1
    mask = (c_i <= r_i) & ((r_i >> shift) == (c_i >> shift))
    u = _dot(xb, w_in_ref[:, CUT_U:CUT_V])
    gather.upto(64)
    n_chunks = TM // CHUNK
    for h in range(N_CHUNK_HEADS):
        m_h = jnp.where(mask, ws_ref[h], 0.0).astype(_bf16)
        b_h = bias_ref[:, h:h + 1]
        cols = slice(h * HEAD_DIM, (h + 1) * HEAD_DIM)
        vn_h = jnp.concatenate([vnb[c * CHUNK:(c + 1) * CHUNK, cols] for c in range(n_chunks)], axis=1)
        mixed = _dot(m_h, vn_h)
        for c in range(n_chunks):
            rows = slice(c * CHUNK, (c + 1) * CHUNK)
            ybuf_ref[rows, CONV_WIDTH + h * HEAD_DIM:CONV_WIDTH + (h + 1) * HEAD_DIM] = (
                u[rows, cols] * (mixed[:, c * HEAD_DIM:(c + 1) * HEAD_DIM] + b_h)).astype(_bf16)
        gather.upto(64 + 8 * (h + 1))

    p = _dot(xb, w_in_ref[:, CUT_P:IN_WIDTH])
    pext_ref[:, POOL_HALO:, :] = p.reshape(n_seq, seq_len, POOL_WIDTH)
    gather.upto(128)
    posf = (pos + 1).astype(_f32)
    for g, w in enumerate(POOL_WINDOWS):
        cols = slice(g * POOL_GROUP_DIM, (g + 1) * POOL_GROUP_DIM)
        s = pext_ref[:, POOL_HALO:, cols]
        for j in range(1, w):
            s = s + pext_ref[:, POOL_HALO - j:POOL_HALO - j + seq_len, cols]
        cnt = jnp.minimum(float(w), posf)
        diff = s.reshape(TM, POOL_GROUP_DIM) / cnt - p[:, cols]
        y_g = _dot(diff.astype(_bf16), pw_ref[g].astype(_bf16)) * ps_ref[:, cols]
        ybuf_ref[:, CONV_WIDTH + CHUNK_WIDTH + g * POOL_GROUP_DIM:
                 CONV_WIDTH + CHUNK_WIDTH + (g + 1) * POOL_GROUP_DIM] = y_g.astype(_bf16)
        gather.upto(128 + 16 * (g + 1))
    assert 128 + 16 * len(POOL_WINDOWS) == BRANCH_GATHER_ROWS
    return z, p, vn


def _route(x1, wr_ref, br_ref):
    x_hi = x1.astype(_bf16)
    x_lo = (x1 - x_hi.astype(_f32)).astype(_bf16)
    hi_parts = _dot(x_hi, wr_ref[...])
    logits = hi_parts[:, :LANES] + hi_parts[:, LANES:] + _dot(x_lo, wr_ref[:, :LANES]) + br_ref[...]
    lane = lax.broadcasted_iota(jnp.int32, logits.shape, 1)
    neg = -jnp.inf
    big = jnp.int32(1 << 20)

    gmask = lane < N_EXPERT_GROUPS
    gl = jnp.where(gmask, logits, neg)
    g_max = jnp.max(gl, axis=1, keepdims=True)
    g_idx = jnp.min(jnp.where(gmask & (gl == g_max), lane, big), axis=1, keepdims=True)
    g_val = 1.0 / jnp.sum(jnp.exp(gl - g_max), axis=1, keepdims=True)

    e_lane = lane - N_EXPERT_GROUPS
    emask = (e_lane >= 0) & (e_lane < N_EXPERTS) & ((e_lane >> (EXPERTS_PER_GROUP.bit_length() - 1)) == g_idx)
    el = jnp.where(emask, logits, neg)
    e_max = jnp.max(el, axis=1, keepdims=True)
    i1 = jnp.min(jnp.where(emask & (el == e_max), e_lane, big), axis=1, keepdims=True)
    e_sum = jnp.sum(jnp.exp(el - e_max), axis=1, keepdims=True)
    mask2 = emask & (e_lane != i1)
    e_second = jnp.max(jnp.where(mask2, el, neg), axis=1, keepdims=True)
    i2 = jnp.min(jnp.where(mask2 & (el == e_second), e_lane, big), axis=1, keepdims=True)
    v1 = 1.0 / e_sum
    v2 = jnp.exp(e_second - e_max) / e_sum
    tot = v1 + v2
    gate1 = g_val * (v1 / tot)
    gate2 = g_val * (v2 / tot)
    return jnp.where(lane == 0, i1.astype(_f32),
                     jnp.where(lane == 1, i2.astype(_f32),
                               jnp.where(lane == 2, gate1, jnp.where(lane == 3, gate2, 0.0))))


N_MIXER_COMMON_REFS = 29


def _mixer_kernel(fused, *refs):
    if fused:
        slot_ref, x1p_ref, routep_ref, y_hbm, g2_ref, b2_ref = refs[:6]
        refs = refs[6:]
    else:
        xp_ref, xsamp_ref = refs[:2]
        refs = refs[2:]
    (w_in_ref, w_out_ref, conv_w_ref, sg_ref, sb_ref, ws_p_ref, ws_s_ref, bias_p_ref, bias_s_ref,
     pw_ref, ps_ref, g1_ref, b1_ref, wr_ref, br_ref, zst_ref, pst_ref,
     x1_ref, route_ref, ztail_ref, ptail_ref, zs_ref, psamp_ref, vns_ref,
     zext_p, pext_p, zext_s, pext_s, ybuf_ref) = refs[:N_MIXER_COMMON_REFS]
    i = pl.program_id(0)
    last = pl.num_programs(0) - 1

    if fused:
        gbuf, sems = refs[N_MIXER_COMMON_REFS:]
        b = i % 2

        def row_starter(tile, half):
            tok0 = tile * TM

            def start_row(r):
                for k in range(2):
                    pltpu.make_async_copy(y_hbm.at[pl.ds(slot_ref[k * N_TOK + tok0 + r], 1)],
                                          gbuf.at[half, k, pl.ds(r, 1)], sems.at[half, k]).start(priority=1)
            return start_row

        def wait_rows(half):
            for k in range(2):
                pltpu.make_async_copy(y_hbm.at[pl.ds(0, TM)], gbuf.at[half, k], sems.at[half, k]).wait()

        @pl.when(i == 0)
        def _():
            first = row_starter(0, 0)

            def body(r, carry):
                first(r)
                return carry

            lax.fori_loop(0, TM, body, 0, unroll=16)

        wait_rows(b)
        routep = routep_ref[...]
        moe = routep[:, 2:3] * gbuf[b, 0] + routep[:, 3:4] * gbuf[b, 1]
        x = _layer_norm(ALPHA * x1p_ref[...] + moe, g2_ref[...], b2_ref[...])
        next_rows = row_starter(jnp.minimum(i + 1, last), 1 - b)
        new_gather = lambda first_row: _RowGather(next_rows, first_row)
    else:
        x = jnp.where(i < N_PROMPT_TILES, xp_ref[...], xsamp_ref[...])
        new_gather = lambda first_row: _NoGather()

    xb = x.astype(_bf16)
    row = lax.broadcasted_iota(jnp.int32, (TM, 1), 0)

    @pl.when(i < N_PROMPT_TILES)
    def _prompt():
        t0 = (i % TILES_PER_SEQ) * TM

        @pl.when(t0 == 0)
        def _():
            zext_p[:, 0:CONV_HALO, :] = jnp.zeros((1, CONV_HALO, CONV_WIDTH), _f32)
            pext_p[:, 0:POOL_HALO, :] = jnp.zeros((1, POOL_HALO, POOL_WIDTH), _f32)

        z, p, _ = _mixer_body(xb, w_in_ref, conv_w_ref, sg_ref, sb_ref, ws_p_ref, bias_p_ref,
                              pw_ref, ps_ref, zext_p, pext_p, ybuf_ref, TM, t0 + row, CHUNK, new_gather(0))
        z_last = z[TM - CONV_HALO:, :].reshape(1, CONV_HALO, CONV_WIDTH)
        p_last = p[TM - POOL_HALO:, :].reshape(1, POOL_HALO, POOL_WIDTH)
        zext_p[:, 0:CONV_HALO, :] = z_last
        pext_p[:, 0:POOL_HALO, :] = p_last
        ztail_ref[...] = z_last
        ptail_ref[...] = p_last

    @pl.when(i >= N_PROMPT_TILES)
    def _sample():
        zext_s[:, 0:CONV_HALO, :] = zst_ref[...]
        pext_s[:, 0:POOL_HALO, :] = pst_ref[...]
        z, p, vn = _mixer_body(xb, w_in_ref, conv_w_ref, sg_ref, sb_ref, ws_s_ref, bias_s_ref,
                               pw_ref, ps_ref, zext_s, pext_s, ybuf_ref, DEC_SEQ,
                               PAST_LEN + (row & (DEC_SEQ - 1)), DEC_SEQ, new_gather(0))
        zs_ref[...] = z.reshape(SEQS_PER_TILE, DEC_SEQ, CONV_WIDTH)
        psamp_ref[...] = p.reshape(SEQS_PER_TILE, DEC_SEQ, POOL_WIDTH)
        vns_ref[...] = vn.reshape(SEQS_PER_TILE, DEC_SEQ, CHUNK_WIDTH)
        ztail_ref[...] = jnp.zeros(ztail_ref.shape, _f32)
        ptail_ref[...] = jnp.zeros(ptail_ref.shape, _f32)

    tail_gather = new_gather(BRANCH_GATHER_ROWS)
    mix = _dot(ybuf_ref[...], w_out_ref[...])
    tail_gather.upto(BRANCH_GATHER_ROWS + (TM - BRANCH_GATHER_ROWS) // 2)
    x1 = _layer_norm(ALPHA * x + mix, g1_ref[...], b1_ref[...])
    x1_ref[...] = x1
    tail_gather.upto(TM)
    route_ref[...] = _route(x1, wr_ref, br_ref)

    if fused:
        @pl.when(i == last)
        def _():
            wait_rows(1 - b)


def _layer_spec(layer, shape):
    nd = len(shape)
    return pl.BlockSpec((None,) + shape, lambda i, *_: (layer,) + (0,) * nd, pipeline_mode=pl.Buffered(1))


def _mixer_call(layer, x_src, params):
    fused = len(x_src) == 6
    samp = lambda i, *_: (jnp.maximum(i - N_PROMPT_TILES, 0), 0, 0)
    samp_l = lambda i, *_: (layer, jnp.maximum(i - N_PROMPT_TILES, 0), 0, 0)
    tail = lambda i, *_: (i // TILES_PER_SEQ, 0, 0)
    tok = lambda i, *_: (i, 0)
    if fused:
        x_specs = [
            pl.BlockSpec((TM, D_MODEL), tok),
            pl.BlockSpec((TM, LANES), tok),
            pl.BlockSpec(memory_space=pl.ANY),
            _layer_spec(layer - 1, (1, D_MODEL)),
            _layer_spec(layer - 1, (1, D_MODEL)),
        ]
    else:
        x_specs = [
            pl.BlockSpec((TM, D_MODEL), lambda i, *_: (jnp.minimum(i, N_PROMPT_TILES - 1), 0)),
            pl.BlockSpec((TM, D_MODEL), lambda i, *_: (jnp.maximum(i - N_PROMPT_TILES, 0), 0)),
        ]
    in_specs = x_specs + [
        _layer_spec(layer, (D_MODEL, IN_WIDTH)),
        _layer_spec(layer, (D_MODEL, D_MODEL)),
        _layer_spec(layer, (3, CONV_WIDTH)),
        _layer_spec(layer, (1, CHUNK_WIDTH)),
        _layer_spec(layer, (1, CHUNK_WIDTH)),
        _layer_spec(layer, (N_CHUNK_HEADS, CHUNK, CHUNK)),
        _layer_spec(layer, (N_CHUNK_HEADS, CHUNK, CHUNK)),
        _layer_spec(layer, (CHUNK, N_CHUNK_HEADS)),
        _layer_spec(layer, (CHUNK, N_CHUNK_HEADS)),
        _layer_spec(layer, (4, POOL_GROUP_DIM, POOL_GROUP_DIM)),
        _layer_spec(layer, (1, POOL_WIDTH)),
        _layer_spec(layer, (1, D_MODEL)),
        _layer_spec(layer, (1, D_MODEL)),
        _layer_spec(layer, (D_MODEL, 2 * LANES)),
        _layer_spec(layer, (1, LANES)),
        pl.BlockSpec((None, SEQS_PER_TILE, CONV_HALO, CONV_WIDTH), samp_l),
        pl.BlockSpec((None, SEQS_PER_TILE, POOL_HALO, POOL_WIDTH), samp_l),
    ]
    out_specs = [
        pl.BlockSpec((TM, D_MODEL), tok),
        pl.BlockSpec((TM, LANES), tok),
        pl.BlockSpec((1, CONV_HALO, CONV_WIDTH), tail),
        pl.BlockSpec((1, POOL_HALO, POOL_WIDTH), tail),
        pl.BlockSpec((SEQS_PER_TILE, DEC_SEQ, CONV_WIDTH), samp),
        pl.BlockSpec((SEQS_PER_TILE, DEC_SEQ, POOL_WIDTH), samp),
        pl.BlockSpec((SEQS_PER_TILE, DEC_SEQ, CHUNK_WIDTH), samp),
    ]
    out_shape = [
        jax.ShapeDtypeStruct((N_TOK, D_MODEL), _f32),
        jax.ShapeDtypeStruct((N_TOK, LANES), _f32),
        jax.ShapeDtypeStruct((BATCH + 1, CONV_HALO, CONV_WIDTH), _f32),
        jax.ShapeDtypeStruct((BATCH + 1, POOL_HALO, POOL_WIDTH), _f32),
        jax.ShapeDtypeStruct((DEC_BATCH, DEC_SEQ, CONV_WIDTH), _f32),
        jax.ShapeDtypeStruct((DEC_BATCH, DEC_SEQ, POOL_WIDTH), _f32),
        jax.ShapeDtypeStruct((DEC_BATCH, DEC_SEQ, CHUNK_WIDTH), _f32),
    ]
    scratch = [
        pltpu.VMEM((1, CONV_HALO + TM, CONV_WIDTH), _f32),
        pltpu.VMEM((1, POOL_HALO + TM, POOL_WIDTH), _f32),
        pltpu.VMEM((SEQS_PER_TILE, CONV_HALO + DEC_SEQ, CONV_WIDTH), _f32),
        pltpu.VMEM((SEQS_PER_TILE, POOL_HALO + DEC_SEQ, POOL_WIDTH), _f32),
        pltpu.VMEM((TM, D_MODEL), _bf16),
    ]
    assert len(in_specs) - len(x_specs) + len(out_specs) + len(scratch) == N_MIXER_COMMON_REFS
    if fused:
        scratch += [pltpu.VMEM((2, 2, TM, D_MODEL), _f32), pltpu.SemaphoreType.DMA((2, 2))]
    return pl.pallas_call(
        functools.partial(_mixer_kernel, fused),
        grid_spec=pltpu.PrefetchScalarGridSpec(
            num_scalar_prefetch=1 if fused else 0,
            grid=(N_TILES,),
            in_specs=in_specs,
            out_specs=out_specs,
            scratch_shapes=scratch,
        ),
        out_shape=out_shape,
        compiler_params=pltpu.CompilerParams(
            dimension_semantics=("arbitrary",), vmem_limit_bytes=VMEM_LIMIT),
        name="mixer",
    )(*x_src, *params)


def _slots_kernel(ids_ref, slot_ref, tile_expert_ref, n_used_ref, count_ref, offset_ref):
    ids = ids_ref[...]
    lane = lax.broadcasted_iota(jnp.int32, (1, LANES), 1)
    counts = jnp.zeros((1, LANES), _f32)
    offsets = jnp.zeros((1, LANES), _f32)
    r_i = lax.broadcasted_iota(jnp.int32, (LANES, LANES), 0)
    c_i = lax.broadcasted_iota(jnp.int32, (LANES, LANES), 1)
    upper = (r_i < c_i).astype(_bf16)
    rr = lax.broadcasted_iota(jnp.int32, (ASSIGN_ROWS, ASSIGN_ROWS), 0)
    rc = lax.broadcasted_iota(jnp.int32, (ASSIGN_ROWS, ASSIGN_ROWS), 1)
    lower = (rc < rr).astype(_bf16)
    tile_row0 = (lax.broadcasted_iota(jnp.int32, (1, LANES), 1) * TE).astype(_f32)

    slot = jnp.zeros((ASSIGN_ROWS, LANES), _f32)
    tile_expert = jnp.zeros((1, LANES), _f32)
    off = jnp.zeros((1, 1), _f32)
    for e in range(N_EXPERTS):
        m = (ids == e).astype(_f32)
        within = _dot(m.astype(_bf16), upper)
        rowsum = jnp.sum(m, axis=1, keepdims=True)
        rowpre = _dot(lower, jnp.broadcast_to(rowsum, (ASSIGN_ROWS, LANES)).astype(_bf16))
        cnt = jnp.sum(rowsum, axis=0, keepdims=True)
        padded = jnp.floor((cnt + (TE - 1)) * (1.0 / TE)) * TE
        slot = slot + m * (off + rowpre + within)
        in_seg = (tile_row0 >= off) & (tile_row0 < off + padded)
        tile_expert = tile_expert + jnp.where(in_seg, float(e), 0.0)
        counts = counts + jnp.where(lane == e, cnt, 0.0)
        offsets = offsets + jnp.where(lane == e, off, 0.0)
        off = off + padded
    tile_expert = jnp.where(tile_row0 >= off, float(N_EXPERTS - 1), tile_expert)
    slot_ref[...] = slot.astype(jnp.int32)
    tile_expert_ref[...] = tile_expert.astype(jnp.int32)
    n_used_ref[...] = jnp.broadcast_to(off * (1.0 / TE), (1, LANES)).astype(jnp.int32)
    count_ref[...] = counts.astype(jnp.int32)
    offset_ref[...] = offsets.astype(jnp.int32)


def _slots_call(ids):
    return pl.pallas_call(
        _slots_kernel,
        out_shape=[jax.ShapeDtypeStruct((ASSIGN_ROWS, LANES), jnp.int32)]
        + [jax.ShapeDtypeStruct((1, LANES), jnp.int32)] * 4,
        name="slots",
    )(ids)


X_BUFS = 5
Y_BUFS = 3


def _experts_kernel(layer, te_ref, nu_ref, count_ref, offset_ref, slot_ref,
                    x1_hbm, wg_hbm, wu_hbm, wd_hbm, y_in_hbm, y_hbm,
                    tok_of_slot, xbuf, ybuf, wg_st, wu_st, wd_st, wg_b, wu_b, wd_b, sem_x, sem_y, sem_w):
    del y_in_hbm
    n_used = nu_ref[0]
    tile_shift = TE.bit_length() - 1

    for k in range(2):
        def fill(t, carry, k=k):
            tok_of_slot[slot_ref[k * N_TOK + t]] = t
            return carry
        lax.fori_loop(0, N_TOK, fill, 0, unroll=16)
    for e in range(N_EXPERTS):
        first_pad = offset_ref[e] + count_ref[e]
        seg_end = offset_ref[e] + (((count_ref[e] + (TE - 1)) >> tile_shift) << tile_shift)

        def pad(s, carry):
            tok_of_slot[s] = 0
            return carry
        lax.fori_loop(first_pad, seg_end, pad, 0)

    def gather(tile, buf):
        base = tile * TE
        for r in range(TE):
            pltpu.make_async_copy(x1_hbm.at[pl.ds(tok_of_slot[base + r], 1)],
                                  xbuf.at[buf, pl.ds(r, 1)], sem_x.at[buf]).start(priority=r % 2)

    def weight_copies(expert, ws):
        return [pltpu.make_async_copy(src.at[layer, expert], dst.at[ws], sem_w.at[ws])
                for src, dst in ((wg_hbm, wg_st), (wu_hbm, wu_st), (wd_hbm, wd_st))]

    def y_copy(tile, buf):
        return pltpu.make_async_copy(ybuf.at[buf], y_hbm.at[pl.ds(tile * TE, TE)], sem_y.at[buf])

    for c in weight_copies(te_ref[0], 0):
        c.start()
    gather(0, 0)
    for ahead in range(1, X_BUFS - 1):
        @pl.when(n_used > ahead)
        def _(ahead=ahead):
            gather(ahead, ahead)

    def tile_step(j, seg):
        expert = te_ref[j]
        first = (j == 0) | (te_ref[jnp.maximum(j - 1, 0)] != expert)

        @pl.when(j + (X_BUFS - 1) < n_used)
        def _():
            gather(j + (X_BUFS - 1), (j + (X_BUFS - 1)) % X_BUFS)

        @pl.when(first)
        def _():
            ws = seg % 2
            for c in weight_copies(expert, ws):
                c.wait()
            wg_b[...] = wg_st[ws].astype(_bf16)
            wu_b[...] = wu_st[ws].astype(_bf16)
            wd_b[...] = wd_st[ws].astype(_bf16)
            next_first = j + ((count_ref[expert] + (TE - 1)) >> tile_shift)

            @pl.when(next_first < n_used)
            def _():
                for c in weight_copies(te_ref[next_first], 1 - ws):
                    c.start()

        xslot = j % X_BUFS
        pltpu.make_async_copy(x1_hbm.at[pl.ds(0, TE)], xbuf.at[xslot], sem_x.at[xslot]).wait()
        xb = xbuf[xslot].astype(_bf16)
        hg = _dot(xb, wg_b[...])
        hu = _dot(xb, wu_b[...])
        a = hg / (1.0 + jnp.exp(-hg)) * hu
        y = _dot(a.astype(_bf16), wd_b[...])

        yslot = j % Y_BUFS

        @pl.when(j >= Y_BUFS)
        def _():
            y_copy(j - Y_BUFS, yslot).wait()

        ybuf[yslot] = y
        y_copy(j, yslot).start()
        return seg + first.astype(jnp.int32)

    lax.fori_loop(0, n_used, tile_step, jnp.int32(0))

    for back in range(1, Y_BUFS + 1):
        @pl.when(n_used >= back)
        def _(back=back):
            y_copy(n_used - back, (n_used - back) % Y_BUFS).wait()


def _experts_call(layer, tile_expert, n_used, counts, offsets, slot_flat, x1, wg, wu, wd, y_buf):
    any_spec = pl.BlockSpec(memory_space=pl.ANY)
    n_prefetch = 5
    return pl.pallas_call(
        functools.partial(_experts_kernel, layer),
        grid_spec=pltpu.PrefetchScalarGridSpec(
            num_scalar_prefetch=n_prefetch,
            grid=(1,),
            in_specs=[any_spec] * 5,
            out_specs=any_spec,
            scratch_shapes=[
                pltpu.SMEM((N_SLOTS,), jnp.int32),
                pltpu.VMEM((X_BUFS, TE, D_MODEL), _f32),
                pltpu.VMEM((Y_BUFS, TE, D_MODEL), _f32),
                pltpu.VMEM((2, D_MODEL, D_EXPERT), _f32),
                pltpu.VMEM((2, D_MODEL, D_EXPERT), _f32),
                pltpu.VMEM((2, D_EXPERT, D_MODEL), _f32),
                pltpu.VMEM((D_MODEL, D_EXPERT), _bf16),
                pltpu.VMEM((D_MODEL, D_EXPERT), _bf16),
                pltpu.VMEM((D_EXPERT, D_MODEL), _bf16),
                pltpu.SemaphoreType.DMA((X_BUFS,)),
                pltpu.SemaphoreType.DMA((Y_BUFS,)),
                pltpu.SemaphoreType.DMA((2,)),
            ],
        ),
        out_shape=jax.ShapeDtypeStruct((N_SLOTS, D_MODEL), _f32),
        input_output_aliases={n_prefetch + 4: 0},
        compiler_params=pltpu.CompilerParams(
            dimension_semantics=("arbitrary",), vmem_limit_bytes=VMEM_LIMIT),
        name="experts",
    )(tile_expert, n_used, counts, offsets, slot_flat, x1, wg, wu, wd, y_buf)


COMBINE_BUFS = 2


def _combine_kernel(slot_ref, x1_ref, route_ref, g2_ref, b2_ref, y_hbm, out_p_ref, out_s_ref,
                    ybuf, sems):
    i = pl.program_id(0)
    b = i % COMBINE_BUFS

    def gather(tile, slot):
        tok0 = tile * TM

        def start(r, carry):
            for k in range(2):
                pltpu.make_async_copy(y_hbm.at[pl.ds(slot_ref[k * N_TOK + tok0 + r], 1)],
                                      ybuf.at[slot, k, pl.ds(r, 1)], sems.at[slot, k]).start(priority=k)
            return carry

        lax.fori_loop(0, TM, start, 0, unroll=16)

    @pl.when(i == 0)
    def _():
        for ahead in range(COMBINE_BUFS - 1):
            gather(ahead, ahead)

    @pl.when(i + (COMBINE_BUFS - 1) < pl.num_programs(0))
    def _():
        gather(i + (COMBINE_BUFS - 1), (i + (COMBINE_BUFS - 1)) % COMBINE_BUFS)

    for k in range(2):
        pltpu.make_async_copy(y_hbm.at[pl.ds(0, TM)], ybuf.at[b, k], sems.at[b, k]).wait()

    route = route_ref[...]
    moe = route[:, 2:3] * ybuf[b, 0] + route[:, 3:4] * ybuf[b, 1]
    out = _layer_norm(ALPHA * x1_ref[...] + moe, g2_ref[...], b2_ref[...])

    @pl.when(i < N_PROMPT_TILES)
    def _():
        out_p_ref[...] = out

    @pl.when(i >= N_PROMPT_TILES)
    def _():
        out_s_ref[...] = out


def _combine_call(slot_flat, x1, route, g2, b2, y):
    tok = lambda i, s: (i, 0)
    const = lambda i, s: (0, 0)
    return pl.pallas_call(
        _combine_kernel,
        grid_spec=pltpu.PrefetchScalarGridSpec(
            num_scalar_prefetch=1,
            grid=(N_TILES,),
            in_specs=[
                pl.BlockSpec((TM, D_MODEL), tok),
                pl.BlockSpec((TM, LANES), tok),
                pl.BlockSpec((1, D_MODEL), const),
                pl.BlockSpec((1, D_MODEL), const),
                pl.BlockSpec(memory_space=pl.ANY),
            ],
            out_specs=[
                pl.BlockSpec((TM, D_MODEL), lambda i, s: (jnp.minimum(i, N_PROMPT_TILES - 1), 0)),
                pl.BlockSpec((TM, D_MODEL), lambda i, s: (jnp.maximum(i - N_PROMPT_TILES, 0), 0)),
            ],
            scratch_shapes=[
                pltpu.VMEM((COMBINE_BUFS, 2, TM, D_MODEL), _f32),
                pltpu.SemaphoreType.DMA((COMBINE_BUFS, 2)),
            ],
        ),
        out_shape=[jax.ShapeDtypeStruct((N_PROMPT, D_MODEL), _f32),
                   jax.ShapeDtypeStruct((N_SAMPLE, D_MODEL), _f32)],
        compiler_params=pltpu.CompilerParams(
            dimension_semantics=("arbitrary",), vmem_limit_bytes=VMEM_LIMIT),
        name="combine",
    )(slot_flat, x1, route, g2, b2, y)


def kernel(x_prompt, x_sample, state_conv, state_pool, w_in, conv_w, sgu_ln_g, sgu_ln_b, sgu_w, sgu_b, pool_w, pool_scale, w_out, ln1_g, ln1_b, router_group_w, router_group_b, router_expert_w, router_expert_b, expert_w_gate, expert_w_up, expert_w_down, ln2_g, ln2_b):
    y_buf = jnp.zeros((N_SLOTS, D_MODEL), _f32)
    reps = CHUNK // DEC_SEQ
    row = lambda a: a.reshape(DEPTH, 1, -1)

    w_r = jnp.concatenate([router_group_w, router_expert_w], axis=2)
    w_r = jnp.pad(w_r, ((0, 0), (0, 0), (0, LANES - w_r.shape[2])))
    wr_hi = w_r.astype(_bf16)
    wr = jnp.concatenate([wr_hi, (w_r - wr_hi.astype(_f32)).astype(_bf16)], axis=2)
    b_r = jnp.concatenate([router_group_b, router_expert_b], axis=1)
    b_r = row(jnp.pad(b_r, ((0, 0), (0, LANES - b_r.shape[1]))))
    ws_s = jnp.tile(sgu_w[:, :, :DEC_SEQ, :DEC_SEQ], (1, 1, reps, reps))
    bias_p = jnp.swapaxes(sgu_b, 1, 2)
    bias_s = jnp.tile(bias_p[:, :DEC_SEQ], (1, reps, 1))
    zst = jnp.pad(state_conv, ((0, 0), (0, 0), (CONV_HALO - 2, 0), (0, 0)))
    pst = jnp.pad(state_pool, ((0, 0), (0, 0), (POOL_HALO - POOL_BUF, 0), (0, 0)))
    mixer_params = (w_in.astype(_bf16), w_out.astype(_bf16), conv_w, row(sgu_ln_g), row(sgu_ln_b), sgu_w,
                    ws_s, bias_p, bias_s, pool_w, row(pool_scale), row(ln1_g), row(ln1_b), wr, b_r, zst, pst)
    g2, b2 = row(ln2_g), row(ln2_b)

    ztails, ptails, zss, psamps, v_s = [], [], [], [], []
    x_src = (x_prompt.reshape(N_PROMPT, D_MODEL), x_sample.reshape(N_SAMPLE, D_MODEL))
    for l in range(DEPTH):
        x1, route, ztail, ptail, zs, psamp, vns = _mixer_call(l, x_src, mixer_params)
        ids = route[:, 0:2].astype(jnp.int32).T.reshape(ASSIGN_ROWS, LANES)
        slot, tile_expert, n_used, counts, offsets = _slots_call(ids)
        slot_flat = slot.reshape(N_ASSIGN)
        y = _experts_call(l, tile_expert.reshape(LANES), n_used.reshape(LANES)[:1], counts.reshape(LANES),
                          offsets.reshape(LANES), slot_flat, x1, expert_w_gate, expert_w_up, expert_w_down,
                          y_buf)
        x_src = (slot_flat, x1, route, y, g2, b2)
        y_buf = y
        ztails.append(ztail)
        ptails.append(ptail)
        zss.append(zs)
        psamps.append(psamp)
        v_s.append(vns)

    x_p, x_s = _combine_call(slot_flat, x1, route, g2[DEPTH - 1], b2[DEPTH - 1], y)
    y_prompt = x_p.reshape(BATCH, SEQ, D_MODEL)
    y_sample = x_s.reshape(DEC_BATCH, DEC_SEQ, D_MODEL)
    new_conv_prompt = jnp.stack(ztails)[:, :BATCH, CONV_HALO - 2:]
    new_pool_prompt = jnp.stack(ptails)[:, :BATCH, POOL_HALO - POOL_BUF:]
    new_conv_sample = jnp.stack(zss)[:, :, DEC_SEQ - 2:]
    new_pool_sample = jnp.concatenate([state_pool[:, :, DEC_SEQ:], jnp.stack(psamps)], axis=2)
    return (y_prompt, y_sample, new_conv_prompt, new_pool_prompt, new_conv_sample, new_pool_sample,
            jnp.stack(v_s))
```

```python
import functools

import jax
import jax.numpy as jnp
from jax import lax
from jax.experimental import pallas as pl
from jax.experimental.pallas import tpu as pltpu

D_MODEL = 2048
BATCH = 4
SEQ = 2048
DEPTH = 4
DEC_BATCH = 128
DEC_SEQ = 8
PAST_LEN = 16384
HEAD_DIM = 128
CONV_WIDTH = 768
POOL_WIDTH = 512
CHUNK_WIDTH = 768
CHUNK = 128
N_CHUNK_HEADS = 6
POOL_WINDOWS = (2, 4, 8, 16)
POOL_GROUP_DIM = 128
POOL_BUF = 15
IN_WIDTH = 4352
N_EXPERT_GROUPS = 4
EXPERTS_PER_GROUP = 4
N_EXPERTS = 16
D_EXPERT = 512
ALPHA = (2 * DEPTH) ** 0.25
LN_EPS = 1e-5

N_PROMPT = BATCH * SEQ
N_SAMPLE = DEC_BATCH * DEC_SEQ
N_TOK = N_PROMPT + N_SAMPLE
N_ASSIGN = 2 * N_TOK

LANES = 128
SUBLANES = 8
CONV_HALO = 8
POOL_HALO = 16

TM = 256
TILES_PER_SEQ = SEQ // TM
N_PROMPT_TILES = N_PROMPT // TM
N_SAMPLE_TILES = N_SAMPLE // TM
N_TILES = N_PROMPT_TILES + N_SAMPLE_TILES
SEQS_PER_TILE = TM // DEC_SEQ

TE = 256
N_SLOTS = N_ASSIGN + N_EXPERTS * TE
N_ETILES = N_SLOTS // TE
ASSIGN_ROWS = N_ASSIGN // LANES

CUT_B, CUT_C, CUT_H, CUT_U, CUT_V, CUT_P = 0, 768, 1536, 2304, 3072, 3840

VMEM_LIMIT = 62 * 1024 * 1024

_f32 = jnp.float32
_bf16 = jnp.bfloat16


def _dot(a, b):
    return jnp.dot(a, b, preferred_element_type=_f32)


def _layer_norm(r, g, b):
    mu = jnp.mean(r, axis=-1, keepdims=True)
    c = r - mu
    var = jnp.mean(c * c, axis=-1, keepdims=True)
    return c * lax.rsqrt(var + LN_EPS) * g + b


class _RowGather:
    def __init__(self, start_row, first_row):
        self._start_row = start_row
        self._done = first_row

    def upto(self, row_end):
        for r in range(self._done, row_end):
            self._start_row(r)
        self._done = max(self._done, row_end)


class _NoGather:
    def upto(self, row_end):
        del row_end


BRANCH_GATHER_ROWS = 192


def _mixer_body(xb, w_in_ref, conv_w_ref, sg_ref, sb_ref, ws_ref, bias_ref, pw_ref, ps_ref,
                zext_ref, pext_ref, ybuf_ref, seq_len, pos, chunk_len, gather):
    n_seq = TM // seq_len

    z = _dot(xb, w_in_ref[:, CUT_C:CUT_H]) * _dot(xb, w_in_ref[:, CUT_H:CUT_U])
    zext_ref[:, CONV_HALO:, :] = z.reshape(n_seq, seq_len, CONV_WIDTH)
    gather.upto(16)
    cw = conv_w_ref[...]
    conv = (cw[0:1, :] * zext_ref[:, CONV_HALO - 2:CONV_HALO - 2 + seq_len, :]
            + cw[1:2, :] * zext_ref[:, CONV_HALO - 1:CONV_HALO - 1 + seq_len, :]
            + cw[2:3, :] * zext_ref[:, CONV_HALO:, :])
    y_conv = _dot(xb, w_in_ref[:, CUT_B:CUT_C]) * conv.reshape(TM, CONV_WIDTH)
    ybuf_ref[:, 0:CONV_WIDTH] = y_conv.astype(_bf16)
    gather.upto(32)

    v = _dot(xb, w_in_ref[:, CUT_V:CUT_P])
    vn = _layer_norm(v, sg_ref[...], sb_ref[...])
    vnb = vn.astype(_bf16)
    gather.upto(48)
    r_i = lax.broadcasted_iota(jnp.int32, (CHUNK, CHUNK), 0)
    c_i = lax.broadcasted_iota(jnp.int32, (CHUNK, CHUNK), 1)
    shift = chunk_len.bit_length() - 1
    mask = (c_i <= r_i) & ((r_i >> shift) == (c_i >> shift))
    u = _dot(xb, w_in_ref[:, CUT_U:CUT_V])
    gather.upto(64)
    n_chunks = TM // CHUNK
    for h in range(N_CHUNK_HEADS):
        m_h = jnp.where(mask, ws_ref[h], 0.0).astype(_bf16)
        b_h = bias_ref[:, h:h + 1]
        cols = slice(h * HEAD_DIM, (h + 1) * HEAD_DIM)
        vn_h = jnp.concatenate([vnb[c * CHUNK:(c + 1) * CHUNK, cols] for c in range(n_chunks)], axis=1)
        mixed = _dot(m_h, vn_h)
        for c in range(n_chunks):
            rows = slice(c * CHUNK, (c + 1) * CHUNK)
            ybuf_ref[rows, CONV_WIDTH + h * HEAD_DIM:CONV_WIDTH + (h + 1) * HEAD_DIM] = (
                u[rows, cols] * (mixed[:, c * HEAD_DIM:(c + 1) * HEAD_DIM] + b_h)).astype(_bf16)
        gather.upto(64 + 8 * (h + 1))

    p = _dot(xb, w_in_ref[:, CUT_P:IN_WIDTH])
    pext_ref[:, POOL_HALO:, :] = p.reshape(n_seq, seq_len, POOL_WIDTH)
    gather.upto(128)
    posf = (pos + 1).astype(_f32)
    for g, w in enumerate(POOL_WINDOWS):
        cols = slice(g * POOL_GROUP_DIM, (g + 1) * POOL_GROUP_DIM)
        s = pext_ref[:, POOL_HALO:, cols]
        for j in range(1, w):
            s = s + pext_ref[:, POOL_HALO - j:POOL_HALO - j + seq_len, cols]
        cnt = jnp.minimum(float(w), posf)
        diff = s.reshape(TM, POOL_GROUP_DIM) / cnt - p[:, cols]
        y_g = _dot(diff.astype(_bf16), pw_ref[g].astype(_bf16)) * ps_ref[:, cols]
        ybuf_ref[:, CONV_WIDTH + CHUNK_WIDTH + g * POOL_GROUP_DIM:
                 CONV_WIDTH + CHUNK_WIDTH + (g + 1) * POOL_GROUP_DIM] = y_g.astype(_bf16)
        gather.upto(128 + 16 * (g + 1))
    assert 128 + 16 * len(POOL_WINDOWS) == BRANCH_GATHER_ROWS
    return z, p, vn


def _route(x1, wr_ref, br_ref):
    x_hi = x1.astype(_bf16)
    x_lo = (x1 - x_hi.astype(_f32)).astype(_bf16)
    hi_parts = _dot(x_hi, wr_ref[...])
    logits = hi_parts[:, :LANES] + hi_parts[:, LANES:] + _dot(x_lo, wr_ref[:, :LANES]) + br_ref[...]
    lane = lax.broadcasted_iota(jnp.int32, logits.shape, 1)
    neg = -jnp.inf
    big = jnp.int32(1 << 20)

    gmask = lane < N_EXPERT_GROUPS
    gl = jnp.where(gmask, logits, neg)
    g_max = jnp.max(gl, axis=1, keepdims=True)
    g_idx = jnp.min(jnp.where(gmask & (gl == g_max), lane, big), axis=1, keepdims=True)
    g_val = 1.0 / jnp.sum(jnp.exp(gl - g_max), axis=1, keepdims=True)

    e_lane = lane - N_EXPERT_GROUPS
    emask = (e_lane >= 0) & (e_lane < N_EXPERTS) & ((e_lane >> (EXPERTS_PER_GROUP.bit_length() - 1)) == g_idx)
    el = jnp.where(emask, logits, neg)
    e_max = jnp.max(el, axis=1, keepdims=True)
    i1 = jnp.min(jnp.where(emask & (el == e_max), e_lane, big), axis=1, keepdims=True)
    e_sum = jnp.sum(jnp.exp(el - e_max), axis=1, keepdims=True)
    mask2 = emask & (e_lane != i1)
    e_second = jnp.max(jnp.where(mask2, el, neg), axis=1, keepdims=True)
    i2 = jnp.min(jnp.where(mask2 & (el == e_second), e_lane, big), axis=1, keepdims=True)
    v1 = 1.0 / e_sum
    v2 = jnp.exp(e_second - e_max) / e_sum
    tot = v1 + v2
    gate1 = g_val * (v1 / tot)
    gate2 = g_val * (v2 / tot)
    return jnp.where(lane == 0, i1.astype(_f32),
                     jnp.where(lane == 1, i2.astype(_f32),
                               jnp.where(lane == 2, gate1, jnp.where(lane == 3, gate2, 0.0))))


N_MIXER_COMMON_REFS = 29


def _mixer_kernel(fused, *refs):
    if fused:
        slot_ref, x1p_ref, routep_ref, y_hbm, g2_ref, b2_ref = refs[:6]
        refs = refs[6:]
    else:
        xp_ref, xsamp_ref = refs[:2]
        refs = refs[2:]
    (w_in_ref, w_out_ref, conv_w_ref, sg_ref, sb_ref, ws_p_ref, ws_s_ref, bias_p_ref, bias_s_ref,
     pw_ref, ps_ref, g1_ref, b1_ref, wr_ref, br_ref, zst_ref, pst_ref,
     x1_ref, route_ref, ztail_ref, ptail_ref, zs_ref, psamp_ref, vns_ref,
     zext_p, pext_p, zext_s, pext_s, ybuf_ref) = refs[:N_MIXER_COMMON_REFS]
    i = pl.program_id(0)
    last = pl.num_programs(0) - 1

    if fused:
        gbuf, sems = refs[N_MIXER_COMMON_REFS:]
        b = i % 2

        def row_starter(tile, half):
            tok0 = tile * TM

            def start_row(r):
                for k in range(2):
                    pltpu.make_async_copy(y_hbm.at[pl.ds(slot_ref[k * N_TOK + tok0 + r], 1)],
                                          gbuf.at[half, k, pl.ds(r, 1)], sems.at[half, k]).start(priority=k)
            return start_row

        def wait_rows(half):
            for k in range(2):
                pltpu.make_async_copy(y_hbm.at[pl.ds(0, TM)], gbuf.at[half, k], sems.at[half, k]).wait()

        @pl.when(i == 0)
        def _():
            first = row_starter(0, 0)

            def body(r, carry):
                first(r)
                return carry

            lax.fori_loop(0, TM, body, 0, unroll=16)

        wait_rows(b)
        routep = routep_ref[...]
        moe = routep[:, 2:3] * gbuf[b, 0] + routep[:, 3:4] * gbuf[b, 1]
        x = _layer_norm(ALPHA * x1p_ref[...] + moe, g2_ref[...], b2_ref[...])
        next_rows = row_starter(jnp.minimum(i + 1, last), 1 - b)
        new_gather = lambda first_row: _RowGather(next_rows, first_row)
    else:
        x = jnp.where(i < N_PROMPT_TILES, xp_ref[...], xsamp_ref[...])
        new_gather = lambda first_row: _NoGather()

    xb = x.astype(_bf16)
    row = lax.broadcasted_iota(jnp.int32, (TM, 1), 0)

    @pl.when(i < N_PROMPT_TILES)
    def _prompt():
        t0 = (i % TILES_PER_SEQ) * TM

        @pl.when(t0 == 0)
        def _():
            zext_p[:, 0:CONV_HALO, :] = jnp.zeros((1, CONV_HALO, CONV_WIDTH), _f32)
            pext_p[:, 0:POOL_HALO, :] = jnp.zeros((1, POOL_HALO, POOL_WIDTH), _f32)

        z, p, _ = _mixer_body(xb, w_in_ref, conv_w_ref, sg_ref, sb_ref, ws_p_ref, bias_p_ref,
                              pw_ref, ps_ref, zext_p, pext_p, ybuf_ref, TM, t0 + row, CHUNK, new_gather(0))
        z_last = z[TM - CONV_HALO:, :].reshape(1, CONV_HALO, CONV_WIDTH)
        p_last = p[TM - POOL_HALO:, :].reshape(1, POOL_HALO, POOL_WIDTH)
        zext_p[:, 0:CONV_HALO, :] = z_last
        pext_p[:, 0:POOL_HALO, :] = p_last
        ztail_ref[...] = z_last
        ptail_ref[...] = p_last

    @pl.when(i >= N_PROMPT_TILES)
    def _sample():
        zext_s[:, 0:CONV_HALO, :] = zst_ref[...]
        pext_s[:, 0:POOL_HALO, :] = pst_ref[...]
        z, p, vn = _mixer_body(xb, w_in_ref, conv_w_ref, sg_ref, sb_ref, ws_s_ref, bias_s_ref,
                               pw_ref, ps_ref, zext_s, pext_s, ybuf_ref, DEC_SEQ,
                               PAST_LEN + (row & (DEC_SEQ - 1)), DEC_SEQ, new_gather(0))
        zs_ref[...] = z.reshape(SEQS_PER_TILE, DEC_SEQ, CONV_WIDTH)
        psamp_ref[...] = p.reshape(SEQS_PER_TILE, DEC_SEQ, POOL_WIDTH)
        vns_ref[...] = vn.reshape(SEQS_PER_TILE, DEC_SEQ, CHUNK_WIDTH)
        ztail_ref[...] = jnp.zeros(ztail_ref.shape, _f32)
        ptail_ref[...] = jnp.zeros(ptail_ref.shape, _f32)

    tail_gather = new_gather(BRANCH_GATHER_ROWS)
    mix = _dot(ybuf_ref[...], w_out_ref[...])
    tail_gather.upto(BRANCH_GATHER_ROWS + (TM - BRANCH_GATHER_ROWS) // 2)
    x1 = _layer_norm(ALPHA * x + mix, g1_ref[...], b1_ref[...])
    x1_ref[...] = x1
    tail_gather.upto(TM)
    route_ref[...] = _route(x1, wr_ref, br_ref)

    if fused:
        @pl.when(i == last)
        def _():
            wait_rows(1 - b)


def _layer_spec(layer, shape):
    nd = len(shape)
    return pl.BlockSpec((None,) + shape, lambda i, *_: (layer,) + (0,) * nd, pipeline_mode=pl.Buffered(1))


def _mixer_call(layer, x_src, params):
    fused = len(x_src) == 6
    samp = lambda i, *_: (jnp.maximum(i - N_PROMPT_TILES, 0), 0, 0)
    samp_l = lambda i, *_: (layer, jnp.maximum(i - N_PROMPT_TILES, 0), 0, 0)
    tail = lambda i, *_: (i // TILES_PER_SEQ, 0, 0)
    tok = lambda i, *_: (i, 0)
    if fused:
        x_specs = [
            pl.BlockSpec((TM, D_MODEL), tok),
            pl.BlockSpec((TM, LANES), tok),
            pl.BlockSpec(memory_space=pl.ANY),
            _layer_spec(layer - 1, (1, D_MODEL)),
            _layer_spec(layer - 1, (1, D_MODEL)),
        ]
    else:
        x_specs = [
            pl.BlockSpec((TM, D_MODEL), lambda i, *_: (jnp.minimum(i, N_PROMPT_TILES - 1), 0)),
            pl.BlockSpec((TM, D_MODEL), lambda i, *_: (jnp.maximum(i - N_PROMPT_TILES, 0), 0)),
        ]
    in_specs = x_specs + [
        _layer_spec(layer, (D_MODEL, IN_WIDTH)),
        _layer_spec(layer, (D_MODEL, D_MODEL)),
        _layer_spec(layer, (3, CONV_WIDTH)),
        _layer_spec(layer, (1, CHUNK_WIDTH)),
        _layer_spec(layer, (1, CHUNK_WIDTH)),
        _layer_spec(layer, (N_CHUNK_HEADS, CHUNK, CHUNK)),
        _layer_spec(layer, (N_CHUNK_HEADS, CHUNK, CHUNK)),
        _layer_spec(layer, (CHUNK, N_CHUNK_HEADS)),
        _layer_spec(layer, (CHUNK, N_CHUNK_HEADS)),
        _layer_spec(layer, (4, POOL_GROUP_DIM, POOL_GROUP_DIM)),
        _layer_spec(layer, (1, POOL_WIDTH)),
        _layer_spec(layer, (1, D_MODEL)),
        _layer_spec(layer, (1, D_MODEL)),
        _layer_spec(layer, (D_MODEL, 2 * LANES)),
        _layer_spec(layer, (1, LANES)),
        pl.BlockSpec((None, SEQS_PER_TILE, CONV_HALO, CONV_WIDTH), samp_l),
        pl.BlockSpec((None, SEQS_PER_TILE, POOL_HALO, POOL_WIDTH), samp_l),
    ]
    out_specs = [
        pl.BlockSpec((TM, D_MODEL), tok),
        pl.BlockSpec((TM, LANES), tok),
        pl.BlockSpec((1, CONV_HALO, CONV_WIDTH), tail),
        pl.BlockSpec((1, POOL_HALO, POOL_WIDTH), tail),
        pl.BlockSpec((SEQS_PER_TILE, DEC_SEQ, CONV_WIDTH), samp),
        pl.BlockSpec((SEQS_PER_TILE, DEC_SEQ, POOL_WIDTH), samp),
        pl.BlockSpec((SEQS_PER_TILE, DEC_SEQ, CHUNK_WIDTH), samp),
    ]
    out_shape = [
        jax.ShapeDtypeStruct((N_TOK, D_MODEL), _f32),
        jax.ShapeDtypeStruct((N_TOK, LANES), _f32),
        jax.ShapeDtypeStruct((BATCH + 1, CONV_HALO, CONV_WIDTH), _f32),
        jax.ShapeDtypeStruct((BATCH + 1, POOL_HALO, POOL_WIDTH), _f32),
        jax.ShapeDtypeStruct((DEC_BATCH, DEC_SEQ, CONV_WIDTH), _f32),
        jax.ShapeDtypeStruct((DEC_BATCH, DEC_SEQ, POOL_WIDTH), _f32),
        jax.ShapeDtypeStruct((DEC_BATCH, DEC_SEQ, CHUNK_WIDTH), _f32),
    ]
    scratch = [
        pltpu.VMEM((1, CONV_HALO + TM, CONV_WIDTH), _f32),
        pltpu.VMEM((1, POOL_HALO + TM, POOL_WIDTH), _f32),
        pltpu.VMEM((SEQS_PER_TILE, CONV_HALO + DEC_SEQ, CONV_WIDTH), _f32),
        pltpu.VMEM((SEQS_PER_TILE, POOL_HALO + DEC_SEQ, POOL_WIDTH), _f32),
        pltpu.VMEM((TM, D_MODEL), _bf16),
    ]
    assert len(in_specs) - len(x_specs) + len(out_specs) + len(scratch) == N_MIXER_COMMON_REFS
    if fused:
        scratch += [pltpu.VMEM((2, 2, TM, D_MODEL), _f32), pltpu.SemaphoreType.DMA((2, 2))]
    return pl.pallas_call(
        functools.partial(_mixer_kernel, fused),
        grid_spec=pltpu.PrefetchScalarGridSpec(
            num_scalar_prefetch=1 if fused else 0,
            grid=(N_TILES,),
            in_specs=in_specs,
            out_specs=out_specs,
            scratch_shapes=scratch,
        ),
        out_shape=out_shape,
        compiler_params=pltpu.CompilerParams(
            dimension_semantics=("arbitrary",), vmem_limit_bytes=VMEM_LIMIT),
        name="mixer",
    )(*x_src, *params)


def _slots_kernel(ids_ref, slot_ref, tile_expert_ref, n_used_ref, count_ref, offset_ref):
    ids = ids_ref[...]
    lane = lax.broadcasted_iota(jnp.int32, (1, LANES), 1)
    counts = jnp.zeros((1, LANES), _f32)
    offsets = jnp.zeros((1, LANES), _f32)
    r_i = lax.broadcasted_iota(jnp.int32, (LANES, LANES), 0)
    c_i = lax.broadcasted_iota(jnp.int32, (LANES, LANES), 1)
    upper = (r_i < c_i).astype(_bf16)
    rr = lax.broadcasted_iota(jnp.int32, (ASSIGN_ROWS, ASSIGN_ROWS), 0)
    rc = lax.broadcasted_iota(jnp.int32, (ASSIGN_ROWS, ASSIGN_ROWS), 1)
    lower = (rc < rr).astype(_bf16)
    tile_row0 = (lax.broadcasted_iota(jnp.int32, (1, LANES), 1) * TE).astype(_f32)

    slot = jnp.zeros((ASSIGN_ROWS, LANES), _f32)
    tile_expert = jnp.zeros((1, LANES), _f32)
    off = jnp.zeros((1, 1), _f32)
    for e in range(N_EXPERTS):
        m = (ids == e).astype(_f32)
        within = _dot(m.astype(_bf16), upper)
        rowsum = jnp.sum(m, axis=1, keepdims=True)
        rowpre = _dot(lower, jnp.broadcast_to(rowsum, (ASSIGN_ROWS, LANES)).astype(_bf16))
        cnt = jnp.sum(rowsum, axis=0, keepdims=True)
        padded = jnp.floor((cnt + (TE - 1)) * (1.0 / TE)) * TE
        slot = slot + m * (off + rowpre + within)
        in_seg = (tile_row0 >= off) & (tile_row0 < off + padded)
        tile_expert = tile_expert + jnp.where(in_seg, float(e), 0.0)
        counts = counts + jnp.where(lane == e, cnt, 0.0)
        offsets = offsets + jnp.where(lane == e, off, 0.0)
        off = off + padded
    tile_expert = jnp.where(tile_row0 >= off, float(N_EXPERTS - 1), tile_expert)
    slot_ref[...] = slot.astype(jnp.int32)
    tile_expert_ref[...] = tile_expert.astype(jnp.int32)
    n_used_ref[...] = jnp.broadcast_to(off * (1.0 / TE), (1, LANES)).astype(jnp.int32)
    count_ref[...] = counts.astype(jnp.int32)
    offset_ref[...] = offsets.astype(jnp.int32)


def _slots_call(ids):
    return pl.pallas_call(
        _slots_kernel,
        out_shape=[jax.ShapeDtypeStruct((ASSIGN_ROWS, LANES), jnp.int32)]
        + [jax.ShapeDtypeStruct((1, LANES), jnp.int32)] * 4,
        name="slots",
    )(ids)


X_BUFS = 4
Y_BUFS = 3
W_SLOTS = 3


def _experts_kernel(layer, te_ref, nu_ref, count_ref, offset_ref, slot_ref,
                    x1_hbm, wg_hbm, wu_hbm, wd_hbm, y_in_hbm, y_hbm,
                    tok_of_slot, xbuf, ybuf, wg_st, wu_st, wd_st, wg_b, wu_b, wd_b, sem_x, sem_y, sem_w):
    del y_in_hbm
    n_used = nu_ref[0]
    tile_shift = TE.bit_length() - 1

    def tiles_of(expert):
        return (count_ref[expert] + (TE - 1)) >> tile_shift

    def weight_copies(expert, ws):
        return [pltpu.make_async_copy(src.at[layer, expert], dst.at[ws], sem_w.at[ws])
                for src, dst in ((wg_hbm, wg_st), (wu_hbm, wu_st), (wd_hbm, wd_st))]

    for c in weight_copies(te_ref[0], 0):
        c.start()
    second_first = tiles_of(te_ref[0])

    @pl.when(second_first < n_used)
    def _():
        for c in weight_copies(te_ref[second_first], 1):
            c.start()

    for k in range(2):
        def fill(t, carry, k=k):
            tok_of_slot[slot_ref[k * N_TOK + t]] = t
            return carry
        lax.fori_loop(0, N_TOK, fill, 0, unroll=16)
    for e in range(N_EXPERTS):
        first_pad = offset_ref[e] + count_ref[e]
        seg_end = offset_ref[e] + (((count_ref[e] + (TE - 1)) >> tile_shift) << tile_shift)

        def pad(s, carry):
            tok_of_slot[s] = 0
            return carry
        lax.fori_loop(first_pad, seg_end, pad, 0)

    def gather(tile, buf):
        base = tile * TE
        for r in range(TE):
            pltpu.make_async_copy(x1_hbm.at[pl.ds(tok_of_slot[base + r], 1)],
                                  xbuf.at[buf, pl.ds(r, 1)], sem_x.at[buf]).start(priority=r % 2)

    def y_copy(tile, buf):
        return pltpu.make_async_copy(ybuf.at[buf], y_hbm.at[pl.ds(tile * TE, TE)], sem_y.at[buf])

    gather(0, 0)
    for ahead in range(1, X_BUFS - 1):
        @pl.when(n_used > ahead)
        def _(ahead=ahead):
            gather(ahead, ahead)

    def tile_step(j, seg):
        expert = te_ref[j]
        first = (j == 0) | (te_ref[jnp.maximum(j - 1, 0)] != expert)

        @pl.when(j + (X_BUFS - 1) < n_used)
        def _():
            gather(j + (X_BUFS - 1), (j + (X_BUFS - 1)) % X_BUFS)

        @pl.when(first)
        def _():
            ws = seg % W_SLOTS
            for c in weight_copies(expert, ws):
                c.wait()
            wg_b[...] = wg_st[ws].astype(_bf16)
            wu_b[...] = wu_st[ws].astype(_bf16)
            wd_b[...] = wd_st[ws].astype(_bf16)
            next_first = j + tiles_of(expert)

            @pl.when(next_first < n_used)
            def _():
                after_next = next_first + tiles_of(te_ref[next_first])

                @pl.when(after_next < n_used)
                def _():
                    for c in weight_copies(te_ref[after_next], (seg + 2) % W_SLOTS):
                        c.start()

        xslot = j % X_BUFS
        pltpu.make_async_copy(x1_hbm.at[pl.ds(0, TE)], xbuf.at[xslot], sem_x.at[xslot]).wait()
        xb = xbuf[xslot].astype(_bf16)
        hg = _dot(xb, wg_b[...])
        hu = _dot(xb, wu_b[...])
        a = hg / (1.0 + jnp.exp(-hg)) * hu
        y = _dot(a.astype(_bf16), wd_b[...])

        yslot = j % Y_BUFS

        @pl.when(j >= Y_BUFS)
        def _():
            y_copy(j - Y_BUFS, yslot).wait()

        ybuf[yslot] = y
        y_copy(j, yslot).start()
        return seg + first.astype(jnp.int32)

    lax.fori_loop(0, n_used, tile_step, jnp.int32(0))

    for back in range(1, Y_BUFS + 1):
        @pl.when(n_used >= back)
        def _(back=back):
            y_copy(n_used - back, (n_used - back) % Y_BUFS).wait()


def _experts_call(layer, tile_expert, n_used, counts, offsets, slot_flat, x1, wg, wu, wd, y_buf):
    any_spec = pl.BlockSpec(memory_space=pl.ANY)
    n_prefetch = 5
    return pl.pallas_call(
        functools.partial(_experts_kernel, layer),
        grid_spec=pltpu.PrefetchScalarGridSpec(
            num_scalar_prefetch=n_prefetch,
            grid=(1,),
            in_specs=[any_spec] * 5,
            out_specs=any_spec,
            scratch_shapes=[
                pltpu.SMEM((N_SLOTS,), jnp.int32),
                pltpu.VMEM((X_BUFS, TE, D_MODEL), _f32),
                pltpu.VMEM((Y_BUFS, TE, D_MODEL), _f32),
                pltpu.VMEM((W_SLOTS, D_MODEL, D_EXPERT), _f32),
                pltpu.VMEM((W_SLOTS, D_MODEL, D_EXPERT), _f32),
                pltpu.VMEM((W_SLOTS, D_EXPERT, D_MODEL), _f32),
                pltpu.VMEM((D_MODEL, D_EXPERT), _bf16),
                pltpu.VMEM((D_MODEL, D_EXPERT), _bf16),
                pltpu.VMEM((D_EXPERT, D_MODEL), _bf16),
                pltpu.SemaphoreType.DMA((X_BUFS,)),
                pltpu.SemaphoreType.DMA((Y_BUFS,)),
                pltpu.SemaphoreType.DMA((W_SLOTS,)),
            ],
        ),
        out_shape=jax.ShapeDtypeStruct((N_SLOTS, D_MODEL), _f32),
        input_output_aliases={n_prefetch + 4: 0},
        compiler_params=pltpu.CompilerParams(
            dimension_semantics=("arbitrary",), vmem_limit_bytes=VMEM_LIMIT),
        name="experts",
    )(tile_expert, n_used, counts, offsets, slot_flat, x1, wg, wu, wd, y_buf)


def _combine_kernel(slot_ref, x1_ref, route_ref, g2_ref, b2_ref, y_hbm, out_p_ref, out_s_ref,
                    ybuf, sems):
    i = pl.program_id(0)
    b = i % 2

    def gather(tile, slot):
        tok0 = tile * TM

        def start(r, carry):
            for k in range(2):
                pltpu.make_async_copy(y_hbm.at[pl.ds(slot_ref[k * N_TOK + tok0 + r], 1)],
                                      ybuf.at[slot, k, pl.ds(r, 1)], sems.at[slot, k]).start(priority=k)
            return carry

        lax.fori_loop(0, TM, start, 0, unroll=16)

    @pl.when(i == 0)
    def _():
        gather(0, 0)

    @pl.when(i + 1 < pl.num_programs(0))
    def _():
        gather(i + 1, 1 - b)

    for k in range(2):
        pltpu.make_async_copy(y_hbm.at[pl.ds(0, TM)], ybuf.at[b, k], sems.at[b, k]).wait()

    route = route_ref[...]
    moe = route[:, 2:3] * ybuf[b, 0] + route[:, 3:4] * ybuf[b, 1]
    out = _layer_norm(ALPHA * x1_ref[...] + moe, g2_ref[...], b2_ref[...])

    @pl.when(i < N_PROMPT_TILES)
    def _():
        out_p_ref[...] = out

    @pl.when(i >= N_PROMPT_TILES)
    def _():
        out_s_ref[...] = out


def _combine_call(slot_flat, x1, route, g2, b2, y):
    tok = lambda i, s: (i, 0)
    const = lambda i, s: (0, 0)
    return pl.pallas_call(
        _combine_kernel,
        grid_spec=pltpu.PrefetchScalarGridSpec(
            num_scalar_prefetch=1,
            grid=(N_TILES,),
            in_specs=[
                pl.BlockSpec((TM, D_MODEL), tok),
                pl.BlockSpec((TM, LANES), tok),
                pl.BlockSpec((1, D_MODEL), const),
                pl.BlockSpec((1, D_MODEL), const),
                pl.BlockSpec(memory_space=pl.ANY),
            ],
            out_specs=[
                pl.BlockSpec((TM, D_MODEL), lambda i, s: (jnp.minimum(i, N_PROMPT_TILES - 1), 0)),
                pl.BlockSpec((TM, D_MODEL), lambda i, s: (jnp.maximum(i - N_PROMPT_TILES, 0), 0)),
            ],
            scratch_shapes=[
                pltpu.VMEM((2, 2, TM, D_MODEL), _f32),
                pltpu.SemaphoreType.DMA((2, 2)),
            ],
        ),
        out_shape=[jax.ShapeDtypeStruct((N_PROMPT, D_MODEL), _f32),
                   jax.ShapeDtypeStruct((N_SAMPLE, D_MODEL), _f32)],
        compiler_params=pltpu.CompilerParams(dimension_semantics=("arbitrary",)),
        name="combine",
    )(slot_flat, x1, route, g2, b2, y)


def kernel(x_prompt, x_sample, state_conv, state_pool, w_in, conv_w, sgu_ln_g, sgu_ln_b, sgu_w, sgu_b, pool_w, pool_scale, w_out, ln1_g, ln1_b, router_group_w, router_group_b, router_expert_w, router_expert_b, expert_w_gate, expert_w_up, expert_w_down, ln2_g, ln2_b):
    y_buf = jnp.zeros((N_SLOTS, D_MODEL), _f32)
    reps = CHUNK // DEC_SEQ
    row = lambda a: a.reshape(DEPTH, 1, -1)

    w_r = jnp.concatenate([router_group_w, router_expert_w], axis=2)
    w_r = jnp.pad(w_r, ((0, 0), (0, 0), (0, LANES - w_r.shape[2])))
    wr_hi = w_r.astype(_bf16)
    wr = jnp.concatenate([wr_hi, (w_r - wr_hi.astype(_f32)).astype(_bf16)], axis=2)
    b_r = jnp.concatenate([router_group_b, router_expert_b], axis=1)
    b_r = row(jnp.pad(b_r, ((0, 0), (0, LANES - b_r.shape[1]))))
    ws_s = jnp.tile(sgu_w[:, :, :DEC_SEQ, :DEC_SEQ], (1, 1, reps, reps))
    bias_p = jnp.swapaxes(sgu_b, 1, 2)
    bias_s = jnp.tile(bias_p[:, :DEC_SEQ], (1, reps, 1))
    zst = jnp.pad(state_conv, ((0, 0), (0, 0), (CONV_HALO - 2, 0), (0, 0)))
    pst = jnp.pad(state_pool, ((0, 0), (0, 0), (POOL_HALO - POOL_BUF, 0), (0, 0)))
    mixer_params = (w_in.astype(_bf16), w_out.astype(_bf16), conv_w, row(sgu_ln_g), row(sgu_ln_b), sgu_w,
                    ws_s, bias_p, bias_s, pool_w, row(pool_scale), row(ln1_g), row(ln1_b), wr, b_r, zst, pst)
    g2, b2 = row(ln2_g), row(ln2_b)

    ztails, ptails, zss, psamps, v_s = [], [], [], [], []
    x_src = (x_prompt.reshape(N_PROMPT, D_MODEL), x_sample.reshape(N_SAMPLE, D_MODEL))
    for l in range(DEPTH):
        x1, route, ztail, ptail, zs, psamp, vns = _mixer_call(l, x_src, mixer_params)
        ids = route[:, 0:2].astype(jnp.int32).T.reshape(ASSIGN_ROWS, LANES)
        slot, tile_expert, n_used, counts, offsets = _slots_call(ids)
        slot_flat = slot.reshape(N_ASSIGN)
        y = _experts_call(l, tile_expert.reshape(LANES), n_used.reshape(LANES)[:1], counts.reshape(LANES),
                          offsets.reshape(LANES), slot_flat, x1, expert_w_gate, expert_w_up, expert_w_down,
                          y_buf)
        x_src = (slot_flat, x1, route, y, g2, b2)
        y_buf = y
        ztails.append(ztail)
        ptails.append(ptail)
        zss.append(zs)
        psamps.append(psamp)
        v_s.append(vns)

    x_p, x_s = _combine_call(slot_flat, x1, route, g2[DEPTH - 1], b2[DEPTH - 1], y)
    y_prompt = x_p.reshape(BATCH, SEQ, D_MODEL)
    y_sample = x_s.reshape(DEC_BATCH, DEC_SEQ, D_MODEL)
    new_conv_prompt = jnp.stack(ztails)[:, :BATCH, CONV_HALO - 2:]
    new_pool_prompt = jnp.stack(ptails)[:, :BATCH, POOL_HALO - POOL_BUF:]
    new_conv_sample = jnp.stack(zss)[:, :, DEC_SEQ - 2:]
    new_pool_sample = jnp.concatenate([state_pool[:, :, DEC_SEQ:], jnp.stack(psamps)], axis=2)
    return (y_prompt, y_sample, new_conv_prompt, new_pool_prompt, new_conv_sample, new_pool_sample,
            jnp.stack(v_s))
```

```python
import functools

import jax
import jax.numpy as jnp
from jax import lax
from jax.experimental import pallas as pl
from jax.experimental.pallas import tpu as pltpu

D_MODEL = 2048
BATCH = 4
SEQ = 2048
DEPTH = 4
DEC_BATCH = 128
DEC_SEQ = 8
PAST_LEN = 16384
HEAD_DIM = 128
CONV_WIDTH = 768
POOL_WIDTH = 512
CHUNK_WIDTH = 768
CHUNK = 128
N_CHUNK_HEADS = 6
POOL_WINDOWS = (2, 4, 8, 16)
POOL_GROUP_DIM = 128
POOL_BUF = 15
IN_WIDTH = 4352
N_EXPERT_GROUPS = 4
EXPERTS_PER_GROUP = 4
N_EXPERTS = 16
D_EXPERT = 512
ALPHA = (2 * DEPTH) ** 0.25
LN_EPS = 1e-5

N_PROMPT = BATCH * SEQ
N_SAMPLE = DEC_BATCH * DEC_SEQ
N_TOK = N_PROMPT + N_SAMPLE
N_ASSIGN = 2 * N_TOK

LANES = 128
SUBLANES = 8
CONV_HALO = 8
POOL_HALO = 16

TM = 256
TILES_PER_SEQ = SEQ // TM
N_PROMPT_TILES = N_PROMPT // TM
N_SAMPLE_TILES = N_SAMPLE // TM
N_TILES = N_PROMPT_TILES + N_SAMPLE_TILES
SEQS_PER_TILE = TM // DEC_SEQ

TE = 256
N_SLOTS = N_ASSIGN + N_EXPERTS * TE
N_ETILES = N_SLOTS // TE
ASSIGN_ROWS = N_ASSIGN // LANES

CUT_B, CUT_C, CUT_H, CUT_U, CUT_V, CUT_P = 0, 768, 1536, 2304, 3072, 3840

VMEM_LIMIT = 62 * 1024 * 1024

_f32 = jnp.float32
_bf16 = jnp.bfloat16


def _dot(a, b):
    return jnp.dot(a, b, preferred_element_type=_f32)


def _layer_norm(r, g, b):
    mu = jnp.mean(r, axis=-1, keepdims=True)
    c = r - mu
    var = jnp.mean(c * c, axis=-1, keepdims=True)
    return c * lax.rsqrt(var + LN_EPS) * g + b


class _RowGather:
    def __init__(self, start_row, first_row):
        self._start_row = start_row
        self._done = first_row

    def upto(self, row_end):
        for r in range(self._done, row_end):
            self._start_row(r)
        self._done = max(self._done, row_end)


class _NoGather:
    def upto(self, row_end):
        del row_end


BRANCH_GATHER_ROWS = 192


def _mixer_body(xb, w_in_ref, conv_w_ref, sg_ref, sb_ref, ws_ref, bias_ref, pw_ref, ps_ref,
                zext_ref, pext_ref, ybuf_ref, seq_len, pos, chunk_len, gather):
    n_seq = TM // seq_len

    z = _dot(xb, w_in_ref[:, CUT_C:CUT_H]) * _dot(xb, w_in_ref[:, CUT_H:CUT_U])
    zext_ref[:, CONV_HALO:, :] = z.reshape(n_seq, seq_len, CONV_WIDTH)
    gather.upto(16)
    cw = conv_w_ref[...]
    conv = (cw[0:1, :] * zext_ref[:, CONV_HALO - 2:CONV_HALO - 2 + seq_len, :]
            + cw[1:2, :] * zext_ref[:, CONV_HALO - 1:CONV_HALO - 1 + seq_len, :]
            + cw[2:3, :] * zext_ref[:, CONV_HALO:, :])
    y_conv = _dot(xb, w_in_ref[:, CUT_B:CUT_C]) * conv.reshape(TM, CONV_WIDTH)
    ybuf_ref[:, 0:CONV_WIDTH] = y_conv.astype(_bf16)
    gather.upto(32)

    v = _dot(xb, w_in_ref[:, CUT_V:CUT_P])
    vn = _layer_norm(v, sg_ref[...], sb_ref[...])
    vnb = vn.astype(_bf16)
    gather.upto(48)
    r_i = lax.broadcasted_iota(jnp.int32, (CHUNK, CHUNK), 0)
    c_i = lax.broadcasted_iota(jnp.int32, (CHUNK, CHUNK), 1)
    shift = chunk_len.bit_length() - 1
    mask = (c_i <= r_i) & ((r_i >> shift) == (c_i >> shift))
    u = _dot(xb, w_in_ref[:, CUT_U:CUT_V])
    gather.upto(64)
    n_chunks = TM // CHUNK
    for h in range(N_CHUNK_HEADS):
        m_h = jnp.where(mask, ws_ref[h], 0.0).astype(_bf16)
        b_h = bias_ref[:, h:h + 1]
        cols = slice(h * HEAD_DIM, (h + 1) * HEAD_DIM)
        vn_h = jnp.concatenate([vnb[c * CHUNK:(c + 1) * CHUNK, cols] for c in range(n_chunks)], axis=1)
        mixed = _dot(m_h, vn_h)
        for c in range(n_chunks):
            rows = slice(c * CHUNK, (c + 1) * CHUNK)
            ybuf_ref[rows, CONV_WIDTH + h * HEAD_DIM:CONV_WIDTH + (h + 1) * HEAD_DIM] = (
                u[rows, cols] * (mixed[:, c * HEAD_DIM:(c + 1) * HEAD_DIM] + b_h)).astype(_bf16)
        gather.upto(64 + 8 * (h + 1))

    p = _dot(xb, w_in_ref[:, CUT_P:IN_WIDTH])
    pext_ref[:, POOL_HALO:, :] = p.reshape(n_seq, seq_len, POOL_WIDTH)
    gather.upto(128)
    posf = (pos + 1).astype(_f32)
    for g, w in enumerate(POOL_WINDOWS):
        cols = slice(g * POOL_GROUP_DIM, (g + 1) * POOL_GROUP_DIM)
        s = pext_ref[:, POOL_HALO:, cols]
        for j in range(1, w):
            s = s + pext_ref[:, POOL_HALO - j:POOL_HALO - j + seq_len, cols]
        cnt = jnp.minimum(float(w), posf)
        diff = s.reshape(TM, POOL_GROUP_DIM) / cnt - p[:, cols]
        y_g = _dot(diff.astype(_bf16), pw_ref[g].astype(_bf16)) * ps_ref[:, cols]
        ybuf_ref[:, CONV_WIDTH + CHUNK_WIDTH + g * POOL_GROUP_DIM:
                 CONV_WIDTH + CHUNK_WIDTH + (g + 1) * POOL_GROUP_DIM] = y_g.astype(_bf16)
        gather.upto(128 + 16 * (g + 1))
    assert 128 + 16 * len(POOL_WINDOWS) == BRANCH_GATHER_ROWS
    return z, p, vn


def _route(x1, wr_ref, br_ref):
    x_hi = x1.astype(_bf16)
    x_lo = (x1 - x_hi.astype(_f32)).astype(_bf16)
    hi_parts = _dot(x_hi, wr_ref[...])
    logits = hi_parts[:, :LANES] + hi_parts[:, LANES:] + _dot(x_lo, wr_ref[:, :LANES]) + br_ref[...]
    lane = lax.broadcasted_iota(jnp.int32, logits.shape, 1)
    neg = -jnp.inf
    big = jnp.int32(1 << 20)

    gmask = lane < N_EXPERT_GROUPS
    gl = jnp.where(gmask, logits, neg)
    g_max = jnp.max(gl, axis=1, keepdims=True)
    g_idx = jnp.min(jnp.where(gmask & (gl == g_max), lane, big), axis=1, keepdims=True)
    g_val = 1.0 / jnp.sum(jnp.exp(gl - g_max), axis=1, keepdims=True)

    e_lane = lane - N_EXPERT_GROUPS
    emask = (e_lane >= 0) & (e_lane < N_EXPERTS) & ((e_lane >> (EXPERTS_PER_GROUP.bit_length() - 1)) == g_idx)
    el = jnp.where(emask, logits, neg)
    e_max = jnp.max(el, axis=1, keepdims=True)
    i1 = jnp.min(jnp.where(emask & (el == e_max), e_lane, big), axis=1, keepdims=True)
    e_sum = jnp.sum(jnp.exp(el - e_max), axis=1, keepdims=True)
    mask2 = emask & (e_lane != i1)
    e_second = jnp.max(jnp.where(mask2, el, neg), axis=1, keepdims=True)
    i2 = jnp.min(jnp.where(mask2 & (el == e_second), e_lane, big), axis=1, keepdims=True)
    v1 = 1.0 / e_sum
    v2 = jnp.exp(e_second - e_max) / e_sum
    tot = v1 + v2
    gate1 = g_val * (v1 / tot)
    gate2 = g_val * (v2 / tot)
    return jnp.where(lane == 0, i1.astype(_f32),
                     jnp.where(lane == 1, i2.astype(_f32),
                               jnp.where(lane == 2, gate1, jnp.where(lane == 3, gate2, 0.0))))


N_MIXER_COMMON_REFS = 29


def _mixer_kernel(fused, *refs):
    if fused:
        slot_ref, x1p_ref, routep_ref, y_hbm, g2_ref, b2_ref = refs[:6]
        refs = refs[6:]
    else:
        xp_ref, xsamp_ref = refs[:2]
        refs = refs[2:]
    (w_in_ref, w_out_ref, conv_w_ref, sg_ref, sb_ref, ws_p_ref, ws_s_ref, bias_p_ref, bias_s_ref,
     pw_ref, ps_ref, g1_ref, b1_ref, wr_ref, br_ref, zst_ref, pst_ref,
     x1_ref, route_ref, ztail_ref, ptail_ref, zs_ref, psamp_ref, vns_ref,
     zext_p, pext_p, zext_s, pext_s, ybuf_ref) = refs[:N_MIXER_COMMON_REFS]
    i = pl.program_id(0)
    last = pl.num_programs(0) - 1

    if fused:
        gbuf, sems = refs[N_MIXER_COMMON_REFS:]
        b = i % 2

        def row_starter(tile, half):
            tok0 = tile * TM

            def start_row(r):
                for k in range(2):
                    pltpu.make_async_copy(y_hbm.at[pl.ds(slot_ref[k * N_TOK + tok0 + r], 1)],
                                          gbuf.at[half, k, pl.ds(r, 1)], sems.at[half, k]).start(priority=k)
            return start_row

        def wait_rows(half):
            for k in range(2):
                pltpu.make_async_copy(y_hbm.at[pl.ds(0, TM)], gbuf.at[half, k], sems.at[half, k]).wait()

        @pl.when(i == 0)
        def _():
            first = row_starter(0, 0)

            def body(r, carry):
                first(r)
                return carry

            lax.fori_loop(0, TM, body, 0, unroll=16)

        wait_rows(b)
        routep = routep_ref[...]
        moe = routep[:, 2:3] * gbuf[b, 0] + routep[:, 3:4] * gbuf[b, 1]
        x = _layer_norm(ALPHA * x1p_ref[...] + moe, g2_ref[...], b2_ref[...])
        next_rows = row_starter(jnp.minimum(i + 1, last), 1 - b)
        new_gather = lambda first_row: _RowGather(next_rows, first_row)
    else:
        x = jnp.where(i < N_PROMPT_TILES, xp_ref[...], xsamp_ref[...])
        new_gather = lambda first_row: _NoGather()

    xb = x.astype(_bf16)
    row = lax.broadcasted_iota(jnp.int32, (TM, 1), 0)

    @pl.when(i < N_PROMPT_TILES)
    def _prompt():
        t0 = (i % TILES_PER_SEQ) * TM

        @pl.when(t0 == 0)
        def _():
            zext_p[:, 0:CONV_HALO, :] = jnp.zeros((1, CONV_HALO, CONV_WIDTH), _f32)
            pext_p[:, 0:POOL_HALO, :] = jnp.zeros((1, POOL_HALO, POOL_WIDTH), _f32)

        z, p, _ = _mixer_body(xb, w_in_ref, conv_w_ref, sg_ref, sb_ref, ws_p_ref, bias_p_ref,
                              pw_ref, ps_ref, zext_p, pext_p, ybuf_ref, TM, t0 + row, CHUNK, new_gather(0))
        z_last = z[TM - CONV_HALO:, :].reshape(1, CONV_HALO, CONV_WIDTH)
        p_last = p[TM - POOL_HALO:, :].reshape(1, POOL_HALO, POOL_WIDTH)
        zext_p[:, 0:CONV_HALO, :] = z_last
        pext_p[:, 0:POOL_HALO, :] = p_last
        ztail_ref[...] = z_last
        ptail_ref[...] = p_last

    @pl.when(i >= N_PROMPT_TILES)
    def _sample():
        zext_s[:, 0:CONV_HALO, :] = zst_ref[...]
        pext_s[:, 0:POOL_HALO, :] = pst_ref[...]
        z, p, vn = _mixer_body(xb, w_in_ref, conv_w_ref, sg_ref, sb_ref, ws_s_ref, bias_s_ref,
                               pw_ref, ps_ref, zext_s, pext_s, ybuf_ref, DEC_SEQ,
                               PAST_LEN + (row & (DEC_SEQ - 1)), DEC_SEQ, new_gather(0))
        zs_ref[...] = z.reshape(SEQS_PER_TILE, DEC_SEQ, CONV_WIDTH)
        psamp_ref[...] = p.reshape(SEQS_PER_TILE, DEC_SEQ, POOL_WIDTH)
        vns_ref[...] = vn.reshape(SEQS_PER_TILE, DEC_SEQ, CHUNK_WIDTH)
        ztail_ref[...] = jnp.zeros(ztail_ref.shape, _f32)
        ptail_ref[...] = jnp.zeros(ptail_ref.shape, _f32)

    tail_gather = new_gather(BRANCH_GATHER_ROWS)
    mix = _dot(ybuf_ref[...], w_out_ref[...])
    tail_gather.upto(BRANCH_GATHER_ROWS + (TM - BRANCH_GATHER_ROWS) // 2)
    x1 = _layer_norm(ALPHA * x + mix, g1_ref[...], b1_ref[...])
    x1_ref[...] = x1
    tail_gather.upto(TM)
    route_ref[...] = _route(x1, wr_ref, br_ref)

    if fused:
        @pl.when(i == last)
        def _():
            wait_rows(1 - b)


def _layer_spec(layer, shape):
    nd = len(shape)
    return pl.BlockSpec((None,) + shape, lambda i, *_: (layer,) + (0,) * nd, pipeline_mode=pl.Buffered(1))


def _mixer_call(layer, x_src, params):
    fused = len(x_src) == 6
    samp = lambda i, *_: (jnp.maximum(i - N_PROMPT_TILES, 0), 0, 0)
    samp_l = lambda i, *_: (layer, jnp.maximum(i - N_PROMPT_TILES, 0), 0, 0)
    tail = lambda i, *_: (i // TILES_PER_SEQ, 0, 0)
    tok = lambda i, *_: (i, 0)
    if fused:
        x_specs = [
            pl.BlockSpec((TM, D_MODEL), tok),
            pl.BlockSpec((TM, LANES), tok),
            pl.BlockSpec(memory_space=pl.ANY),
            _layer_spec(layer - 1, (1, D_MODEL)),
            _layer_spec(layer - 1, (1, D_MODEL)),
        ]
    else:
        x_specs = [
            pl.BlockSpec((TM, D_MODEL), lambda i, *_: (jnp.minimum(i, N_PROMPT_TILES - 1), 0)),
            pl.BlockSpec((TM, D_MODEL), lambda i, *_: (jnp.maximum(i - N_PROMPT_TILES, 0), 0)),
        ]
    in_specs = x_specs + [
        _layer_spec(layer, (D_MODEL, IN_WIDTH)),
        _layer_spec(layer, (D_MODEL, D_MODEL)),
        _layer_spec(layer, (3, CONV_WIDTH)),
        _layer_spec(layer, (1, CHUNK_WIDTH)),
        _layer_spec(layer, (1, CHUNK_WIDTH)),
        _layer_spec(layer, (N_CHUNK_HEADS, CHUNK, CHUNK)),
        _layer_spec(layer, (N_CHUNK_HEADS, CHUNK, CHUNK)),
        _layer_spec(layer, (CHUNK, N_CHUNK_HEADS)),
        _layer_spec(layer, (CHUNK, N_CHUNK_HEADS)),
        _layer_spec(layer, (4, POOL_GROUP_DIM, POOL_GROUP_DIM)),
        _layer_spec(layer, (1, POOL_WIDTH)),
        _layer_spec(layer, (1, D_MODEL)),
        _layer_spec(layer, (1, D_MODEL)),
        _layer_spec(layer, (D_MODEL, 2 * LANES)),
        _layer_spec(layer, (1, LANES)),
        pl.BlockSpec((None, SEQS_PER_TILE, CONV_HALO, CONV_WIDTH), samp_l),
        pl.BlockSpec((None, SEQS_PER_TILE, POOL_HALO, POOL_WIDTH), samp_l),
    ]
    out_specs = [
        pl.BlockSpec((TM, D_MODEL), tok),
        pl.BlockSpec((TM, LANES), tok),
        pl.BlockSpec((1, CONV_HALO, CONV_WIDTH), tail),
        pl.BlockSpec((1, POOL_HALO, POOL_WIDTH), tail),
        pl.BlockSpec((SEQS_PER_TILE, DEC_SEQ, CONV_WIDTH), samp),
        pl.BlockSpec((SEQS_PER_TILE, DEC_SEQ, POOL_WIDTH), samp),
        pl.BlockSpec((SEQS_PER_TILE, DEC_SEQ, CHUNK_WIDTH), samp),
    ]
    out_shape = [
        jax.ShapeDtypeStruct((N_TOK, D_MODEL), _f32),
        jax.ShapeDtypeStruct((N_TOK, LANES), _f32),
        jax.ShapeDtypeStruct((BATCH + 1, CONV_HALO, CONV_WIDTH), _f32),
        jax.ShapeDtypeStruct((BATCH + 1, POOL_HALO, POOL_WIDTH), _f32),
        jax.ShapeDtypeStruct((DEC_BATCH, DEC_SEQ, CONV_WIDTH), _f32),
        jax.ShapeDtypeStruct((DEC_BATCH, DEC_SEQ, POOL_WIDTH), _f32),
        jax.ShapeDtypeStruct((DEC_BATCH, DEC_SEQ, CHUNK_WIDTH), _f32),
    ]
    scratch = [
        pltpu.VMEM((1, CONV_HALO + TM, CONV_WIDTH), _f32),
        pltpu.VMEM((1, POOL_HALO + TM, POOL_WIDTH), _f32),
        pltpu.VMEM((SEQS_PER_TILE, CONV_HALO + DEC_SEQ, CONV_WIDTH), _f32),
        pltpu.VMEM((SEQS_PER_TILE, POOL_HALO + DEC_SEQ, POOL_WIDTH), _f32),
        pltpu.VMEM((TM, D_MODEL), _bf16),
    ]
    assert len(in_specs) - len(x_specs) + len(out_specs) + len(scratch) == N_MIXER_COMMON_REFS
    if fused:
        scratch += [pltpu.VMEM((2, 2, TM, D_MODEL), _f32), pltpu.SemaphoreType.DMA((2, 2))]
    return pl.pallas_call(
        functools.partial(_mixer_kernel, fused),
        grid_spec=pltpu.PrefetchScalarGridSpec(
            num_scalar_prefetch=1 if fused else 0,
            grid=(N_TILES,),
            in_specs=in_specs,
            out_specs=out_specs,
            scratch_shapes=scratch,
        ),
        out_shape=out_shape,
        compiler_params=pltpu.CompilerParams(
            dimension_semantics=("arbitrary",), vmem_limit_bytes=VMEM_LIMIT),
        name="mixer",
    )(*x_src, *params)


def _slots_kernel(ids_ref, slot_ref, tile_expert_ref, n_used_ref, count_ref, offset_ref):
    ids = ids_ref[...]
    lane = lax.broadcasted_iota(jnp.int32, (1, LANES), 1)
    counts = jnp.zeros((1, LANES), _f32)
    offsets = jnp.zeros((1, LANES), _f32)
    r_i = lax.broadcasted_iota(jnp.int32, (LANES, LANES), 0)
    c_i = lax.broadcasted_iota(jnp.int32, (LANES, LANES), 1)
    upper = (r_i < c_i).astype(_bf16)
    rr = lax.broadcasted_iota(jnp.int32, (ASSIGN_ROWS, ASSIGN_ROWS), 0)
    rc = lax.broadcasted_iota(jnp.int32, (ASSIGN_ROWS, ASSIGN_ROWS), 1)
    lower = (rc < rr).astype(_bf16)
    tile_row0 = (lax.broadcasted_iota(jnp.int32, (1, LANES), 1) * TE).astype(_f32)

    slot = jnp.zeros((ASSIGN_ROWS, LANES), _f32)
    tile_expert = jnp.zeros((1, LANES), _f32)
    off = jnp.zeros((1, 1), _f32)
    for e in range(N_EXPERTS):
        m = (ids == e).astype(_f32)
        within = _dot(m.astype(_bf16), upper)
        rowsum = jnp.sum(m, axis=1, keepdims=True)
        rowpre = _dot(lower, jnp.broadcast_to(rowsum, (ASSIGN_ROWS, LANES)).astype(_bf16))
        cnt = jnp.sum(rowsum, axis=0, keepdims=True)
        padded = jnp.floor((cnt + (TE - 1)) * (1.0 / TE)) * TE
        slot = slot + m * (off + rowpre + within)
        in_seg = (tile_row0 >= off) & (tile_row0 < off + padded)
        tile_expert = tile_expert + jnp.where(in_seg, float(e), 0.0)
        counts = counts + jnp.where(lane == e, cnt, 0.0)
        offsets = offsets + jnp.where(lane == e, off, 0.0)
        off = off + padded
    tile_expert = jnp.where(tile_row0 >= off, float(N_EXPERTS - 1), tile_expert)
    slot_ref[...] = slot.astype(jnp.int32)
    tile_expert_ref[...] = tile_expert.astype(jnp.int32)
    n_used_ref[...] = jnp.broadcast_to(off * (1.0 / TE), (1, LANES)).astype(jnp.int32)
    count_ref[...] = counts.astype(jnp.int32)
    offset_ref[...] = offsets.astype(jnp.int32)


def _slots_call(ids):
    return pl.pallas_call(
        _slots_kernel,
        out_shape=[jax.ShapeDtypeStruct((ASSIGN_ROWS, LANES), jnp.int32)]
        + [jax.ShapeDtypeStruct((1, LANES), jnp.int32)] * 4,
        name="slots",
    )(ids)


X_BUFS = 4
Y_BUFS = 3
W_SLOTS = 3


def _experts_kernel(layer, te_ref, nu_ref, count_ref, offset_ref, slot_ref,
                    x1_hbm, wg_hbm, wu_hbm, wd_hbm, y_hbm,
                    tok_of_slot, xbuf, ybuf, wg_st, wu_st, wd_st, wg_b, wu_b, wd_b, sem_x, sem_y, sem_w):
    n_used = nu_ref[0]
    tile_shift = TE.bit_length() - 1

    def tiles_of(expert):
        return (count_ref[expert] + (TE - 1)) >> tile_shift

    def weight_copies(expert, ws):
        return [pltpu.make_async_copy(src.at[layer, expert], dst.at[ws], sem_w.at[ws])
                for src, dst in ((wg_hbm, wg_st), (wu_hbm, wu_st), (wd_hbm, wd_st))]

    for c in weight_copies(te_ref[0], 0):
        c.start()
    second_first = tiles_of(te_ref[0])

    @pl.when(second_first < n_used)
    def _():
        for c in weight_copies(te_ref[second_first], 1):
            c.start()

    for k in range(2):
        def fill(t, carry, k=k):
            tok_of_slot[slot_ref[k * N_TOK + t]] = t
            return carry
        lax.fori_loop(0, N_TOK, fill, 0, unroll=16)
    for e in range(N_EXPERTS):
        first_pad = offset_ref[e] + count_ref[e]
        seg_end = offset_ref[e] + (((count_ref[e] + (TE - 1)) >> tile_shift) << tile_shift)

        def pad(s, carry):
            tok_of_slot[s] = 0
            return carry
        lax.fori_loop(first_pad, seg_end, pad, 0)

    def gather(tile, buf):
        base = tile * TE
        for r in range(TE):
            pltpu.make_async_copy(x1_hbm.at[pl.ds(tok_of_slot[base + r], 1)],
                                  xbuf.at[buf, pl.ds(r, 1)], sem_x.at[buf]).start(priority=r % 2)

    def y_copy(tile, buf):
        return pltpu.make_async_copy(ybuf.at[buf], y_hbm.at[pl.ds(tile * TE, TE)], sem_y.at[buf])

    gather(0, 0)
    for ahead in range(1, X_BUFS - 1):
        @pl.when(n_used > ahead)
        def _(ahead=ahead):
            gather(ahead, ahead)

    def tile_step(j, seg):
        expert = te_ref[j]
        first = (j == 0) | (te_ref[jnp.maximum(j - 1, 0)] != expert)

        @pl.when(j + (X_BUFS - 1) < n_used)
        def _():
            gather(j + (X_BUFS - 1), (j + (X_BUFS - 1)) % X_BUFS)

        @pl.when(first)
        def _():
            ws = seg % W_SLOTS
            for c in weight_copies(expert, ws):
                c.wait()
            wg_b[...] = wg_st[ws].astype(_bf16)
            wu_b[...] = wu_st[ws].astype(_bf16)
            wd_b[...] = wd_st[ws].astype(_bf16)
            next_first = j + tiles_of(expert)

            @pl.when(next_first < n_used)
            def _():
                after_next = next_first + tiles_of(te_ref[next_first])

                @pl.when(after_next < n_used)
                def _():
                    for c in weight_copies(te_ref[after_next], (seg + 2) % W_SLOTS):
                        c.start()

        xslot = j % X_BUFS
        pltpu.make_async_copy(x1_hbm.at[pl.ds(0, TE)], xbuf.at[xslot], sem_x.at[xslot]).wait()
        xb = xbuf[xslot].astype(_bf16)
        hg = _dot(xb, wg_b[...])
        hu = _dot(xb, wu_b[...])
        a = hg / (1.0 + jnp.exp(-hg)) * hu
        y = _dot(a.astype(_bf16), wd_b[...])

        yslot = j % Y_BUFS

        @pl.when(j >= Y_BUFS)
        def _():
            y_copy(j - Y_BUFS, yslot).wait()

        ybuf[yslot] = y
        y_copy(j, yslot).start()
        return seg + first.astype(jnp.int32)

    lax.fori_loop(0, n_used, tile_step, jnp.int32(0))

    for back in range(1, Y_BUFS + 1):
        @pl.when(n_used >= back)
        def _(back=back):
            y_copy(n_used - back, (n_used - back) % Y_BUFS).wait()

    ybuf[0] = jnp.zeros((TE, D_MODEL), _f32)

    def zero_tile(j, carry):
        y_copy(j, 0).start()
        y_copy(j, 0).wait()
        return carry

    lax.fori_loop(n_used, N_ETILES, zero_tile, 0)


def _experts_call(layer, tile_expert, n_used, counts, offsets, slot_flat, x1, wg, wu, wd):
    any_spec = pl.BlockSpec(memory_space=pl.ANY)
    n_prefetch = 5
    return pl.pallas_call(
        functools.partial(_experts_kernel, layer),
        grid_spec=pltpu.PrefetchScalarGridSpec(
            num_scalar_prefetch=n_prefetch,
            grid=(1,),
            in_specs=[any_spec] * 4,
            out_specs=any_spec,
            scratch_shapes=[
                pltpu.SMEM((N_SLOTS,), jnp.int32),
                pltpu.VMEM((X_BUFS, TE, D_MODEL), _f32),
                pltpu.VMEM((Y_BUFS, TE, D_MODEL), _f32),
                pltpu.VMEM((W_SLOTS, D_MODEL, D_EXPERT), _f32),
                pltpu.VMEM((W_SLOTS, D_MODEL, D_EXPERT), _f32),
                pltpu.VMEM((W_SLOTS, D_EXPERT, D_MODEL), _f32),
                pltpu.VMEM((D_MODEL, D_EXPERT), _bf16),
                pltpu.VMEM((D_MODEL, D_EXPERT), _bf16),
                pltpu.VMEM((D_EXPERT, D_MODEL), _bf16),
                pltpu.SemaphoreType.DMA((X_BUFS,)),
                pltpu.SemaphoreType.DMA((Y_BUFS,)),
                pltpu.SemaphoreType.DMA((W_SLOTS,)),
            ],
        ),
        out_shape=jax.ShapeDtypeStruct((N_SLOTS, D_MODEL), _f32),
        compiler_params=pltpu.CompilerParams(
            dimension_semantics=("arbitrary",), vmem_limit_bytes=VMEM_LIMIT),
        name="experts",
    )(tile_expert, n_used, counts, offsets, slot_flat, x1, wg, wu, wd)


def _combine_kernel(slot_ref, x1_ref, route_ref, g2_ref, b2_ref, y_hbm, out_p_ref, out_s_ref,
                    ybuf, sems):
    i = pl.program_id(0)
    b = i % 2

    def gather(tile, slot):
        tok0 = tile * TM

        def start(r, carry):
            for k in range(2):
                pltpu.make_async_copy(y_hbm.at[pl.ds(slot_ref[k * N_TOK + tok0 + r], 1)],
                                      ybuf.at[slot, k, pl.ds(r, 1)], sems.at[slot, k]).start(priority=k)
            return carry

        lax.fori_loop(0, TM, start, 0, unroll=16)

    @pl.when(i == 0)
    def _():
        gather(0, 0)

    @pl.when(i + 1 < pl.num_programs(0))
    def _():
        gather(i + 1, 1 - b)

    for k in range(2):
        pltpu.make_async_copy(y_hbm.at[pl.ds(0, TM)], ybuf.at[b, k], sems.at[b, k]).wait()

    route = route_ref[...]
    moe = route[:, 2:3] * ybuf[b, 0] + route[:, 3:4] * ybuf[b, 1]
    out = _layer_norm(ALPHA * x1_ref[...] + moe, g2_ref[...], b2_ref[...])

    @pl.when(i < N_PROMPT_TILES)
    def _():
        out_p_ref[...] = out

    @pl.when(i >= N_PROMPT_TILES)
    def _():
        out_s_ref[...] = out


def _combine_call(slot_flat, x1, route, g2, b2, y):
    tok = lambda i, s: (i, 0)
    const = lambda i, s: (0, 0)
    return pl.pallas_call(
        _combine_kernel,
        grid_spec=pltpu.PrefetchScalarGridSpec(
            num_scalar_prefetch=1,
            grid=(N_TILES,),
            in_specs=[
                pl.BlockSpec((TM, D_MODEL), tok),
                pl.BlockSpec((TM, LANES), tok),
                pl.BlockSpec((1, D_MODEL), const),
                pl.BlockSpec((1, D_MODEL), const),
                pl.BlockSpec(memory_space=pl.ANY),
            ],
            out_specs=[
                pl.BlockSpec((TM, D_MODEL), lambda i, s: (jnp.minimum(i, N_PROMPT_TILES - 1), 0)),
                pl.BlockSpec((TM, D_MODEL), lambda i, s: (jnp.maximum(i - N_PROMPT_TILES, 0), 0)),
            ],
            scratch_shapes=[
                pltpu.VMEM((2, 2, TM, D_MODEL), _f32),
                pltpu.SemaphoreType.DMA((2, 2)),
            ],
        ),
        out_shape=[jax.ShapeDtypeStruct((N_PROMPT, D_MODEL), _f32),
                   jax.ShapeDtypeStruct((N_SAMPLE, D_MODEL), _f32)],
        compiler_params=pltpu.CompilerParams(dimension_semantics=("arbitrary",)),
        name="combine",
    )(slot_flat, x1, route, g2, b2, y)


def kernel(x_prompt, x_sample, state_conv, state_pool, w_in, conv_w, sgu_ln_g, sgu_ln_b, sgu_w, sgu_b, pool_w, pool_scale, w_out, ln1_g, ln1_b, router_group_w, router_group_b, router_expert_w, router_expert_b, expert_w_gate, expert_w_up, expert_w_down, ln2_g, ln2_b):
    reps = CHUNK // DEC_SEQ
    row = lambda a: a.reshape(DEPTH, 1, -1)

    w_r = jnp.concatenate([router_group_w, router_expert_w], axis=2)
    w_r = jnp.pad(w_r, ((0, 0), (0, 0), (0, LANES - w_r.shape[2])))
    wr_hi = w_r.astype(_bf16)
    wr = jnp.concatenate([wr_hi, (w_r - wr_hi.astype(_f32)).astype(_bf16)], axis=2)
    b_r = jnp.concatenate([router_group_b, router_expert_b], axis=1)
    b_r = row(jnp.pad(b_r, ((0, 0), (0, LANES - b_r.shape[1]))))
    ws_s = jnp.tile(sgu_w[:, :, :DEC_SEQ, :DEC_SEQ], (1, 1, reps, reps))
    bias_p = jnp.swapaxes(sgu_b, 1, 2)
    bias_s = jnp.tile(bias_p[:, :DEC_SEQ], (1, reps, 1))
    zst = jnp.pad(state_conv, ((0, 0), (0, 0), (CONV_HALO - 2, 0), (0, 0)))
    pst = jnp.pad(state_pool, ((0, 0), (0, 0), (POOL_HALO - POOL_BUF, 0), (0, 0)))
    mixer_params = (w_in.astype(_bf16), w_out.astype(_bf16), conv_w, row(sgu_ln_g), row(sgu_ln_b), sgu_w,
                    ws_s, bias_p, bias_s, pool_w, row(pool_scale), row(ln1_g), row(ln1_b), wr, b_r, zst, pst)
    g2, b2 = row(ln2_g), row(ln2_b)

    ztails, ptails, zss, psamps, v_s = [], [], [], [], []
    x_src = (x_prompt.reshape(N_PROMPT, D_MODEL), x_sample.reshape(N_SAMPLE, D_MODEL))
    for l in range(DEPTH):
        x1, route, ztail, ptail, zs, psamp, vns = _mixer_call(l, x_src, mixer_params)
        ids = route[:, 0:2].astype(jnp.int32).T.reshape(ASSIGN_ROWS, LANES)
        slot, tile_expert, n_used, counts, offsets = _slots_call(ids)
        slot_flat = slot.reshape(N_ASSIGN)
        y = _experts_call(l, tile_expert.reshape(LANES), n_used.reshape(LANES)[:1], counts.reshape(LANES),
                          offsets.reshape(LANES), slot_flat, x1, expert_w_gate, expert_w_up, expert_w_down)
        x_src = (slot_flat, x1, route, y, g2, b2)
        ztails.append(ztail)
        ptails.append(ptail)
        zss.append(zs)
        psamps.append(psamp)
        v_s.append(vns)

    x_p, x_s = _combine_call(slot_flat, x1, route, g2[DEPTH - 1], b2[DEPTH - 1], y)
    y_prompt = x_p.reshape(BATCH, SEQ, D_MODEL)
    y_sample = x_s.reshape(DEC_BATCH, DEC_SEQ, D_MODEL)
    new_conv_prompt = jnp.stack(ztails)[:, :BATCH, CONV_HALO - 2:]
    new_pool_prompt = jnp.stack(ptails)[:, :BATCH, POOL_HALO - POOL_BUF:]
    new_conv_sample = jnp.stack(zss)[:, :, DEC_SEQ - 2:]
    new_pool_sample = jnp.concatenate([state_pool[:, :, DEC_SEQ:], jnp.stack(psamps)], axis=2)
    return (y_prompt, y_sample, new_conv_prompt, new_pool_prompt, new_conv_sample, new_pool_sample,
            jnp.stack(v_s))
```

```python
import functools

import jax
import jax.numpy as jnp
from jax import lax
from jax.experimental import pallas as pl
from jax.experimental.pallas import tpu as pltpu

D_MODEL = 2048
BATCH = 4
SEQ = 2048
DEPTH = 4
DEC_BATCH = 128
DEC_SEQ = 8
PAST_LEN = 16384
HEAD_DIM = 128
CONV_WIDTH = 768
POOL_WIDTH = 512
CHUNK_WIDTH = 768
CHUNK = 128
N_CHUNK_HEADS = 6
POOL_WINDOWS = (2, 4, 8, 16)
POOL_GROUP_DIM = 128
POOL_BUF = 15
IN_WIDTH = 4352
N_EXPERT_GROUPS = 4
EXPERTS_PER_GROUP = 4
N_EXPERTS = 16
D_EXPERT = 512
ALPHA = (2 * DEPTH) ** 0.25
LN_EPS = 1e-5

N_PROMPT = BATCH * SEQ
N_SAMPLE = DEC_BATCH * DEC_SEQ
N_TOK = N_PROMPT + N_SAMPLE
N_ASSIGN = 2 * N_TOK

LANES = 128
SUBLANES = 8
CONV_HALO = 8
POOL_HALO = 16

TM = 256
TILES_PER_SEQ = SEQ // TM
N_PROMPT_TILES = N_PROMPT // TM
N_SAMPLE_TILES = N_SAMPLE // TM
N_TILES = N_PROMPT_TILES + N_SAMPLE_TILES
SEQS_PER_TILE = TM // DEC_SEQ

TE = 256
N_SLOTS = N_ASSIGN + N_EXPERTS * TE
N_ETILES = N_SLOTS // TE
ASSIGN_ROWS = N_ASSIGN // LANES

CUT_B, CUT_C, CUT_H, CUT_U, CUT_V, CUT_P = 0, 768, 1536, 2304, 3072, 3840

VMEM_LIMIT = 62 * 1024 * 1024

_f32 = jnp.float32
_bf16 = jnp.bfloat16


def _dot(a, b):
    return jnp.dot(a, b, preferred_element_type=_f32)


def _layer_norm(r, g, b):
    mu = jnp.mean(r, axis=-1, keepdims=True)
    c = r - mu
    var = jnp.mean(c * c, axis=-1, keepdims=True)
    return c * lax.rsqrt(var + LN_EPS) * g + b


class _RowGather:
    def __init__(self, start_row, first_row):
        self._start_row = start_row
        self._done = first_row

    def upto(self, row_end):
        for r in range(self._done, row_end):
            self._start_row(r)
        self._done = max(self._done, row_end)


class _NoGather:
    def upto(self, row_end):
        del row_end


BRANCH_GATHER_ROWS = 192


def _mixer_body(xb, w_in_ref, conv_w_ref, sg_ref, sb_ref, ws_ref, bias_ref, pw_ref, ps_ref,
                zext_ref, pext_ref, ybuf_ref, seq_len, pos, chunk_len, gather):
    n_seq = TM // seq_len

    z = _dot(xb, w_in_ref[:, CUT_C:CUT_H]) * _dot(xb, w_in_ref[:, CUT_H:CUT_U])
    zext_ref[:, CONV_HALO:, :] = z.reshape(n_seq, seq_len, CONV_WIDTH)
    gather.upto(16)
    cw = conv_w_ref[...]
    conv = (cw[0:1, :] * zext_ref[:, CONV_HALO - 2:CONV_HALO - 2 + seq_len, :]
            + cw[1:2, :] * zext_ref[:, CONV_HALO - 1:CONV_HALO - 1 + seq_len, :]
            + cw[2:3, :] * zext_ref[:, CONV_HALO:, :])
    y_conv = _dot(xb, w_in_ref[:, CUT_B:CUT_C]) * conv.reshape(TM, CONV_WIDTH)
    ybuf_ref[:, 0:CONV_WIDTH] = y_conv.astype(_bf16)
    gather.upto(32)

    v = _dot(xb, w_in_ref[:, CUT_V:CUT_P])
    vn = _layer_norm(v, sg_ref[...], sb_ref[...])
    vnb = vn.astype(_bf16)
    gather.upto(48)
    r_i = lax.broadcasted_iota(jnp.int32, (CHUNK, CHUNK), 0)
    c_i = lax.broadcasted_iota(jnp.int32, (CHUNK, CHUNK), 1)
    shift = chunk_len.bit_length() - 1
    mask = (c_i <= r_i) & ((r_i >> shift) == (c_i >> shift))
    u = _dot(xb, w_in_ref[:, CUT_U:CUT_V])
    gather.upto(64)
    n_chunks = TM // CHUNK
    for h in range(N_CHUNK_HEADS):
        m_h = jnp.where(mask, ws_ref[h], 0.0).astype(_bf16)
        b_h = bias_ref[:, h:h + 1]
        cols = slice(h * HEAD_DIM, (h + 1) * HEAD_DIM)
        vn_h = jnp.concatenate([vnb[c * CHUNK:(c + 1) * CHUNK, cols] for c in range(n_chunks)], axis=1)
        mixed = _dot(m_h, vn_h)
        for c in range(n_chunks):
            rows = slice(c * CHUNK, (c + 1) * CHUNK)
            ybuf_ref[rows, CONV_WIDTH + h * HEAD_DIM:CONV_WIDTH + (h + 1) * HEAD_DIM] = (
                u[rows, cols] * (mixed[:, c * HEAD_DIM:(c + 1) * HEAD_DIM] + b_h)).astype(_bf16)
        gather.upto(64 + 8 * (h + 1))

    p = _dot(xb, w_in_ref[:, CUT_P:IN_WIDTH])
    pext_ref[:, POOL_HALO:, :] = p.reshape(n_seq, seq_len, POOL_WIDTH)
    gather.upto(128)
    posf = (pos + 1).astype(_f32)
    for g, w in enumerate(POOL_WINDOWS):
        cols = slice(g * POOL_GROUP_DIM, (g + 1) * POOL_GROUP_DIM)
        s = pext_ref[:, POOL_HALO:, cols]
        for j in range(1, w):
            s = s + pext_ref[:, POOL_HALO - j:POOL_HALO - j + seq_len, cols]
        cnt = jnp.minimum(float(w), posf)
        diff = s.reshape(TM, POOL_GROUP_DIM) / cnt - p[:, cols]
        y_g = _dot(diff.astype(_bf16), pw_ref[g].astype(_bf16)) * ps_ref[:, cols]
        ybuf_ref[:, CONV_WIDTH + CHUNK_WIDTH + g * POOL_GROUP_DIM:
                 CONV_WIDTH + CHUNK_WIDTH + (g + 1) * POOL_GROUP_DIM] = y_g.astype(_bf16)
        gather.upto(128 + 16 * (g + 1))
    assert 128 + 16 * len(POOL_WINDOWS) == BRANCH_GATHER_ROWS
    return z, p, vn


def _route(x1, wr_ref, br_ref):
    x_hi = x1.astype(_bf16)
    x_lo = (x1 - x_hi.astype(_f32)).astype(_bf16)
    hi_parts = _dot(x_hi, wr_ref[...])
    logits = hi_parts[:, :LANES] + hi_parts[:, LANES:] + _dot(x_lo, wr_ref[:, :LANES]) + br_ref[...]
    lane = lax.broadcasted_iota(jnp.int32, logits.shape, 1)
    neg = -jnp.inf
    big = jnp.int32(1 << 20)

    gmask = lane < N_EXPERT_GROUPS
    gl = jnp.where(gmask, logits, neg)
    g_max = jnp.max(gl, axis=1, keepdims=True)
    g_idx = jnp.min(jnp.where(gmask & (gl == g_max), lane, big), axis=1, keepdims=True)
    g_val = 1.0 / jnp.sum(jnp.exp(gl - g_max), axis=1, keepdims=True)

    e_lane = lane - N_EXPERT_GROUPS
    emask = (e_lane >= 0) & (e_lane < N_EXPERTS) & ((e_lane >> (EXPERTS_PER_GROUP.bit_length() - 1)) == g_idx)
    el = jnp.where(emask, logits, neg)
    e_max = jnp.max(el, axis=1, keepdims=True)
    i1 = jnp.min(jnp.where(emask & (el == e_max), e_lane, big), axis=1, keepdims=True)
    e_sum = jnp.sum(jnp.exp(el - e_max), axis=1, keepdims=True)
    mask2 = emask & (e_lane != i1)
    e_second = jnp.max(jnp.where(mask2, el, neg), axis=1, keepdims=True)
    i2 = jnp.min(jnp.where(mask2 & (el == e_second), e_lane, big), axis=1, keepdims=True)
    v1 = 1.0 / e_sum
    v2 = jnp.exp(e_second - e_max) / e_sum
    tot = v1 + v2
    gate1 = g_val * (v1 / tot)
    gate2 = g_val * (v2 / tot)
    return jnp.where(lane == 0, i1.astype(_f32),
                     jnp.where(lane == 1, i2.astype(_f32),
                               jnp.where(lane == 2, gate1, jnp.where(lane == 3, gate2, 0.0))))


N_MIXER_COMMON_REFS = 29


def _mixer_kernel(fused, *refs):
    if fused:
        slot_ref, x1p_ref, routep_ref, y_hbm, g2_ref, b2_ref = refs[:6]
        refs = refs[6:]
    else:
        xp_ref, xsamp_ref = refs[:2]
        refs = refs[2:]
    (w_in_ref, w_out_ref, conv_w_ref, sg_ref, sb_ref, ws_p_ref, ws_s_ref, bias_p_ref, bias_s_ref,
     pw_ref, ps_ref, g1_ref, b1_ref, wr_ref, br_ref, zst_ref, pst_ref,
     x1_ref, route_ref, ztail_ref, ptail_ref, zs_ref, psamp_ref, vns_ref,
     zext_p, pext_p, zext_s, pext_s, ybuf_ref) = refs[:N_MIXER_COMMON_REFS]
    i = pl.program_id(0)
    last = pl.num_programs(0) - 1

    if fused:
        gbuf, sems = refs[N_MIXER_COMMON_REFS:]
        b = i % 2

        def row_starter(tile, half):
            tok0 = tile * TM

            def start_row(r):
                for k in range(2):
                    pltpu.make_async_copy(y_hbm.at[pl.ds(slot_ref[k * N_TOK + tok0 + r], 1)],
                                          gbuf.at[half, k, pl.ds(r, 1)], sems.at[half, k]).start(priority=k)
            return start_row

        def wait_rows(half):
            for k in range(2):
                pltpu.make_async_copy(y_hbm.at[pl.ds(0, TM)], gbuf.at[half, k], sems.at[half, k]).wait()

        @pl.when(i == 0)
        def _():
            first = row_starter(0, 0)

            def body(r, carry):
                first(r)
                return carry

            lax.fori_loop(0, TM, body, 0, unroll=16)

        wait_rows(b)
        routep = routep_ref[...]
        moe = routep[:, 2:3] * gbuf[b, 0] + routep[:, 3:4] * gbuf[b, 1]
        x = _layer_norm(ALPHA * x1p_ref[...] + moe, g2_ref[...], b2_ref[...])
        next_rows = row_starter(jnp.minimum(i + 1, last), 1 - b)
        new_gather = lambda first_row: _RowGather(next_rows, first_row)
    else:
        x = jnp.where(i < N_PROMPT_TILES, xp_ref[...], xsamp_ref[...])
        new_gather = lambda first_row: _NoGather()

    xb = x.astype(_bf16)
    row = lax.broadcasted_iota(jnp.int32, (TM, 1), 0)

    @pl.when(i < N_PROMPT_TILES)
    def _prompt():
        t0 = (i % TILES_PER_SEQ) * TM

        @pl.when(t0 == 0)
        def _():
            zext_p[:, 0:CONV_HALO, :] = jnp.zeros((1, CONV_HALO, CONV_WIDTH), _f32)
            pext_p[:, 0:POOL_HALO, :] = jnp.zeros((1, POOL_HALO, POOL_WIDTH), _f32)

        z, p, _ = _mixer_body(xb, w_in_ref, conv_w_ref, sg_ref, sb_ref, ws_p_ref, bias_p_ref,
                              pw_ref, ps_ref, zext_p, pext_p, ybuf_ref, TM, t0 + row, CHUNK, new_gather(0))
        z_last = z[TM - CONV_HALO:, :].reshape(1, CONV_HALO, CONV_WIDTH)
        p_last = p[TM - POOL_HALO:, :].reshape(1, POOL_HALO, POOL_WIDTH)
        zext_p[:, 0:CONV_HALO, :] = z_last
        pext_p[:, 0:POOL_HALO, :] = p_last
        ztail_ref[...] = z_last
        ptail_ref[...] = p_last

    @pl.when(i >= N_PROMPT_TILES)
    def _sample():
        zext_s[:, 0:CONV_HALO, :] = zst_ref[...]
        pext_s[:, 0:POOL_HALO, :] = pst_ref[...]
        z, p, vn = _mixer_body(xb, w_in_ref, conv_w_ref, sg_ref, sb_ref, ws_s_ref, bias_s_ref,
                               pw_ref, ps_ref, zext_s, pext_s, ybuf_ref, DEC_SEQ,
                               PAST_LEN + (row & (DEC_SEQ - 1)), DEC_SEQ, new_gather(0))
        zs_ref[...] = z.reshape(SEQS_PER_TILE, DEC_SEQ, CONV_WIDTH)
        psamp_ref[...] = p.reshape(SEQS_PER_TILE, DEC_SEQ, POOL_WIDTH)
        vns_ref[...] = vn.reshape(SEQS_PER_TILE, DEC_SEQ, CHUNK_WIDTH)
        ztail_ref[...] = jnp.zeros(ztail_ref.shape, _f32)
        ptail_ref[...] = jnp.zeros(ptail_ref.shape, _f32)

    tail_gather = new_gather(BRANCH_GATHER_ROWS)
    mix = _dot(ybuf_ref[...], w_out_ref[...])
    tail_gather.upto(BRANCH_GATHER_ROWS + (TM - BRANCH_GATHER_ROWS) // 2)
    x1 = _layer_norm(ALPHA * x + mix, g1_ref[...], b1_ref[...])
    x1_ref[...] = x1
    tail_gather.upto(TM)
    route_ref[...] = _route(x1, wr_ref, br_ref)

    if fused:
        @pl.when(i == last)
        def _():
            wait_rows(1 - b)


def _layer_spec(layer, shape):
    nd = len(shape)
    return pl.BlockSpec((None,) + shape, lambda i, *_: (layer,) + (0,) * nd, pipeline_mode=pl.Buffered(1))


def _mixer_call(layer, x_src, params):
    fused = len(x_src) == 6
    samp = lambda i, *_: (jnp.maximum(i - N_PROMPT_TILES, 0), 0, 0)
    samp_l = lambda i, *_: (layer, jnp.maximum(i - N_PROMPT_TILES, 0), 0, 0)
    tail = lambda i, *_: (i // TILES_PER_SEQ, 0, 0)
    tok = lambda i, *_: (i, 0)
    if fused:
        x_specs = [
            pl.BlockSpec((TM, D_MODEL), tok),
            pl.BlockSpec((TM, LANES), tok),
            pl.BlockSpec(memory_space=pl.ANY),
            _layer_spec(layer - 1, (1, D_MODEL)),
            _layer_spec(layer - 1, (1, D_MODEL)),
        ]
    else:
        x_specs = [
            pl.BlockSpec((TM, D_MODEL), lambda i, *_: (jnp.minimum(i, N_PROMPT_TILES - 1), 0)),
            pl.BlockSpec((TM, D_MODEL), lambda i, *_: (jnp.maximum(i - N_PROMPT_TILES, 0), 0)),
        ]
    in_specs = x_specs + [
        _layer_spec(layer, (D_MODEL, IN_WIDTH)),
        _layer_spec(layer, (D_MODEL, D_MODEL)),
        _layer_spec(layer, (3, CONV_WIDTH)),
        _layer_spec(layer, (1, CHUNK_WIDTH)),
        _layer_spec(layer, (1, CHUNK_WIDTH)),
        _layer_spec(layer, (N_CHUNK_HEADS, CHUNK, CHUNK)),
        _layer_spec(layer, (N_CHUNK_HEADS, CHUNK, CHUNK)),
        _layer_spec(layer, (CHUNK, N_CHUNK_HEADS)),
        _layer_spec(layer, (CHUNK, N_CHUNK_HEADS)),
        _layer_spec(layer, (4, POOL_GROUP_DIM, POOL_GROUP_DIM)),
        _layer_spec(layer, (1, POOL_WIDTH)),
        _layer_spec(layer, (1, D_MODEL)),
        _layer_spec(layer, (1, D_MODEL)),
        _layer_spec(layer, (D_MODEL, 2 * LANES)),
        _layer_spec(layer, (1, LANES)),
        pl.BlockSpec((None, SEQS_PER_TILE, CONV_HALO, CONV_WIDTH), samp_l),
        pl.BlockSpec((None, SEQS_PER_TILE, POOL_HALO, POOL_WIDTH), samp_l),
    ]
    out_specs = [
        pl.BlockSpec((TM, D_MODEL), tok),
        pl.BlockSpec((TM, LANES), tok),
        pl.BlockSpec((1, CONV_HALO, CONV_WIDTH), tail),
        pl.BlockSpec((1, POOL_HALO, POOL_WIDTH), tail),
        pl.BlockSpec((SEQS_PER_TILE, DEC_SEQ, CONV_WIDTH), samp),
        pl.BlockSpec((SEQS_PER_TILE, DEC_SEQ, POOL_WIDTH), samp),
        pl.BlockSpec((SEQS_PER_TILE, DEC_SEQ, CHUNK_WIDTH), samp),
    ]
    out_shape = [
        jax.ShapeDtypeStruct((N_TOK, D_MODEL), _f32),
        jax.ShapeDtypeStruct((N_TOK, LANES), _f32),
        jax.ShapeDtypeStruct((BATCH + 1, CONV_HALO, CONV_WIDTH), _f32),
        jax.ShapeDtypeStruct((BATCH + 1, POOL_HALO, POOL_WIDTH), _f32),
        jax.ShapeDtypeStruct((DEC_BATCH, DEC_SEQ, CONV_WIDTH), _f32),
        jax.ShapeDtypeStruct((DEC_BATCH, DEC_SEQ, POOL_WIDTH), _f32),
        jax.ShapeDtypeStruct((DEC_BATCH, DEC_SEQ, CHUNK_WIDTH), _f32),
    ]
    scratch = [
        pltpu.VMEM((1, CONV_HALO + TM, CONV_WIDTH), _f32),
        pltpu.VMEM((1, POOL_HALO + TM, POOL_WIDTH), _f32),
        pltpu.VMEM((SEQS_PER_TILE, CONV_HALO + DEC_SEQ, CONV_WIDTH), _f32),
        pltpu.VMEM((SEQS_PER_TILE, POOL_HALO + DEC_SEQ, POOL_WIDTH), _f32),
        pltpu.VMEM((TM, D_MODEL), _bf16),
    ]
    assert len(in_specs) - len(x_specs) + len(out_specs) + len(scratch) == N_MIXER_COMMON_REFS
    if fused:
        scratch += [pltpu.VMEM((2, 2, TM, D_MODEL), _f32), pltpu.SemaphoreType.DMA((2, 2))]
    return pl.pallas_call(
        functools.partial(_mixer_kernel, fused),
        grid_spec=pltpu.PrefetchScalarGridSpec(
            num_scalar_prefetch=1 if fused else 0,
            grid=(N_TILES,),
            in_specs=in_specs,
            out_specs=out_specs,
            scratch_shapes=scratch,
        ),
        out_shape=out_shape,
        compiler_params=pltpu.CompilerParams(
            dimension_semantics=("arbitrary",), vmem_limit_bytes=VMEM_LIMIT),
        name="mixer",
    )(*x_src, *params)


def _slots_kernel(ids_ref, slot_ref, tile_expert_ref, n_used_ref, count_ref, offset_ref):
    ids = ids_ref[...]
    lane = lax.broadcasted_iota(jnp.int32, (1, LANES), 1)
    counts = jnp.zeros((1, LANES), _f32)
    offsets = jnp.zeros((1, LANES), _f32)
    r_i = lax.broadcasted_iota(jnp.int32, (LANES, LANES), 0)
    c_i = lax.broadcasted_iota(jnp.int32, (LANES, LANES), 1)
    upper = (r_i < c_i).astype(_bf16)
    rr = lax.broadcasted_iota(jnp.int32, (ASSIGN_ROWS, ASSIGN_ROWS), 0)
    rc = lax.broadcasted_iota(jnp.int32, (ASSIGN_ROWS, ASSIGN_ROWS), 1)
    lower = (rc < rr).astype(_bf16)
    tile_row0 = (lax.broadcasted_iota(jnp.int32, (1, LANES), 1) * TE).astype(_f32)

    slot = jnp.zeros((ASSIGN_ROWS, LANES), _f32)
    tile_expert = jnp.zeros((1, LANES), _f32)
    off = jnp.zeros((1, 1), _f32)
    for e in range(N_EXPERTS):
        m = (ids == e).astype(_f32)
        within = _dot(m.astype(_bf16), upper)
        rowsum = jnp.sum(m, axis=1, keepdims=True)
        rowpre = _dot(lower, jnp.broadcast_to(rowsum, (ASSIGN_ROWS, LANES)).astype(_bf16))
        cnt = jnp.sum(rowsum, axis=0, keepdims=True)
        padded = jnp.floor((cnt + (TE - 1)) * (1.0 / TE)) * TE
        slot = slot + m * (off + rowpre + within)
        in_seg = (tile_row0 >= off) & (tile_row0 < off + padded)
        tile_expert = tile_expert + jnp.where(in_seg, float(e), 0.0)
        counts = counts + jnp.where(lane == e, cnt, 0.0)
        offsets = offsets + jnp.where(lane == e, off, 0.0)
        off = off + padded
    tile_expert = jnp.where(tile_row0 >= off, float(N_EXPERTS - 1), tile_expert)
    slot_ref[...] = slot.astype(jnp.int32)
    tile_expert_ref[...] = tile_expert.astype(jnp.int32)
    n_used_ref[...] = jnp.broadcast_to(off * (1.0 / TE), (1, LANES)).astype(jnp.int32)
    count_ref[...] = counts.astype(jnp.int32)
    offset_ref[...] = offsets.astype(jnp.int32)


def _slots_call(ids):
    return pl.pallas_call(
        _slots_kernel,
        out_shape=[jax.ShapeDtypeStruct((ASSIGN_ROWS, LANES), jnp.int32)]
        + [jax.ShapeDtypeStruct((1, LANES), jnp.int32)] * 4,
        name="slots",
    )(ids)


X_BUFS = 4
Y_BUFS = 3
W_SLOTS = 3


def _experts_kernel(layer, te_ref, nu_ref, count_ref, offset_ref, slot_ref,
                    x1_hbm, wg_hbm, wu_hbm, wd_hbm, y_hbm,
                    tok_of_slot, xbuf, ybuf, wg_st, wu_st, wd_st, wg_b, wu_b, wd_b, sem_x, sem_y, sem_w):
    n_used = nu_ref[0]
    tile_shift = TE.bit_length() - 1

    def tiles_of(expert):
        return (count_ref[expert] + (TE - 1)) >> tile_shift

    def weight_copies(expert, ws):
        return [pltpu.make_async_copy(src.at[layer, expert], dst.at[ws], sem_w.at[ws])
                for src, dst in ((wg_hbm, wg_st), (wu_hbm, wu_st), (wd_hbm, wd_st))]

    for c in weight_copies(te_ref[0], 0):
        c.start()
    second_first = tiles_of(te_ref[0])

    @pl.when(second_first < n_used)
    def _():
        for c in weight_copies(te_ref[second_first], 1):
            c.start()

    for k in range(2):
        def fill(t, carry, k=k):
            tok_of_slot[slot_ref[k * N_TOK + t]] = t
            return carry
        lax.fori_loop(0, N_TOK, fill, 0, unroll=16)
    for e in range(N_EXPERTS):
        first_pad = offset_ref[e] + count_ref[e]
        seg_end = offset_ref[e] + (((count_ref[e] + (TE - 1)) >> tile_shift) << tile_shift)

        def pad(s, carry):
            tok_of_slot[s] = 0
            return carry
        lax.fori_loop(first_pad, seg_end, pad, 0)

    def gather(tile, buf):
        base = tile * TE
        for r in range(TE):
            pltpu.make_async_copy(x1_hbm.at[pl.ds(tok_of_slot[base + r], 1)],
                                  xbuf.at[buf, pl.ds(r, 1)], sem_x.at[buf]).start(priority=r % 2)

    def y_copy(tile, buf):
        return pltpu.make_async_copy(ybuf.at[buf], y_hbm.at[pl.ds(tile * TE, TE)], sem_y.at[buf])

    gather(0, 0)
    for ahead in range(1, X_BUFS - 1):
        @pl.when(n_used > ahead)
        def _(ahead=ahead):
            gather(ahead, ahead)

    def tile_step(j, seg):
        expert = te_ref[j]
        first = (j == 0) | (te_ref[jnp.maximum(j - 1, 0)] != expert)

        @pl.when(j + (X_BUFS - 1) < n_used)
        def _():
            gather(j + (X_BUFS - 1), (j + (X_BUFS - 1)) % X_BUFS)

        @pl.when(first)
        def _():
            ws = seg % W_SLOTS
            for c in weight_copies(expert, ws):
                c.wait()
            wg_b[...] = wg_st[ws].astype(_bf16)
            wu_b[...] = wu_st[ws].astype(_bf16)
            wd_b[...] = wd_st[ws].astype(_bf16)
            next_first = j + tiles_of(expert)

            @pl.when(next_first < n_used)
            def _():
                after_next = next_first + tiles_of(te_ref[next_first])

                @pl.when(after_next < n_used)
                def _():
                    for c in weight_copies(te_ref[after_next], (seg + 2) % W_SLOTS):
                        c.start()

        xslot = j % X_BUFS
        pltpu.make_async_copy(x1_hbm.at[pl.ds(0, TE)], xbuf.at[xslot], sem_x.at[xslot]).wait()
        xb = xbuf[xslot].astype(_bf16)
        hg = _dot(xb, wg_b[...])
        hu = _dot(xb, wu_b[...])
        a = hg / (1.0 + jnp.exp(-hg)) * hu
        y = _dot(a.astype(_bf16), wd_b[...])

        yslot = j % Y_BUFS

        @pl.when(j >= Y_BUFS)
        def _():
            y_copy(j - Y_BUFS, yslot).wait()

        ybuf[yslot] = y
        y_copy(j, yslot).start()
        return seg + first.astype(jnp.int32)

    lax.fori_loop(0, n_used, tile_step, jnp.int32(0))

    for back in range(1, Y_BUFS + 1):
        @pl.when(n_used >= back)
        def _(back=back):
            y_copy(n_used - back, (n_used - back) % Y_BUFS).wait()

    ybuf[0] = jnp.zeros((TE, D_MODEL), _f32)

    def start_zero_tile(j, carry):
        y_copy(j, 0).start()
        return carry

    def wait_zero_tile(j, carry):
        y_copy(j, 0).wait()
        return carry

    lax.fori_loop(n_used, N_ETILES, start_zero_tile, 0)
    lax.fori_loop(n_used, N_ETILES, wait_zero_tile, 0)


def _experts_call(layer, tile_expert, n_used, counts, offsets, slot_flat, x1, wg, wu, wd):
    any_spec = pl.BlockSpec(memory_space=pl.ANY)
    n_prefetch = 5
    return pl.pallas_call(
        functools.partial(_experts_kernel, layer),
        grid_spec=pltpu.PrefetchScalarGridSpec(
            num_scalar_prefetch=n_prefetch,
            grid=(1,),
            in_specs=[any_spec] * 4,
            out_specs=any_spec,
            scratch_shapes=[
                pltpu.SMEM((N_SLOTS,), jnp.int32),
                pltpu.VMEM((X_BUFS, TE, D_MODEL), _f32),
                pltpu.VMEM((Y_BUFS, TE, D_MODEL), _f32),
                pltpu.VMEM((W_SLOTS, D_MODEL, D_EXPERT), _f32),
                pltpu.VMEM((W_SLOTS, D_MODEL, D_EXPERT), _f32),
                pltpu.VMEM((W_SLOTS, D_EXPERT, D_MODEL), _f32),
                pltpu.VMEM((D_MODEL, D_EXPERT), _bf16),
                pltpu.VMEM((D_MODEL, D_EXPERT), _bf16),
                pltpu.VMEM((D_EXPERT, D_MODEL), _bf16),
                pltpu.SemaphoreType.DMA((X_BUFS,)),
                pltpu.SemaphoreType.DMA((Y_BUFS,)),
                pltpu.SemaphoreType.DMA((W_SLOTS,)),
            ],
        ),
        out_shape=jax.ShapeDtypeStruct((N_SLOTS, D_MODEL), _f32),
        compiler_params=pltpu.CompilerParams(
            dimension_semantics=("arbitrary",), vmem_limit_bytes=VMEM_LIMIT),
        name="experts",
    )(tile_expert, n_used, counts, offsets, slot_flat, x1, wg, wu, wd)


def _combine_kernel(slot_ref, x1_ref, route_ref, g2_ref, b2_ref, y_hbm, out_p_ref, out_s_ref,
                    ybuf, sems):
    i = pl.program_id(0)
    b = i % 2

    def gather(tile, slot):
        tok0 = tile * TM

        def start(r, carry):
            for k in range(2):
                pltpu.make_async_copy(y_hbm.at[pl.ds(slot_ref[k * N_TOK + tok0 + r], 1)],
                                      ybuf.at[slot, k, pl.ds(r, 1)], sems.at[slot, k]).start(priority=k)
            return carry

        lax.fori_loop(0, TM, start, 0, unroll=16)

    @pl.when(i == 0)
    def _():
        gather(0, 0)

    @pl.when(i + 1 < pl.num_programs(0))
    def _():
        gather(i + 1, 1 - b)

    for k in range(2):
        pltpu.make_async_copy(y_hbm.at[pl.ds(0, TM)], ybuf.at[b, k], sems.at[b, k]).wait()

    route = route_ref[...]
    moe = route[:, 2:3] * ybuf[b, 0] + route[:, 3:4] * ybuf[b, 1]
    out = _layer_norm(ALPHA * x1_ref[...] + moe, g2_ref[...], b2_ref[...])

    @pl.when(i < N_PROMPT_TILES)
    def _():
        out_p_ref[...] = out

    @pl.when(i >= N_PROMPT_TILES)
    def _():
        out_s_ref[...] = out


def _combine_call(slot_flat, x1, route, g2, b2, y):
    tok = lambda i, s: (i, 0)
    const = lambda i, s: (0, 0)
    return pl.pallas_call(
        _combine_kernel,
        grid_spec=pltpu.PrefetchScalarGridSpec(
            num_scalar_prefetch=1,
            grid=(N_TILES,),
            in_specs=[
                pl.BlockSpec((TM, D_MODEL), tok),
                pl.BlockSpec((TM, LANES), tok),
                pl.BlockSpec((1, D_MODEL), const),
                pl.BlockSpec((1, D_MODEL), const),
                pl.BlockSpec(memory_space=pl.ANY),
            ],
            out_specs=[
                pl.BlockSpec((TM, D_MODEL), lambda i, s: (jnp.minimum(i, N_PROMPT_TILES - 1), 0)),
                pl.BlockSpec((TM, D_MODEL), lambda i, s: (jnp.maximum(i - N_PROMPT_TILES, 0), 0)),
            ],
            scratch_shapes=[
                pltpu.VMEM((2, 2, TM, D_MODEL), _f32),
                pltpu.SemaphoreType.DMA((2, 2)),
            ],
        ),
        out_shape=[jax.ShapeDtypeStruct((N_PROMPT, D_MODEL), _f32),
                   jax.ShapeDtypeStruct((N_SAMPLE, D_MODEL), _f32)],
        compiler_params=pltpu.CompilerParams(dimension_semantics=("arbitrary",)),
        name="combine",
    )(slot_flat, x1, route, g2, b2, y)


def kernel(x_prompt, x_sample, state_conv, state_pool, w_in, conv_w, sgu_ln_g, sgu_ln_b, sgu_w, sgu_b, pool_w, pool_scale, w_out, ln1_g, ln1_b, router_group_w, router_group_b, router_expert_w, router_expert_b, expert_w_gate, expert_w_up, expert_w_down, ln2_g, ln2_b):
    reps = CHUNK // DEC_SEQ
    row = lambda a: a.reshape(DEPTH, 1, -1)

    w_r = jnp.concatenate([router_group_w, router_expert_w], axis=2)
    w_r = jnp.pad(w_r, ((0, 0), (0, 0), (0, LANES - w_r.shape[2])))
    wr_hi = w_r.astype(_bf16)
    wr = jnp.concatenate([wr_hi, (w_r - wr_hi.astype(_f32)).astype(_bf16)], axis=2)
    b_r = jnp.concatenate([router_group_b, router_expert_b], axis=1)
    b_r = row(jnp.pad(b_r, ((0, 0), (0, LANES - b_r.shape[1]))))
    ws_s = jnp.tile(sgu_w[:, :, :DEC_SEQ, :DEC_SEQ], (1, 1, reps, reps))
    bias_p = jnp.swapaxes(sgu_b, 1, 2)
    bias_s = jnp.tile(bias_p[:, :DEC_SEQ], (1, reps, 1))
    zst = jnp.pad(state_conv, ((0, 0), (0, 0), (CONV_HALO - 2, 0), (0, 0)))
    pst = jnp.pad(state_pool, ((0, 0), (0, 0), (POOL_HALO - POOL_BUF, 0), (0, 0)))
    mixer_params = (w_in.astype(_bf16), w_out.astype(_bf16), conv_w, row(sgu_ln_g), row(sgu_ln_b), sgu_w,
                    ws_s, bias_p, bias_s, pool_w, row(pool_scale), row(ln1_g), row(ln1_b), wr, b_r, zst, pst)
    g2, b2 = row(ln2_g), row(ln2_b)

    ztails, ptails, zss, psamps, v_s = [], [], [], [], []
    x_src = (x_prompt.reshape(N_PROMPT, D_MODEL), x_sample.reshape(N_SAMPLE, D_MODEL))
    for l in range(DEPTH):
        x1, route, ztail, ptail, zs, psamp, vns = _mixer_call(l, x_src, mixer_params)
        ids = route[:, 0:2].astype(jnp.int32).T.reshape(ASSIGN_ROWS, LANES)
        slot, tile_expert, n_used, counts, offsets = _slots_call(ids)
        slot_flat = slot.reshape(N_ASSIGN)
        y = _experts_call(l, tile_expert.reshape(LANES), n_used.reshape(LANES)[:1], counts.reshape(LANES),
                          offsets.reshape(LANES), slot_flat, x1, expert_w_gate, expert_w_up, expert_w_down)
        x_src = (slot_flat, x1, route, y, g2, b2)
        ztails.append(ztail)
        ptails.append(ptail)
        zss.append(zs)
        psamps.append(psamp)
        v_s.append(vns)

    x_p, x_s = _combine_call(slot_flat, x1, route, g2[DEPTH - 1], b2[DEPTH - 1], y)
    y_prompt = x_p.reshape(BATCH, SEQ, D_MODEL)
    y_sample = x_s.reshape(DEC_BATCH, DEC_SEQ, D_MODEL)
    new_conv_prompt = jnp.stack(ztails)[:, :BATCH, CONV_HALO - 2:]
    new_pool_prompt = jnp.stack(ptails)[:, :BATCH, POOL_HALO - POOL_BUF:]
    new_conv_sample = jnp.stack(zss)[:, :, DEC_SEQ - 2:]
    new_pool_sample = jnp.concatenate([state_pool[:, :, DEC_SEQ:], jnp.stack(psamps)], axis=2)
    return (y_prompt, y_sample, new_conv_prompt, new_pool_prompt, new_conv_sample, new_pool_sample,
            jnp.stack(v_s))
```
